```python
import math
import jax, jax.numpy as jnp
from jax import lax
import numpy as np

D_MODEL = 1024
BATCH = 2
SEQ = 16384
DEPTH = 2

GRID_W = 64
RET_HEADS = 4
RET_DK = 128
RET_DV = 128
RET_CHUNK = 128
NA_HEADS = 8
NA_DH = 64
NA_KH = 8
NA_KW = 16
GQA_HEADS = 8
GQA_KV_HEADS = 2
GQA_DH = 64
Q_BLOCK = 128
ROPE_THETA = 10000.0
RMS_EPS = 1e-6
LN_EPS = 1e-5

RET_W = RET_HEADS * RET_DV
NA_W = NA_HEADS * NA_DH
GQA_W = GQA_HEADS * GQA_DH
GQA_KV_W = GQA_KV_HEADS * GQA_DH

IN_SEGMENTS = (
    ("ret_q", RET_HEADS * RET_DK, False),
    ("ret_k", RET_HEADS * RET_DK, False),
    ("ret_v", RET_W, True),
    ("ret_z", RET_W, False),
    ("na_q", NA_W, False),
    ("na_k", NA_W, False),
    ("na_v", NA_W, True),
    ("na_z", NA_W, False),
    ("gqa_q", GQA_W, False),
    ("gqa_k", GQA_KV_W, False),
    ("gqa_v", GQA_KV_W, True),
    ("gqa_z", GQA_W, False),
    ("gate_a", D_MODEL, False),
    ("gate_b", D_MODEL, False),
    ("gate_c", D_MODEL, False),
)
D_IN = sum(s for _, s, _ in IN_SEGMENTS)

kernel_name = "hybrid_retention_natten_gqa_encoder"


def _dense(key, shape, fan_in, scale=1.0):
    return jax.random.normal(key, shape, jnp.float32) * (scale * fan_in ** -0.5)


def _split_columns(p):
    idx = []
    acc = 0
    for _, size, _ in IN_SEGMENTS[:-1]:
        acc += size
        idx.append(acc)
    return jnp.split(p, idx, axis=-1)


def _layer_norm(x, g, b):
    xf = x.astype(jnp.float32)
    mu = jnp.mean(xf, -1, keepdims=True)
    var = jnp.mean(jnp.square(xf - mu), -1, keepdims=True)
    return ((xf - mu) * lax.rsqrt(var + LN_EPS) * g.astype(jnp.float32) + b.astype(jnp.float32)).astype(x.dtype)


def _rms_norm(x, g):
    xf = x.astype(jnp.float32)
    return (xf * lax.rsqrt(jnp.mean(jnp.square(xf), -1, keepdims=True) + RMS_EPS) * g.astype(jnp.float32)).astype(x.dtype)


def _apply_rope(x, cos, sin):
    half = x.shape[-1] // 2
    x1 = x[..., :half].astype(jnp.float32)
    x2 = x[..., half:].astype(jnp.float32)
    return jnp.concatenate([x1 * cos - x2 * sin, x2 * cos + x1 * sin], -1).astype(x.dtype)


def _retention_bidir(q, k, v, theta_f, theta_b):
    B_, L, H, dk = q.shape
    dv = v.shape[-1]
    C = RET_CHUNK
    N = L // C
    dt = q.dtype
    lf = jax.nn.log_sigmoid(theta_f.astype(jnp.float32))
    lb = jax.nn.log_sigmoid(theta_b.astype(jnp.float32))
    pos = jnp.arange(C, dtype=jnp.float32)
    diff = pos[:, None] - pos[None, :]
    d_intra = jnp.where(diff >= 0,
                        jnp.exp(jnp.maximum(diff, 0.0) * lf[:, None, None]),
                        jnp.exp(jnp.maximum(-diff, 0.0) * lb[:, None, None]))
    w_kf = jnp.exp((C - 1 - pos)[None, :] * lf[:, None])
    w_qf = jnp.exp((pos + 1.0)[None, :] * lf[:, None])
    w_kb = jnp.exp(pos[None, :] * lb[:, None])
    w_qb = jnp.exp((C - pos)[None, :] * lb[:, None])
    chunk_f = jnp.exp(C * lf)[None, :, None, None]
    chunk_b = jnp.exp(C * lb)[None, :, None, None]

    qc = (q * (dk ** -0.5)).reshape(B_, N, C, H, dk)
    kc = k.reshape(B_, N, C, H, dk)
    vc = v.reshape(B_, N, C, H, dv)
    s = jnp.einsum('bnihd,bnjhd->bnhij', qc, kc) * d_intra.astype(dt)
    o = jnp.einsum('bnhij,bnjhe->bnihe', s, vc).astype(jnp.float32)
    kv_f = jnp.einsum('bnjhd,hj,bnjhe->nbhde', kc, w_kf.astype(dt), vc).astype(jnp.float32)
    kv_b = jnp.einsum('bnjhd,hj,bnjhe->nbhde', kc, w_kb.astype(dt), vc).astype(jnp.float32)

    def step_f(S, kv):
        return chunk_f * S + kv, S

    def step_b(R, kv):
        return chunk_b * R + kv, R

    zeros = jnp.zeros((B_, H, dk, dv), jnp.float32)
    _, states_f = lax.scan(step_f, zeros, kv_f)
    _, states_b = lax.scan(step_b, zeros, kv_b, reverse=True)
    qf = qc.astype(jnp.float32)
    o = o + jnp.einsum('bnihd,hi,nbhde->bnihe', qf, w_qf, states_f)
    o = o + jnp.einsum('bnihd,hi,nbhde->bnihe', qf, w_qb, states_b)
    return o.reshape(B_, L, H, dv).astype(dt)


def _neighbourhood_attention(q, k, v, rpb):
    B_, L, H, d = q.shape
    rows = L // GRID_W
    kh = min(NA_KH, rows)
    rows_per_block = Q_BLOCK // GRID_W
    n_blocks = rows // rows_per_block
    qg = (q * (d ** -0.5)).reshape(B_, rows, GRID_W, H, d)
    kg = k.reshape(B_, rows, GRID_W, H, d)
    vg = v.reshape(B_, rows, GRID_W, H, d)
    col = jnp.arange(GRID_W)
    col_start = jnp.clip(col - NA_KW // 2, 0, GRID_W - NA_KW)
    col_in = (col[None, :] >= col_start[:, None]) & (col[None, :] < col_start[:, None] + NA_KW)
    col_idx = jnp.clip(col[None, :] - col[:, None] + NA_KW - 1, 0, 2 * NA_KW - 2)
    mask = col_in[:, None, :]
    neg = jnp.finfo(jnp.float32).min

    def block(i):
        r = i * rows_per_block + jnp.arange(rows_per_block)
        r_start = jnp.clip(r - kh // 2, 0, rows - kh)
        key_rows = r_start[:, None] + jnp.arange(kh)[None, :]
        qb = lax.dynamic_slice_in_dim(qg, i * rows_per_block, rows_per_block, axis=1)
        kb = kg[:, key_rows]
        vb = vg[:, key_rows]
        row_idx = key_rows - r[:, None] + NA_KH - 1
        bias = rpb[:, row_idx][..., col_idx]
        bias = jnp.transpose(bias, (0, 1, 3, 2, 4)).astype(jnp.float32)
        s = jnp.einsum('brqhd,brkwhd->bhrqkw', qb, kb).astype(jnp.float32) + bias[None]
        s = jnp.where(mask, s, neg)
        sh = s.shape
        p = jax.nn.softmax(s.reshape(sh[:-2] + (sh[-2] * sh[-1],)), axis=-1).reshape(sh).astype(v.dtype)
        return jnp.einsum('bhrqkw,brkwhd->brqhd', p, vb)

    o = lax.map(block, jnp.arange(n_blocks))
    return jnp.moveaxis(o, 0, 1).reshape(B_, L, H * d)


def _gqa_attention(q, k, v):
    B_, L, Hq, d = q.shape
    Hkv = k.shape[2]
    G = Hq // Hkv
    qb = (q * (d ** -0.5)).reshape(B_, L // Q_BLOCK, Q_BLOCK, Hkv, G, d)
    qb = jnp.moveaxis(qb, 1, 0)

    def block(qi):
        s = jnp.einsum('bqngd,bknd->bngqk', qi, k).astype(jnp.float32)
        p = jax.nn.softmax(s, axis=-1).astype(v.dtype)
        return jnp.einsum('bngqk,bknd->bqngd', p, v)

    o = lax.map(block, qb)
    return jnp.moveaxis(o, 0, 1).reshape(B_, L, Hq * d)


def _hybrid_layer(x, w_in, theta_f, theta_b, gn_gain, rpb, q_norm, k_norm,
                  w_a, w_b, w_c, w_out, ln_g, ln_b, ret_cs, ax_cs, alpha):
    B_, L, _ = x.shape
    p = jnp.einsum('bld,de->ble', x, w_in)
    (rq, rk, rv, rz, nq, nk, nv, nz, cq, ck, cv, cz, ga, gb, gc) = _split_columns(p)

    rq = _apply_rope(rq.reshape(B_, L, RET_HEADS, RET_DK), *ret_cs)
    rk = _apply_rope(rk.reshape(B_, L, RET_HEADS, RET_DK), *ret_cs)
    ra = _retention_bidir(rq, rk, rv.reshape(B_, L, RET_HEADS, RET_DV), theta_f, theta_b)
    raf = ra.astype(jnp.float32)
    mu = jnp.mean(raf, -1, keepdims=True)
    var = jnp.mean(jnp.square(raf - mu), -1, keepdims=True)
    ra = ((raf - mu) * lax.rsqrt(var + LN_EPS)).reshape(B_, L, RET_W) * gn_gain.astype(jnp.float32)
    o_a = ra.astype(x.dtype) * jax.nn.silu(rz)

    o_b = _neighbourhood_attention(nq.reshape(B_, L, NA_HEADS, NA_DH), nk.reshape(B_, L, NA_HEADS, NA_DH),
                                   nv.reshape(B_, L, NA_HEADS, NA_DH), rpb) * jax.nn.silu(nz)

    cq = _apply_rope(_rms_norm(cq.reshape(B_, L, GQA_HEADS, GQA_DH), q_norm), *ax_cs)
    ck = _apply_rope(_rms_norm(ck.reshape(B_, L, GQA_KV_HEADS, GQA_DH), k_norm), *ax_cs)
    o_c = _gqa_attention(cq, ck, cv.reshape(B_, L, GQA_KV_HEADS, GQA_DH)) * jax.nn.silu(cz)

    y = (jax.nn.sigmoid(ga) * (o_a @ w_a)
         + jax.nn.sigmoid(gb) * (o_b @ w_b)
         + jax.nn.sigmoid(gc) * (o_c @ w_c))
    out = y @ w_out
    return _layer_norm(alpha * x + out, ln_g, ln_b)


def setup_inputs(seed: int = 0) -> dict:
    key = jax.random.key(seed)
    ks = jax.random.split(key, 16)
    beta = (8.0 * DEPTH) ** -0.25
    x = jax.random.normal(ks[0], (BATCH, SEQ, D_MODEL), jnp.float32)
    seg_keys = jax.random.split(ks[1], len(IN_SEGMENTS))
    parts = [_dense(sk, (DEPTH, D_MODEL, size), D_MODEL, beta if is_v else 1.0)
             for sk, (_, size, is_v) in zip(seg_keys, IN_SEGMENTS)]
    w_in = jnp.concatenate(parts, axis=-1)
    eps = 2.0 ** (-5.0 - jnp.arange(RET_HEADS, dtype=jnp.float32))
    theta0 = jnp.log((1.0 - eps) / eps)
    ret_theta_fwd = theta0[None] + 0.1 * jax.random.normal(ks[2], (DEPTH, RET_HEADS), jnp.float32)
    ret_theta_bwd = theta0[None] + 0.1 * jax.random.normal(ks[3], (DEPTH, RET_HEADS), jnp.float32)
    ret_gn_gain = 1.0 + 0.02 * jax.random.normal(ks[4], (DEPTH, RET_W), jnp.float32)
    na_rpb = 0.02 * jax.random.normal(ks[5], (DEPTH, NA_HEADS, 2 * NA_KH - 1, 2 * NA_KW - 1), jnp.float32)
    gqa_q_norm = 1.0 + 0.02 * jax.random.normal(ks[6], (DEPTH, GQA_DH), jnp.float32)
    gqa_k_norm = 1.0 + 0.02 * jax.random.normal(ks[7], (DEPTH, GQA_DH), jnp.float32)
    w_branch_a = _dense(ks[8], (DEPTH, RET_W, D_MODEL), RET_W, beta)
    w_branch_b = _dense(ks[9], (DEPTH, NA_W, D_MODEL), NA_W, beta)
    w_branch_c = _dense(ks[10], (DEPTH, GQA_W, D_MODEL), GQA_W, beta)
    w_out = _dense(ks[11], (DEPTH, D_MODEL, D_MODEL), D_MODEL, beta)
    ln_gain = 1.0 + 0.02 * jax.random.normal(ks[12], (DEPTH, D_MODEL), jnp.float32)
    ln_bias = 0.02 * jax.random.normal(ks[13], (DEPTH, D_MODEL), jnp.float32)
    return {"x": x, "w_in": w_in, "ret_theta_fwd": ret_theta_fwd, "ret_theta_bwd": ret_theta_bwd,
            "ret_gn_gain": ret_gn_gain, "na_rpb": na_rpb, "gqa_q_norm": gqa_q_norm,
            "gqa_k_norm": gqa_k_norm, "w_branch_a": w_branch_a, "w_branch_b": w_branch_b,
            "w_branch_c": w_branch_c, "w_out": w_out, "ln_gain": ln_gain, "ln_bias": ln_bias}


def reference(x, w_in, ret_theta_fwd, ret_theta_bwd, ret_gn_gain, na_rpb, gqa_q_norm,
              gqa_k_norm, w_branch_a, w_branch_b, w_branch_c, w_out, ln_gain, ln_bias):
    L = x.shape[1]
    alpha = (2.0 * DEPTH) ** 0.25
    t = jnp.arange(L)
    tf = t.astype(jnp.float32)
    row = (t // GRID_W).astype(jnp.float32)
    col = (t % GRID_W).astype(jnp.float32)
    inv_ret = 1.0 / (ROPE_THETA ** jnp.linspace(0.0, 1.0, RET_DK // 2, dtype=jnp.float32))
    ang_ret = tf[:, None] * inv_ret[None, :]
    ret_cs = (jnp.cos(ang_ret)[:, None, :], jnp.sin(ang_ret)[:, None, :])
    n_ax = GQA_DH // 4
    inv_ax = ROPE_THETA ** (-jnp.arange(n_ax, dtype=jnp.float32) / n_ax)
    ang_ax = jnp.concatenate([row[:, None] * inv_ax[None, :], col[:, None] * inv_ax[None, :]], -1)
    ax_cs = (jnp.cos(ang_ax)[:, None, :], jnp.sin(ang_ax)[:, None, :])
    for l in range(DEPTH):
        x = _hybrid_layer(x, w_in[l], ret_theta_fwd[l], ret_theta_bwd[l], ret_gn_gain[l], na_rpb[l],
                          gqa_q_norm[l], gqa_k_norm[l], w_branch_a[l], w_branch_b[l], w_branch_c[l],
                          w_out[l], ln_gain[l], ln_bias[l], ret_cs, ax_cs, alpha)
    return x
```

```python
import functools
import math

import numpy as np
import jax
import jax.numpy as jnp
from jax import lax
from jax.experimental import pallas as pl
from jax.experimental.pallas import tpu as pltpu

D_MODEL = 1024
GRID_W = 64
RET_HEADS = 4
RET_DK = 128
RET_CHUNK = 128
NA_HEADS = 8
NA_DH = 64
NA_KH = 8
NA_KW = 16
GQA_HEADS = 8
GQA_KV_HEADS = 2
GQA_DH = 64
ROPE_THETA = 10000.0
RMS_EPS = 1e-6
LN_EPS = 1e-5

RET_W = RET_HEADS * RET_DK
NA_W = NA_HEADS * NA_DH
GQA_W = GQA_HEADS * GQA_DH
GQA_KV_W = GQA_KV_HEADS * GQA_DH

OFF_RET = 0
OFF_NA = 4 * RET_W
OFF_GQA = OFF_NA + 4 * NA_W
OFF_GATES = OFF_GQA + 2 * GQA_W + 2 * GQA_KV_W
D_IN = OFF_GATES + 3 * D_MODEL

LANES = 128
VMEM_LIMIT = 56 * 1024 * 1024

BF16 = jnp.bfloat16
F32 = jnp.float32


def _dot(a, b):
    return jnp.dot(a, b, preferred_element_type=F32)


def _dot_nt(a, b):
    return lax.dot_general(a, b, (((1,), (1,)), ((), ())), preferred_element_type=F32)


def _dot_tn(a, b):
    return lax.dot_general(a, b, (((0,), (0,)), ((), ())), preferred_element_type=F32)


def _silu(z):
    return z / (1.0 + jnp.exp(-z))


def _sigmoid(z):
    return 1.0 / (1.0 + jnp.exp(-z))


def _params(sem):
    return pltpu.CompilerParams(dimension_semantics=sem, vmem_limit_bytes=VMEM_LIMIT)


def _proj_ret_kernel(x_ref, w_ref, cos_ref, sin_ref, q_ref, k_ref, v_ref, z_ref):
    xb = x_ref[...].astype(BF16)
    cos = cos_ref[...]
    sin = sin_ref[...]
    scale = RET_DK ** -0.5
    yq = _dot(xb, w_ref[:, 0:RET_W])
    yk = _dot(xb, w_ref[:, RET_W:2 * RET_W])
    for h in range(RET_HEADS):
        sl = slice(h * RET_DK, (h + 1) * RET_DK)
        t = yq[:, sl]
        q_ref[:, sl] = ((t * cos + pltpu.roll(t, RET_DK // 2, 1) * sin) * scale).astype(BF16)
        t = yk[:, sl]
        k_ref[:, sl] = (t * cos + pltpu.roll(t, RET_DK // 2, 1) * sin).astype(BF16)
    v_ref[...] = _dot(xb, w_ref[:, 2 * RET_W:3 * RET_W]).astype(BF16)
    z_ref[...] = _silu(_dot(xb, w_ref[:, 3 * RET_W:4 * RET_W]))


def _proj_na_kernel(x_ref, w_ref, q_ref, k_ref, v_ref, z_ref):
    xb = x_ref[...].astype(BF16)
    q_ref[...] = (_dot(xb, w_ref[:, 0:NA_W]) * (NA_DH ** -0.5)).astype(BF16)
    k_ref[...] = _dot(xb, w_ref[:, NA_W:2 * NA_W]).astype(BF16)
    v_ref[...] = _dot(xb, w_ref[:, 2 * NA_W:3 * NA_W]).astype(BF16)
    z_ref[...] = _silu(_dot(xb, w_ref[:, 3 * NA_W:4 * NA_W]))


def _group_mean_sq(y, g_ref, width):
    sq = y * y
    hi = sq.astype(BF16)
    lo = (sq - hi.astype(F32)).astype(BF16)
    g = g_ref[0:width, 0:width]
    return (_dot(hi, g) + _dot(lo, g)) * (1.0 / GQA_DH)


def _rope64(t, cos, sin, first_half):
    swapped = jnp.where(first_half, pltpu.roll(t, LANES - GQA_DH // 2, 1), pltpu.roll(t, GQA_DH // 2, 1))
    return t * cos + swapped * sin


def _proj_gqa_kernel(x_ref, w_ref, cos_ref, sin_ref, g_ref, qn_ref, kn_ref,
                     q_ref, k_ref, v_ref, z_ref):
    xb = x_ref[...].astype(BF16)
    cos = cos_ref[...]
    sin = sin_ref[...]
    lane = lax.broadcasted_iota(jnp.int32, cos.shape, 1)
    first_half = (lane % GQA_DH) < (GQA_DH // 2)
    yq = _dot(xb, w_ref[:, 0:GQA_W])
    qn = yq * lax.rsqrt(_group_mean_sq(yq, g_ref, GQA_W) + RMS_EPS) * qn_ref[...]
    for j in range(GQA_W // LANES):
        sl = slice(j * LANES, (j + 1) * LANES)
        q_ref[:, sl] = (_rope64(qn[:, sl], cos, sin, first_half) * (GQA_DH ** -0.5)).astype(BF16)
    yk = _dot(xb, w_ref[:, GQA_W:GQA_W + GQA_KV_W])
    kn = yk * lax.rsqrt(_group_mean_sq(yk, g_ref, GQA_KV_W) + RMS_EPS) * kn_ref[...]
    k_ref[...] = _rope64(kn, cos, sin, first_half).astype(BF16)
    v_ref[...] = _dot(xb, w_ref[:, GQA_W + GQA_KV_W:GQA_W + 2 * GQA_KV_W]).astype(BF16)
    z_ref[...] = _silu(_dot(xb, w_ref[:, GQA_W + 2 * GQA_KV_W:2 * GQA_W + 2 * GQA_KV_W]))


def _row_spec(tm, width):
    return pl.BlockSpec((tm, width), lambda i: (i, 0))


def _const_spec(shape):
    return pl.BlockSpec(shape, lambda i: (0,) * len(shape))


def _table_spec(tm, seq):
    nblk = seq // tm
    return pl.BlockSpec((tm, LANES), lambda i: (i % nblk, 0))


def _proj_ret(x2, w, cos, sin, seq, tm):
    m = x2.shape[0]
    out = [jax.ShapeDtypeStruct((m, RET_W), BF16)] * 3 + [jax.ShapeDtypeStruct((m, RET_W), F32)]
    return pl.pallas_call(
        _proj_ret_kernel, out_shape=out, grid=(m // tm,),
        in_specs=[_row_spec(tm, D_MODEL), _const_spec(w.shape), _table_spec(tm, seq), _table_spec(tm, seq)],
        out_specs=[_row_spec(tm, RET_W)] * 4,
        compiler_params=_params(("parallel",)), name="proj_ret")(x2, w, cos, sin)


def _proj_na(x2, w, tm):
    m = x2.shape[0]
    out = [jax.ShapeDtypeStruct((m, NA_W), BF16)] * 3 + [jax.ShapeDtypeStruct((m, NA_W), F32)]
    return pl.pallas_call(
        _proj_na_kernel, out_shape=out, grid=(m // tm,),
        in_specs=[_row_spec(tm, D_MODEL), _const_spec(w.shape)],
        out_specs=[_row_spec(tm, NA_W)] * 4,
        compiler_params=_params(("parallel",)), name="proj_na")(x2, w)


def _proj_gqa(x2, w, cos, sin, gmat, qn, kn, seq, tm):
    m = x2.shape[0]
    out = [jax.ShapeDtypeStruct((m, GQA_W), BF16), jax.ShapeDtypeStruct((m, GQA_KV_W), BF16),
           jax.ShapeDtypeStruct((m, GQA_KV_W), BF16), jax.ShapeDtypeStruct((m, GQA_W), F32)]
    return pl.pallas_call(
        _proj_gqa_kernel, out_shape=out, grid=(m // tm,),
        in_specs=[_row_spec(tm, D_MODEL), _const_spec(w.shape), _table_spec(tm, seq), _table_spec(tm, seq),
                  _const_spec(gmat.shape), _const_spec(qn.shape), _const_spec(kn.shape)],
        out_specs=[_row_spec(tm, GQA_W), _row_spec(tm, GQA_KV_W), _row_spec(tm, GQA_KV_W), _row_spec(tm, GQA_W)],
        compiler_params=_params(("parallel",)), name="proj_gqa")(x2, w, cos, sin, gmat, qn, kn)


def _retention_kernel(dec_ref, qf_ref, kf_ref, vf_ref, qb_ref, kb_ref, vb_ref,
                      of_ref, ob_ref, sf_ref, sb_ref, *, chunks):
    c = RET_CHUNK

    @pl.when(pl.program_id(2) == 0)
    def _():
        sf_ref[...] = jnp.zeros_like(sf_ref)
        sb_ref[...] = jnp.zeros_like(sb_ref)

    lf = dec_ref[0:1, :]
    lb = dec_ref[1:2, :]
    row = lax.broadcasted_iota(jnp.int32, (c, c), 0).astype(F32)
    col = lax.broadcasted_iota(jnp.int32, (c, c), 1).astype(F32)
    diff = row - col
    d_intra = jnp.where(diff >= 0, jnp.exp(jnp.maximum(diff, 0.0) * lf), jnp.exp(jnp.maximum(-diff, 0.0) * lb))
    w_kf = jnp.exp((c - 1.0 - row) * lf)
    w_qf = jnp.exp((row + 1.0) * lf)
    w_kb = jnp.exp(row * lb)
    w_qb = jnp.exp((c - row) * lb)
    chunk_f = jnp.exp(c * lf)
    chunk_b = jnp.exp(c * lb)

    s_f = sf_ref[...]
    for i in range(chunks):
        rows = slice(i * c, (i + 1) * c)
        q = qf_ref[rows, :]
        k = kf_ref[rows, :]
        v = vf_ref[rows, :]
        s = (_dot_nt(q, k) * d_intra).astype(BF16)
        o = _dot(s, v)
        qw = (q.astype(F32) * w_qf).astype(BF16)
        o = o + _dot(qw, s_f.astype(BF16))
        of_ref[rows, :] = o
        kw = (k.astype(F32) * w_kf).astype(BF16)
        s_f = chunk_f * s_f + _dot_tn(kw, v)
    sf_ref[...] = s_f

    s_b = sb_ref[...]
    for i in reversed(range(chunks)):
        rows = slice(i * c, (i + 1) * c)
        q = qb_ref[rows, :]
        k = kb_ref[rows, :]
        v = vb_ref[rows, :]
        qw = (q.astype(F32) * w_qb).astype(BF16)
        ob_ref[rows, :] = _dot(qw, s_b.astype(BF16))
        kw = (k.astype(F32) * w_kb).astype(BF16)
        s_b = chunk_b * s_b + _dot_tn(kw, v)
    sb_ref[...] = s_b


def _retention(rq, rk, rv, decay, batch, seq, chunks):
    m = rq.shape[0]
    rows = chunks * RET_CHUNK
    ng = seq // rows
    fwd = pl.BlockSpec((rows, RET_DK), lambda b, h, g: (b * ng + g, h))
    bwd = pl.BlockSpec((rows, RET_DK), lambda b, h, g: (b * ng + ng - 1 - g, h))
    dec = pl.BlockSpec((None, 8, LANES), lambda b, h, g: (h, 0, 0))
    out = [jax.ShapeDtypeStruct((m, RET_W), F32)] * 2
    return pl.pallas_call(
        functools.partial(_retention_kernel, chunks=chunks), out_shape=out,
        grid=(batch, RET_HEADS, ng),
        in_specs=[dec, fwd, fwd, fwd, bwd, bwd, bwd],
        out_specs=[fwd, bwd],
        scratch_shapes=[pltpu.VMEM((RET_DK, RET_DK), F32), pltpu.VMEM((RET_DK, RET_DK), F32)],
        compiler_params=_params(("parallel", "parallel", "arbitrary")),
        name="retention")(decay, rq, rk, rv, rq, rk, rv)


NA_QROWS = 4
NA_KROWS = 3 * NA_QROWS
NA_TQ = NA_QROWS * GRID_W
NA_TK = NA_KROWS * GRID_W


def _na_kernel(q_ref, k0_ref, k1_ref, k2_ref, v0_ref, v1_ref, v2_ref, bias_ref, z_ref, o_ref):
    q = q_ref[...]
    k = jnp.concatenate([k0_ref[...], k1_ref[...], k2_ref[...]], axis=0)
    v = jnp.concatenate([v0_ref[...], v1_ref[...], v2_ref[...]], axis=0)
    lane = lax.broadcasted_iota(jnp.int32, k.shape, 1)
    zero = jnp.zeros_like(k)
    acc = None
    for half in range(2):
        own = (lane < NA_DH) if half == 0 else (lane >= NA_DH)
        s = _dot_nt(q, jnp.where(own, k, zero)) + bias_ref[half]
        mx = jnp.max(s, axis=-1, keepdims=True)
        p = jnp.exp(s - mx)
        den = jnp.sum(p, axis=-1, keepdims=True)
        o = _dot(p.astype(BF16), jnp.where(own, v, zero)) / den
        acc = o if acc is None else acc + o
    o_ref[...] = (acc * z_ref[...]).astype(o_ref.dtype)


def _na_bias_tables(rpb, rows):
    nblk = rows // NA_QROWS
    tables = []
    ql = np.arange(NA_TQ) // GRID_W
    qc = np.arange(NA_TQ) % GRID_W
    kl = np.arange(NA_TK) // GRID_W
    kc = np.arange(NA_TK) % GRID_W
    for blk in (0, 1, nblk - 1):
        first_kblk = min(max(blk - 1, 0), nblk - 3)
        qr = blk * NA_QROWS + ql
        kr = first_kblk * NA_QROWS + kl
        r_start = np.clip(qr - NA_KH // 2, 0, rows - NA_KH)
        row_ok = (kr[None, :] >= r_start[:, None]) & (kr[None, :] < r_start[:, None] + NA_KH)
        c_start = np.clip(qc - NA_KW // 2, 0, GRID_W - NA_KW)
        col_ok = (kc[None, :] >= c_start[:, None]) & (kc[None, :] < c_start[:, None] + NA_KW)
        ridx = np.clip(kr[None, :] - qr[:, None] + NA_KH - 1, 0, 2 * NA_KH - 2)
        cidx = np.clip(kc[None, :] - qc[:, None] + NA_KW - 1, 0, 2 * NA_KW - 2)
        vals = rpb[:, ridx, cidx].astype(F32)
        tables.append(jnp.where((row_ok & col_ok)[None], vals, -jnp.inf))
    return jnp.stack(tables)


def _na_attention(nq, nk, nv, nz, bias, batch, seq):
    m = nq.shape[0]
    rows = seq // GRID_W
    nblk = rows // NA_QROWS
    pairs = NA_HEADS // 2

    def kblk(j):
        return lambda b, hp, i: (b * nblk + jnp.clip(i - 1, 0, nblk - 3) + j, hp)

    def bias_idx(b, hp, i):
        cls = jnp.where(i == 0, 0, jnp.where(i == nblk - 1, 2, 1))
        return (cls, hp, 0, 0)

    qspec = pl.BlockSpec((NA_TQ, LANES), lambda b, hp, i: (b * nblk + i, hp))
    kspecs = [pl.BlockSpec((NA_TQ, LANES), kblk(j)) for j in range(3)]
    return pl.pallas_call(
        _na_kernel, out_shape=jax.ShapeDtypeStruct((m, NA_W), BF16),
        grid=(batch, pairs, nblk),
        in_specs=[qspec] + kspecs + kspecs + [pl.BlockSpec((None, 2, NA_TQ, NA_TK), bias_idx), qspec],
        out_specs=qspec,
        compiler_params=_params(("parallel", "parallel", "arbitrary")),
        name="na_attention")(nq, nk, nk, nk, nv, nv, nv, bias, nz)


GQA_GROUP = GQA_HEADS // GQA_KV_HEADS


def _gqa_kernel(q_ref, kt_ref, v_ref, z_ref, o_ref, m_ref, acc_ref, *, tk):
    tq = q_ref.shape[0]
    nk = kt_ref.shape[1] // tk
    m_ref[...] = jnp.full(m_ref.shape, -jnp.inf, F32)
    acc_ref[...] = jnp.zeros_like(acc_ref)
    qs = [q_ref[:, g * GQA_DH:(g + 1) * GQA_DH] for g in range(GQA_GROUP)]

    def body(c, carry):
        start = pl.multiple_of(c * tk, tk)
        kc = kt_ref[:, pl.ds(start, tk)]
        vc = v_ref[pl.ds(start, tk), :]
        for g in range(GQA_GROUP):
            s = _dot(qs[g], kc)
            m_old = m_ref[g]
            m_new = jnp.maximum(m_old, jnp.max(s, axis=-1, keepdims=True))
            p = jnp.exp(s - m_new[:, 0:1])
            alpha = jnp.exp(m_old - m_new)
            acc_ref[g] = alpha * acc_ref[g] + _dot(p.astype(BF16), vc)
            m_ref[g] = m_new
        return carry

    lax.fori_loop(0, nk, body, 0)
    outs = []
    for g in range(GQA_GROUP):
        a = acc_ref[g]
        outs.append(a[:, 0:GQA_DH] / a[:, GQA_DH:GQA_DH + 1])
    o_ref[...] = (jnp.concatenate(outs, axis=-1) * z_ref[...]).astype(o_ref.dtype)


def _gqa_attention(cq, kt, va, cz, batch, seq, tq, tk):
    m = cq.shape[0]
    nq = seq // tq
    width = GQA_GROUP * GQA_DH
    qspec = pl.BlockSpec((tq, width), lambda b, n, i: (b * nq + i, n))
    return pl.pallas_call(
        functools.partial(_gqa_kernel, tk=tk), out_shape=jax.ShapeDtypeStruct((m, GQA_W), BF16),
        grid=(batch, GQA_KV_HEADS, nq),
        in_specs=[qspec,
                  pl.BlockSpec((None, GQA_DH, seq), lambda b, n, i: (b * GQA_KV_HEADS + n, 0, 0)),
                  pl.BlockSpec((None, seq, LANES), lambda b, n, i: (b * GQA_KV_HEADS + n, 0, 0)),
                  qspec],
        out_specs=qspec,
        scratch_shapes=[pltpu.VMEM((GQA_GROUP, tq, LANES), F32), pltpu.VMEM((GQA_GROUP, tq, LANES), F32)],
        compiler_params=_params(("parallel", "parallel", "arbitrary")),
        name="gqa_attention")(cq, kt, va, cz)


def _merge_kernel(x_ref, of_ref, ob_ref, rz_ref, gn_ref, onb_ref, ogq_ref,
                  wg_ref, wa_ref, wb_ref, wc_ref, wo_ref, lg_ref, lb_ref, o_ref, *, alpha):
    x = x_ref[...]
    xb = x.astype(BF16)
    ra = of_ref[...] + ob_ref[...]
    parts = []
    for h in range(RET_HEADS):
        t = ra[:, h * RET_DK:(h + 1) * RET_DK]
        mu = jnp.mean(t, axis=-1, keepdims=True)
        d = t - mu
        var = jnp.mean(d * d, axis=-1, keepdims=True)
        parts.append(d * lax.rsqrt(var + LN_EPS))
    o_a = (jnp.concatenate(parts, axis=-1) * gn_ref[...] * rz_ref[...]).astype(BF16)
    y = _sigmoid(_dot(xb, wg_ref[:, 0:D_MODEL])) * _dot(o_a, wa_ref[...])
    y = y + _sigmoid(_dot(xb, wg_ref[:, D_MODEL:2 * D_MODEL])) * _dot(onb_ref[...], wb_ref[...])
    y = y + _sigmoid(_dot(xb, wg_ref[:, 2 * D_MODEL:3 * D_MODEL])) * _dot(ogq_ref[...], wc_ref[...])
    z = alpha * x + _dot(y.astype(BF16), wo_ref[...])
    mu = jnp.mean(z, axis=-1, keepdims=True)
    d = z - mu
    var = jnp.mean(d * d, axis=-1, keepdims=True)
    o_ref[...] = d * lax.rsqrt(var + LN_EPS) * lg_ref[...] + lb_ref[...]


def _merge(x2, o_f, o_b, rz, gn, o_nb, o_gq, wg, wa, wb, wc, wo, lg, lb, alpha, tm):
    m = x2.shape[0]
    return pl.pallas_call(
        functools.partial(_merge_kernel, alpha=alpha), out_shape=jax.ShapeDtypeStruct((m, D_MODEL), F32),
        grid=(m // tm,),
        in_specs=[_row_spec(tm, D_MODEL), _row_spec(tm, RET_W), _row_spec(tm, RET_W), _row_spec(tm, RET_W),
                  _const_spec(gn.shape), _row_spec(tm, NA_W), _row_spec(tm, GQA_W),
                  _const_spec(wg.shape), _const_spec(wa.shape), _const_spec(wb.shape), _const_spec(wc.shape),
                  _const_spec(wo.shape), _const_spec(lg.shape), _const_spec(lb.shape)],
        out_specs=_row_spec(tm, D_MODEL),
        compiler_params=_params(("parallel",)), name="merge")(
            x2, o_f, o_b, rz, gn, o_nb, o_gq, wg, wa, wb, wc, wo, lg, lb)


PROJ_TM = 512
MERGE_TM = 256
RET_CHUNKS_PER_STEP = 4
GQA_TQ = 256
GQA_TK = 512


def _rope_tables(seq):
    t = jnp.arange(seq)
    tf = t.astype(F32)
    row = (t // GRID_W).astype(F32)
    col = (t % GRID_W).astype(F32)
    inv_ret = 1.0 / (ROPE_THETA ** jnp.linspace(0.0, 1.0, RET_DK // 2, dtype=F32))
    ang_ret = tf[:, None] * inv_ret[None, :]
    n_ax = GQA_DH // 4
    inv_ax = ROPE_THETA ** (-jnp.arange(n_ax, dtype=F32) / n_ax)
    ang_ax = jnp.concatenate([row[:, None] * inv_ax[None, :], col[:, None] * inv_ax[None, :]], -1)

    def full(ang, reps):
        c, s = jnp.cos(ang), jnp.sin(ang)
        return jnp.tile(jnp.concatenate([c, c], -1), (1, reps)), jnp.tile(jnp.concatenate([-s, s], -1), (1, reps))

    return full(ang_ret, 1), full(ang_ax, LANES // GQA_DH)


def kernel(x, w_in, ret_theta_fwd, ret_theta_bwd, ret_gn_gain, na_rpb, gqa_q_norm, gqa_k_norm,
           w_branch_a, w_branch_b, w_branch_c, w_out, ln_gain, ln_bias):
    batch, seq, d_model = x.shape
    depth = w_in.shape[0]
    assert d_model == D_MODEL and w_in.shape[-1] == D_IN and seq % (GRID_W * NA_QROWS) == 0
    alpha = (2.0 * depth) ** 0.25
    m = batch * seq
    (ret_cos, ret_sin), (ax_cos, ax_sin) = _rope_tables(seq)
    gidx = np.arange(GQA_W) // GQA_DH
    gmat = jnp.asarray(gidx[:, None] == gidx[None, :], BF16)
    ones_col = jnp.concatenate([jnp.ones((m, 1), BF16), jnp.zeros((m, LANES - GQA_DH - 1), BF16)], -1)

    x2 = x.reshape(m, D_MODEL)
    for l in range(depth):
        w = w_in[l].astype(BF16)
        rq, rk, rv, rz = _proj_ret(x2, w[:, OFF_RET:OFF_NA], ret_cos, ret_sin, seq, PROJ_TM)
        nq, nk, nv, nz = _proj_na(x2, w[:, OFF_NA:OFF_GQA], PROJ_TM)
        qn = jnp.tile(gqa_q_norm[l], GQA_HEADS)[None, :]
        kn = jnp.tile(gqa_k_norm[l], GQA_KV_HEADS)[None, :]
        cq, ck, cv, cz = _proj_gqa(x2, w[:, OFF_GQA:OFF_GATES], ax_cos, ax_sin, gmat, qn, kn, seq, PROJ_TM)

        decay = jnp.stack([jax.nn.log_sigmoid(ret_theta_fwd[l].astype(F32)),
                           jax.nn.log_sigmoid(ret_theta_bwd[l].astype(F32))], axis=1)
        decay = jnp.broadcast_to(jnp.pad(decay, ((0, 0), (0, 6)))[:, :, None], (RET_HEADS, 8, LANES))
        o_f, o_b = _retention(rq, rk, rv, decay, batch, seq, RET_CHUNKS_PER_STEP)

        bias = _na_bias_tables(na_rpb[l], seq // GRID_W)
        o_nb = _na_attention(nq, nk, nv, nz, bias, batch, seq)

        kt = ck.reshape(batch, seq, GQA_KV_HEADS, GQA_DH).transpose(0, 2, 3, 1).reshape(
            batch * GQA_KV_HEADS, GQA_DH, seq)
        va = jnp.concatenate([cv.reshape(m, GQA_KV_HEADS, GQA_DH),
                              jnp.broadcast_to(ones_col[:, None, :], (m, GQA_KV_HEADS, LANES - GQA_DH))], -1)
        va = va.reshape(batch, seq, GQA_KV_HEADS, LANES).transpose(0, 2, 1, 3).reshape(
            batch * GQA_KV_HEADS, seq, LANES)
        o_gq = _gqa_attention(cq, kt, va, cz, batch, seq, GQA_TQ, GQA_TK)

        x2 = _merge(x2, o_f, o_b, rz, ret_gn_gain[l][None, :], o_nb, o_gq,
                    w[:, OFF_GATES:D_IN], w_branch_a[l].astype(BF16), w_branch_b[l].astype(BF16),
                    w_branch_c[l].astype(BF16), w_out[l].astype(BF16),
                    ln_gain[l][None, :], ln_bias[l][None, :], alpha, MERGE_TM)
    return x2.reshape(batch, seq, D_MODEL)
```

```python
import functools
import math

import numpy as np
import jax
import jax.numpy as jnp
from jax import lax
from jax.experimental import pallas as pl
from jax.experimental.pallas import tpu as pltpu

D_MODEL = 1024
GRID_W = 64
RET_HEADS = 4
RET_DK = 128
RET_CHUNK = 128
NA_HEADS = 8
NA_DH = 64
NA_KH = 8
NA_KW = 16
GQA_HEADS = 8
GQA_KV_HEADS = 2
GQA_DH = 64
ROPE_THETA = 10000.0
RMS_EPS = 1e-6
LN_EPS = 1e-5

RET_W = RET_HEADS * RET_DK
NA_W = NA_HEADS * NA_DH
GQA_W = GQA_HEADS * GQA_DH
GQA_KV_W = GQA_KV_HEADS * GQA_DH

OFF_RET = 0
OFF_NA = 4 * RET_W
OFF_GQA = OFF_NA + 4 * NA_W
OFF_GATES = OFF_GQA + 2 * GQA_W + 2 * GQA_KV_W
D_IN = OFF_GATES + 3 * D_MODEL

LANES = 128
VMEM_LIMIT = 56 * 1024 * 1024

BF16 = jnp.bfloat16
F32 = jnp.float32


def _dot(a, b):
    return jnp.dot(a, b, preferred_element_type=F32)


def _dot_nt(a, b):
    return lax.dot_general(a, b, (((1,), (1,)), ((), ())), preferred_element_type=F32)


def _dot_tn(a, b):
    return lax.dot_general(a, b, (((0,), (0,)), ((), ())), preferred_element_type=F32)


def _silu(z):
    return z / (1.0 + jnp.exp(-z))


def _sigmoid(z):
    return 1.0 / (1.0 + jnp.exp(-z))


def _params(sem):
    return pltpu.CompilerParams(dimension_semantics=sem, vmem_limit_bytes=VMEM_LIMIT)


def _proj_ret_kernel(x_ref, w_ref, cos_ref, sin_ref, q_ref, k_ref, v_ref, z_ref):
    xb = x_ref[...].astype(BF16)
    cos = cos_ref[...]
    sin = sin_ref[...]
    scale = RET_DK ** -0.5
    yq = _dot(xb, w_ref[:, 0:RET_W])
    yk = _dot(xb, w_ref[:, RET_W:2 * RET_W])
    for h in range(RET_HEADS):
        sl = slice(h * RET_DK, (h + 1) * RET_DK)
        t = yq[:, sl]
        q_ref[:, sl] = ((t * cos + pltpu.roll(t, RET_DK // 2, 1) * sin) * scale).astype(BF16)
        t = yk[:, sl]
        k_ref[:, sl] = (t * cos + pltpu.roll(t, RET_DK // 2, 1) * sin).astype(BF16)
    v_ref[...] = _dot(xb, w_ref[:, 2 * RET_W:3 * RET_W]).astype(BF16)
    z_ref[...] = _silu(_dot(xb, w_ref[:, 3 * RET_W:4 * RET_W]))


def _proj_na_kernel(x_ref, w_ref, q_ref, k_ref, v_ref, z_ref):
    xb = x_ref[...].astype(BF16)
    q_ref[...] = (_dot(xb, w_ref[:, 0:NA_W]) * (NA_DH ** -0.5)).astype(BF16)
    k_ref[...] = _dot(xb, w_ref[:, NA_W:2 * NA_W]).astype(BF16)
    v_ref[...] = _dot(xb, w_ref[:, 2 * NA_W:3 * NA_W]).astype(BF16)
    z_ref[...] = _silu(_dot(xb, w_ref[:, 3 * NA_W:4 * NA_W]))


def _group_mean_sq(y, g_ref, width):
    sq = y * y
    hi = sq.astype(BF16)
    lo = (sq - hi.astype(F32)).astype(BF16)
    g = g_ref[0:width, 0:width]
    return (_dot(hi, g) + _dot(lo, g)) * (1.0 / GQA_DH)


def _rope64(t, cos, sin, first_half):
    swapped = jnp.where(first_half, pltpu.roll(t, LANES - GQA_DH // 2, 1), pltpu.roll(t, GQA_DH // 2, 1))
    return t * cos + swapped * sin


def _proj_gqa_kernel(x_ref, w_ref, cos_ref, sin_ref, g_ref, qn_ref, kn_ref,
                     q_ref, k_ref, v_ref, z_ref):
    xb = x_ref[...].astype(BF16)
    cos = cos_ref[...]
    sin = sin_ref[...]
    lane = lax.broadcasted_iota(jnp.int32, cos.shape, 1)
    first_half = (lane % GQA_DH) < (GQA_DH // 2)
    yq = _dot(xb, w_ref[:, 0:GQA_W])
    qn = yq * lax.rsqrt(_group_mean_sq(yq, g_ref, GQA_W) + RMS_EPS) * qn_ref[...]
    for j in range(GQA_W // LANES):
        sl = slice(j * LANES, (j + 1) * LANES)
        q_ref[:, sl] = (_rope64(qn[:, sl], cos, sin, first_half) * (LOG2E * GQA_DH ** -0.5)).astype(BF16)
    yk = _dot(xb, w_ref[:, GQA_W:GQA_W + GQA_KV_W])
    kn = yk * lax.rsqrt(_group_mean_sq(yk, g_ref, GQA_KV_W) + RMS_EPS) * kn_ref[...]
    k_ref[...] = _rope64(kn, cos, sin, first_half).astype(BF16)
    v_ref[...] = _dot(xb, w_ref[:, GQA_W + GQA_KV_W:GQA_W + 2 * GQA_KV_W]).astype(BF16)
    z_ref[...] = _silu(_dot(xb, w_ref[:, GQA_W + 2 * GQA_KV_W:2 * GQA_W + 2 * GQA_KV_W]))


def _row_spec(tm, width):
    return pl.BlockSpec((tm, width), lambda i: (i, 0))


def _const_spec(shape):
    return pl.BlockSpec(shape, lambda i: (0,) * len(shape))


def _table_spec(tm, seq):
    nblk = seq // tm
    return pl.BlockSpec((tm, LANES), lambda i: (i % nblk, 0))


def _proj_ret(x2, w, cos, sin, seq, tm):
    m = x2.shape[0]
    out = [jax.ShapeDtypeStruct((m, RET_W), BF16)] * 3 + [jax.ShapeDtypeStruct((m, RET_W), F32)]
    return pl.pallas_call(
        _proj_ret_kernel, out_shape=out, grid=(m // tm,),
        in_specs=[_row_spec(tm, D_MODEL), _const_spec(w.shape), _table_spec(tm, seq), _table_spec(tm, seq)],
        out_specs=[_row_spec(tm, RET_W)] * 4,
        compiler_params=_params(("parallel",)), name="proj_ret")(x2, w, cos, sin)


def _proj_na(x2, w, tm):
    m = x2.shape[0]
    out = [jax.ShapeDtypeStruct((m, NA_W), BF16)] * 3 + [jax.ShapeDtypeStruct((m, NA_W), F32)]
    return pl.pallas_call(
        _proj_na_kernel, out_shape=out, grid=(m // tm,),
        in_specs=[_row_spec(tm, D_MODEL), _const_spec(w.shape)],
        out_specs=[_row_spec(tm, NA_W)] * 4,
        compiler_params=_params(("parallel",)), name="proj_na")(x2, w)


def _proj_gqa(x2, w, cos, sin, gmat, qn, kn, seq, tm):
    m = x2.shape[0]
    out = [jax.ShapeDtypeStruct((m, GQA_W), BF16), jax.ShapeDtypeStruct((m, GQA_KV_W), BF16),
           jax.ShapeDtypeStruct((m, GQA_KV_W), BF16), jax.ShapeDtypeStruct((m, GQA_W), F32)]
    return pl.pallas_call(
        _proj_gqa_kernel, out_shape=out, grid=(m // tm,),
        in_specs=[_row_spec(tm, D_MODEL), _const_spec(w.shape), _table_spec(tm, seq), _table_spec(tm, seq),
                  _const_spec(gmat.shape), _const_spec(qn.shape), _const_spec(kn.shape)],
        out_specs=[_row_spec(tm, GQA_W), _row_spec(tm, GQA_KV_W), _row_spec(tm, GQA_KV_W), _row_spec(tm, GQA_W)],
        compiler_params=_params(("parallel",)), name="proj_gqa")(x2, w, cos, sin, gmat, qn, kn)


def _retention_kernel(dec_ref, qf_ref, kf_ref, vf_ref, qb_ref, kb_ref, vb_ref,
                      of_ref, ob_ref, sf_ref, sb_ref, *, chunks):
    c = RET_CHUNK

    @pl.when(pl.program_id(2) == 0)
    def _():
        sf_ref[...] = jnp.zeros_like(sf_ref)
        sb_ref[...] = jnp.zeros_like(sb_ref)

    lf = dec_ref[0:1, :]
    lb = dec_ref[1:2, :]
    row = lax.broadcasted_iota(jnp.int32, (c, c), 0).astype(F32)
    col = lax.broadcasted_iota(jnp.int32, (c, c), 1).astype(F32)
    diff = row - col
    d_intra = jnp.where(diff >= 0, jnp.exp(jnp.maximum(diff, 0.0) * lf), jnp.exp(jnp.maximum(-diff, 0.0) * lb))
    w_kf = jnp.exp((c - 1.0 - row) * lf)
    w_qf = jnp.exp((row + 1.0) * lf)
    w_kb = jnp.exp(row * lb)
    w_qb = jnp.exp((c - row) * lb)
    chunk_f = jnp.exp(c * lf)
    chunk_b = jnp.exp(c * lb)

    s_f = sf_ref[...]
    for i in range(chunks):
        rows = slice(i * c, (i + 1) * c)
        q = qf_ref[rows, :]
        k = kf_ref[rows, :]
        v = vf_ref[rows, :]
        s = (_dot_nt(q, k) * d_intra).astype(BF16)
        o = _dot(s, v)
        qw = (q.astype(F32) * w_qf).astype(BF16)
        o = o + _dot(qw, s_f.astype(BF16))
        of_ref[rows, :] = o
        kw = (k.astype(F32) * w_kf).astype(BF16)
        s_f = chunk_f * s_f + _dot_tn(kw, v)
    sf_ref[...] = s_f

    s_b = sb_ref[...]
    for i in reversed(range(chunks)):
        rows = slice(i * c, (i + 1) * c)
        q = qb_ref[rows, :]
        k = kb_ref[rows, :]
        v = vb_ref[rows, :]
        qw = (q.astype(F32) * w_qb).astype(BF16)
        ob_ref[rows, :] = _dot(qw, s_b.astype(BF16))
        kw = (k.astype(F32) * w_kb).astype(BF16)
        s_b = chunk_b * s_b + _dot_tn(kw, v)
    sb_ref[...] = s_b


def _retention(rq, rk, rv, decay, batch, seq, chunks):
    m = rq.shape[0]
    rows = chunks * RET_CHUNK
    ng = seq // rows
    fwd = pl.BlockSpec((rows, RET_DK), lambda b, h, g: (b * ng + g, h))
    bwd = pl.BlockSpec((rows, RET_DK), lambda b, h, g: (b * ng + ng - 1 - g, h))
    dec = pl.BlockSpec((None, 8, LANES), lambda b, h, g: (h, 0, 0))
    out = [jax.ShapeDtypeStruct((m, RET_W), F32)] * 2
    return pl.pallas_call(
        functools.partial(_retention_kernel, chunks=chunks), out_shape=out,
        grid=(batch, RET_HEADS, ng),
        in_specs=[dec, fwd, fwd, fwd, bwd, bwd, bwd],
        out_specs=[fwd, bwd],
        scratch_shapes=[pltpu.VMEM((RET_DK, RET_DK), F32), pltpu.VMEM((RET_DK, RET_DK), F32)],
        compiler_params=_params(("parallel", "parallel", "arbitrary")),
        name="retention")(decay, rq, rk, rv, rq, rk, rv)


NA_QROWS = 4
NA_KROWS = 3 * NA_QROWS
NA_TQ = NA_QROWS * GRID_W
NA_TK = NA_KROWS * GRID_W


def _na_kernel(q_ref, k0_ref, k1_ref, k2_ref, v0_ref, v1_ref, v2_ref, bias_ref, z_ref, o_ref):
    q = q_ref[...]
    k = jnp.concatenate([k0_ref[...], k1_ref[...], k2_ref[...]], axis=0)
    v = jnp.concatenate([v0_ref[...], v1_ref[...], v2_ref[...]], axis=0)
    lane = lax.broadcasted_iota(jnp.int32, k.shape, 1)
    zero = jnp.zeros_like(k)
    acc = None
    for half in range(2):
        own = (lane < NA_DH) if half == 0 else (lane >= NA_DH)
        s = _dot_nt(q, jnp.where(own, k, zero)) + bias_ref[half]
        mx = jnp.max(s, axis=-1, keepdims=True)
        p = jnp.exp(s - mx)
        den = jnp.sum(p, axis=-1, keepdims=True)
        o = _dot(p.astype(BF16), jnp.where(own, v, zero)) / den
        acc = o if acc is None else acc + o
    o_ref[...] = (acc * z_ref[...]).astype(o_ref.dtype)


def _na_bias_tables(rpb, rows):
    nblk = rows // NA_QROWS
    nrow, ncol = 2 * NA_KH - 1, 2 * NA_KW - 1
    col = np.arange(GRID_W)
    c_start = np.clip(col - NA_KW // 2, 0, GRID_W - NA_KW)
    col_ok = (col[None, :] >= c_start[:, None]) & (col[None, :] < c_start[:, None] + NA_KW)
    cidx = np.clip(col[None, :] - col[:, None] + NA_KW - 1, 0, ncol - 1)
    col_sel = (cidx[:, :, None] == np.arange(ncol)).astype(np.float32)
    row_sel = np.zeros((3, NA_QROWS, NA_KROWS, nrow), np.float32)
    row_ok = np.zeros((3, NA_QROWS, NA_KROWS), bool)
    for cls, blk in enumerate((0, 1, nblk - 1)):
        first_kblk = min(max(blk - 1, 0), nblk - 3)
        qr = blk * NA_QROWS + np.arange(NA_QROWS)
        kr = first_kblk * NA_QROWS + np.arange(NA_KROWS)
        r_start = np.clip(qr - NA_KH // 2, 0, rows - NA_KH)
        ok = (kr[None, :] >= r_start[:, None]) & (kr[None, :] < r_start[:, None] + NA_KH)
        ridx = kr[None, :] - qr[:, None] + NA_KH - 1
        row_ok[cls] = ok
        row_sel[cls] = ((ridx[:, :, None] == np.arange(nrow)) & ok[:, :, None])
    by_col = jnp.einsum('har,qkr->haqk', rpb.astype(F32), col_sel, precision=lax.Precision.HIGHEST)
    vals = jnp.einsum('cija,haqk->chiqjk', row_sel, by_col, precision=lax.Precision.HIGHEST)
    valid = row_ok[:, None, :, None, :, None] & col_ok[None, None, None, :, None, :]
    table = jnp.where(valid, vals, -jnp.inf)
    return table.reshape(3, NA_HEADS, NA_TQ, NA_TK)


def _na_attention(nq, nk, nv, nz, bias, batch, seq):
    m = nq.shape[0]
    rows = seq // GRID_W
    nblk = rows // NA_QROWS
    pairs = NA_HEADS // 2

    def kblk(j):
        return lambda b, hp, i: (b * nblk + jnp.clip(i - 1, 0, nblk - 3) + j, hp)

    def bias_idx(b, hp, i):
        cls = jnp.where(i == 0, 0, jnp.where(i == nblk - 1, 2, 1))
        return (cls, hp, 0, 0)

    qspec = pl.BlockSpec((NA_TQ, LANES), lambda b, hp, i: (b * nblk + i, hp))
    kspecs = [pl.BlockSpec((NA_TQ, LANES), kblk(j)) for j in range(3)]
    return pl.pallas_call(
        _na_kernel, out_shape=jax.ShapeDtypeStruct((m, NA_W), BF16),
        grid=(batch, pairs, nblk),
        in_specs=[qspec] + kspecs + kspecs + [pl.BlockSpec((None, 2, NA_TQ, NA_TK), bias_idx), qspec],
        out_specs=qspec,
        compiler_params=_params(("parallel", "parallel", "arbitrary")),
        name="na_attention")(nq, nk, nk, nk, nv, nv, nv, bias, nz)


GQA_GROUP = GQA_HEADS // GQA_KV_HEADS
LOG2E = math.log2(math.e)


def _gqa_kernel(q_ref, kt_ref, v_ref, z_ref, o_ref,
                qs_ref, s_ref, p_ref, a_ref, m_ref, acc_ref, *, tk, rb):
    tq = q_ref.shape[0]
    rows = GQA_GROUP * tq
    nk = kt_ref.shape[1] // tk
    assert nk % 2 == 0 and nk >= 4
    for g in range(GQA_GROUP):
        qs_ref[g * tq:(g + 1) * tq, :] = q_ref[:, g * GQA_DH:(g + 1) * GQA_DH]
    m_ref[...] = jnp.full(m_ref.shape, -jnp.inf, F32)
    acc_ref[...] = jnp.zeros_like(acc_ref)

    def scores(c, par):
        start = pl.multiple_of(c * tk, tk)
        s_ref[par] = _dot(qs_ref[...], kt_ref[:, pl.ds(start, tk)])

    def softmax(par):
        for r in range(0, rows, rb):
            rs = slice(r, r + rb)
            groups = [s_ref[par, rs, j * LANES:(j + 1) * LANES] for j in range(tk // LANES)]
            mg = functools.reduce(jnp.maximum, groups)
            m_old = m_ref[rs, :]
            m_new = jnp.maximum(m_old, jnp.max(mg, axis=-1, keepdims=True))
            for j, sj in enumerate(groups):
                p_ref[par, rs, j * LANES:(j + 1) * LANES] = jnp.exp2(sj - m_new).astype(BF16)
            a_ref[par, rs, :] = jnp.exp2(m_old - m_new)
            m_ref[rs, :] = m_new

    def values(c, par):
        start = pl.multiple_of(c * tk, tk)
        acc_ref[...] = a_ref[par] * acc_ref[...] + _dot(p_ref[par], v_ref[pl.ds(start, tk), :])

    scores(0, 0)
    scores(1, 1)
    softmax(0)

    def body(j, carry):
        i = 2 * j
        scores(i, 0)
        softmax(1)
        values(i - 2, 0)
        scores(i + 1, 1)
        softmax(0)
        values(i - 1, 1)
        return carry

    lax.fori_loop(1, nk // 2, body, 0)
    softmax(1)
    values(nk - 2, 0)
    values(nk - 1, 1)

    outs = []
    for g in range(GQA_GROUP):
        a = acc_ref[g * tq:(g + 1) * tq, :]
        outs.append(a[:, 0:GQA_DH] / a[:, GQA_DH:GQA_DH + 1])
    o_ref[...] = (jnp.concatenate(outs, axis=-1) * z_ref[...]).astype(o_ref.dtype)


def _gqa_attention(cq, kt, va, cz, batch, seq, tq, tk, rb):
    m = cq.shape[0]
    nq = seq // tq
    width = GQA_GROUP * GQA_DH
    rows = GQA_GROUP * tq
    qspec = pl.BlockSpec((tq, width), lambda b, n, i: (b * nq + i, n))
    return pl.pallas_call(
        functools.partial(_gqa_kernel, tk=tk, rb=rb), out_shape=jax.ShapeDtypeStruct((m, GQA_W), BF16),
        grid=(batch, GQA_KV_HEADS, nq),
        in_specs=[qspec,
                  pl.BlockSpec((None, GQA_DH, seq), lambda b, n, i: (b * GQA_KV_HEADS + n, 0, 0)),
                  pl.BlockSpec((None, seq, LANES), lambda b, n, i: (b * GQA_KV_HEADS + n, 0, 0)),
                  qspec],
        out_specs=qspec,
        scratch_shapes=[pltpu.VMEM((rows, GQA_DH), BF16),
                        pltpu.VMEM((2, rows, tk), F32),
                        pltpu.VMEM((2, rows, tk), BF16),
                        pltpu.VMEM((2, rows, LANES), F32),
                        pltpu.VMEM((rows, LANES), F32),
                        pltpu.VMEM((rows, LANES), F32)],
        compiler_params=_params(("parallel", "parallel", "arbitrary")),
        name="gqa_attention")(cq, kt, va, cz)


def _merge_kernel(x_ref, of_ref, ob_ref, rz_ref, gn_ref, onb_ref, ogq_ref,
                  wg_ref, wa_ref, wb_ref, wc_ref, wo_ref, lg_ref, lb_ref, o_ref, *, alpha):
    x = x_ref[...]
    xb = x.astype(BF16)
    ra = of_ref[...] + ob_ref[...]
    parts = []
    for h in range(RET_HEADS):
        t = ra[:, h * RET_DK:(h + 1) * RET_DK]
        mu = jnp.mean(t, axis=-1, keepdims=True)
        d = t - mu
        var = jnp.mean(d * d, axis=-1, keepdims=True)
        parts.append(d * lax.rsqrt(var + LN_EPS))
    o_a = (jnp.concatenate(parts, axis=-1) * gn_ref[...] * rz_ref[...]).astype(BF16)
    y = _sigmoid(_dot(xb, wg_ref[:, 0:D_MODEL])) * _dot(o_a, wa_ref[...])
    y = y + _sigmoid(_dot(xb, wg_ref[:, D_MODEL:2 * D_MODEL])) * _dot(onb_ref[...], wb_ref[...])
    y = y + _sigmoid(_dot(xb, wg_ref[:, 2 * D_MODEL:3 * D_MODEL])) * _dot(ogq_ref[...], wc_ref[...])
    z = alpha * x + _dot(y.astype(BF16), wo_ref[...])
    mu = jnp.mean(z, axis=-1, keepdims=True)
    d = z - mu
    var = jnp.mean(d * d, axis=-1, keepdims=True)
    o_ref[...] = d * lax.rsqrt(var + LN_EPS) * lg_ref[...] + lb_ref[...]


def _merge(x2, o_f, o_b, rz, gn, o_nb, o_gq, wg, wa, wb, wc, wo, lg, lb, alpha, tm):
    m = x2.shape[0]
    return pl.pallas_call(
        functools.partial(_merge_kernel, alpha=alpha), out_shape=jax.ShapeDtypeStruct((m, D_MODEL), F32),
        grid=(m // tm,),
        in_specs=[_row_spec(tm, D_MODEL), _row_spec(tm, RET_W), _row_spec(tm, RET_W), _row_spec(tm, RET_W),
                  _const_spec(gn.shape), _row_spec(tm, NA_W), _row_spec(tm, GQA_W),
                  _const_spec(wg.shape), _const_spec(wa.shape), _const_spec(wb.shape), _const_spec(wc.shape),
                  _const_spec(wo.shape), _const_spec(lg.shape), _const_spec(lb.shape)],
        out_specs=_row_spec(tm, D_MODEL),
        compiler_params=_params(("parallel",)), name="merge")(
            x2, o_f, o_b, rz, gn, o_nb, o_gq, wg, wa, wb, wc, wo, lg, lb)


PROJ_TM = 512
MERGE_TM = 256
RET_CHUNKS_PER_STEP = 4
GQA_TQ = 128
GQA_TK = 512
GQA_RB = 64


def _rope_tables(seq):
    t = jnp.arange(seq)
    tf = t.astype(F32)
    row = (t // GRID_W).astype(F32)
    col = (t % GRID_W).astype(F32)
    inv_ret = 1.0 / (ROPE_THETA ** jnp.linspace(0.0, 1.0, RET_DK // 2, dtype=F32))
    ang_ret = tf[:, None] * inv_ret[None, :]
    n_ax = GQA_DH // 4
    inv_ax = ROPE_THETA ** (-jnp.arange(n_ax, dtype=F32) / n_ax)
    ang_ax = jnp.concatenate([row[:, None] * inv_ax[None, :], col[:, None] * inv_ax[None, :]], -1)

    def full(ang, reps):
        c, s = jnp.cos(ang), jnp.sin(ang)
        return jnp.tile(jnp.concatenate([c, c], -1), (1, reps)), jnp.tile(jnp.concatenate([-s, s], -1), (1, reps))

    return full(ang_ret, 1), full(ang_ax, LANES // GQA_DH)


def kernel(x, w_in, ret_theta_fwd, ret_theta_bwd, ret_gn_gain, na_rpb, gqa_q_norm, gqa_k_norm,
           w_branch_a, w_branch_b, w_branch_c, w_out, ln_gain, ln_bias):
    batch, seq, d_model = x.shape
    depth = w_in.shape[0]
    assert d_model == D_MODEL and w_in.shape[-1] == D_IN and seq % (GRID_W * NA_QROWS) == 0
    alpha = (2.0 * depth) ** 0.25
    m = batch * seq
    (ret_cos, ret_sin), (ax_cos, ax_sin) = _rope_tables(seq)
    gidx = np.arange(GQA_W) // GQA_DH
    gmat = jnp.asarray(gidx[:, None] == gidx[None, :], BF16)
    ones_col = jnp.concatenate([jnp.ones((m, 1), BF16), jnp.zeros((m, LANES - GQA_DH - 1), BF16)], -1)

    x2 = x.reshape(m, D_MODEL)
    for l in range(depth):
        w = w_in[l].astype(BF16)
        rq, rk, rv, rz = _proj_ret(x2, w[:, OFF_RET:OFF_NA], ret_cos, ret_sin, seq, PROJ_TM)
        nq, nk, nv, nz = _proj_na(x2, w[:, OFF_NA:OFF_GQA], PROJ_TM)
        qn = jnp.tile(gqa_q_norm[l], GQA_HEADS)[None, :]
        kn = jnp.tile(gqa_k_norm[l], GQA_KV_HEADS)[None, :]
        cq, ck, cv, cz = _proj_gqa(x2, w[:, OFF_GQA:OFF_GATES], ax_cos, ax_sin, gmat, qn, kn, seq, PROJ_TM)

        decay = jnp.stack([jax.nn.log_sigmoid(ret_theta_fwd[l].astype(F32)),
                           jax.nn.log_sigmoid(ret_theta_bwd[l].astype(F32))], axis=1)
        decay = jnp.broadcast_to(jnp.pad(decay, ((0, 0), (0, 6)))[:, :, None], (RET_HEADS, 8, LANES))
        o_f, o_b = _retention(rq, rk, rv, decay, batch, seq, RET_CHUNKS_PER_STEP)

        bias = _na_bias_tables(na_rpb[l], seq // GRID_W)
        o_nb = _na_attention(nq, nk, nv, nz, bias, batch, seq)

        kt = ck.reshape(batch, seq, GQA_KV_HEADS, GQA_DH).transpose(0, 2, 3, 1).reshape(
            batch * GQA_KV_HEADS, GQA_DH, seq)
        va = jnp.concatenate([cv.reshape(m, GQA_KV_HEADS, GQA_DH),
                              jnp.broadcast_to(ones_col[:, None, :], (m, GQA_KV_HEADS, LANES - GQA_DH))], -1)
        va = va.reshape(batch, seq, GQA_KV_HEADS, LANES).transpose(0, 2, 1, 3).reshape(
            batch * GQA_KV_HEADS, seq, LANES)
        o_gq = _gqa_attention(cq, kt, va, cz, batch, seq, GQA_TQ, GQA_TK, GQA_RB)

        x2 = _merge(x2, o_f, o_b, rz, ret_gn_gain[l][None, :], o_nb, o_gq,
                    w[:, OFF_GATES:D_IN], w_branch_a[l].astype(BF16), w_branch_b[l].astype(BF16),
                    w_branch_c[l].astype(BF16), w_out[l].astype(BF16),
                    ln_gain[l][None, :], ln_bias[l][None, :], alpha, MERGE_TM)
    return x2.reshape(batch, seq, D_MODEL)
```

```python
import functools
import math

import numpy as np
import jax
import jax.numpy as jnp
from jax import lax
from jax.experimental import pallas as pl
from jax.experimental.pallas import tpu as pltpu

D_MODEL = 1024
GRID_W = 64
RET_HEADS = 4
RET_DK = 128
RET_CHUNK = 128
NA_HEADS = 8
NA_DH = 64
NA_KH = 8
NA_KW = 16
GQA_HEADS = 8
GQA_KV_HEADS = 2
GQA_DH = 64
ROPE_THETA = 10000.0
RMS_EPS = 1e-6
LN_EPS = 1e-5

RET_W = RET_HEADS * RET_DK
NA_W = NA_HEADS * NA_DH
GQA_W = GQA_HEADS * GQA_DH
GQA_KV_W = GQA_KV_HEADS * GQA_DH

OFF_RET = 0
OFF_NA = 4 * RET_W
OFF_GQA = OFF_NA + 4 * NA_W
OFF_GATES = OFF_GQA + 2 * GQA_W + 2 * GQA_KV_W
D_IN = OFF_GATES + 3 * D_MODEL

LANES = 128
VMEM_LIMIT = 56 * 1024 * 1024

BF16 = jnp.bfloat16
F32 = jnp.float32


def _dot(a, b):
    return jnp.dot(a, b, preferred_element_type=F32)


def _dot_nt(a, b):
    return lax.dot_general(a, b, (((1,), (1,)), ((), ())), preferred_element_type=F32)


def _dot_tn(a, b):
    return lax.dot_general(a, b, (((0,), (0,)), ((), ())), preferred_element_type=F32)


def _silu(z):
    return z / (1.0 + jnp.exp(-z))


def _sigmoid(z):
    return 1.0 / (1.0 + jnp.exp(-z))


def _params(sem):
    return pltpu.CompilerParams(dimension_semantics=sem, vmem_limit_bytes=VMEM_LIMIT)


def _proj_ret_kernel(x_ref, w_ref, cos_ref, sin_ref, q_ref, k_ref, v_ref, z_ref):
    xb = x_ref[...].astype(BF16)
    cos = cos_ref[...]
    sin = sin_ref[...]
    scale = RET_DK ** -0.5
    yq = _dot(xb, w_ref[:, 0:RET_W])
    yk = _dot(xb, w_ref[:, RET_W:2 * RET_W])
    for h in range(RET_HEADS):
        sl = slice(h * RET_DK, (h + 1) * RET_DK)
        t = yq[:, sl]
        q_ref[:, sl] = ((t * cos + pltpu.roll(t, RET_DK // 2, 1) * sin) * scale).astype(BF16)
        t = yk[:, sl]
        k_ref[:, sl] = (t * cos + pltpu.roll(t, RET_DK // 2, 1) * sin).astype(BF16)
    v_ref[...] = _dot(xb, w_ref[:, 2 * RET_W:3 * RET_W]).astype(BF16)
    z_ref[...] = _silu(_dot(xb, w_ref[:, 3 * RET_W:4 * RET_W]))


def _proj_na_kernel(x_ref, w_ref, q_ref, k_ref, v_ref, z_ref):
    xb = x_ref[...].astype(BF16)
    q_ref[...] = (_dot(xb, w_ref[:, 0:NA_W]) * (NA_DH ** -0.5)).astype(BF16)
    k_ref[...] = _dot(xb, w_ref[:, NA_W:2 * NA_W]).astype(BF16)
    v_ref[...] = _dot(xb, w_ref[:, 2 * NA_W:3 * NA_W]).astype(BF16)
    z_ref[...] = _silu(_dot(xb, w_ref[:, 3 * NA_W:4 * NA_W]))


def _group_mean_sq(y, g_ref, width):
    sq = y * y
    hi = sq.astype(BF16)
    lo = (sq - hi.astype(F32)).astype(BF16)
    g = g_ref[0:width, 0:width]
    return (_dot(hi, g) + _dot(lo, g)) * (1.0 / GQA_DH)


def _rope64(t, cos, sin, first_half):
    swapped = jnp.where(first_half, pltpu.roll(t, LANES - GQA_DH // 2, 1), pltpu.roll(t, GQA_DH // 2, 1))
    return t * cos + swapped * sin


def _proj_gqa_kernel(x_ref, w_ref, cos_ref, sin_ref, g_ref, qn_ref, kn_ref,
                     q_ref, k_ref, v_ref, z_ref):
    xb = x_ref[...].astype(BF16)
    cos = cos_ref[...]
    sin = sin_ref[...]
    lane = lax.broadcasted_iota(jnp.int32, cos.shape, 1)
    first_half = (lane % GQA_DH) < (GQA_DH // 2)
    yq = _dot(xb, w_ref[:, 0:GQA_W])
    qn = yq * lax.rsqrt(_group_mean_sq(yq, g_ref, GQA_W) + RMS_EPS) * qn_ref[...]
    for j in range(GQA_W // LANES):
        sl = slice(j * LANES, (j + 1) * LANES)
        q_ref[:, sl] = (_rope64(qn[:, sl], cos, sin, first_half) * (LOG2E * GQA_DH ** -0.5)).astype(BF16)
    yk = _dot(xb, w_ref[:, GQA_W:GQA_W + GQA_KV_W])
    kn = yk * lax.rsqrt(_group_mean_sq(yk, g_ref, GQA_KV_W) + RMS_EPS) * kn_ref[...]
    k_ref[...] = _rope64(kn, cos, sin, first_half).astype(BF16)
    v_ref[...] = _dot(xb, w_ref[:, GQA_W + GQA_KV_W:GQA_W + 2 * GQA_KV_W]).astype(BF16)
    z_ref[...] = _silu(_dot(xb, w_ref[:, GQA_W + 2 * GQA_KV_W:2 * GQA_W + 2 * GQA_KV_W]))


def _row_spec(tm, width):
    return pl.BlockSpec((tm, width), lambda i: (i, 0))


def _const_spec(shape):
    return pl.BlockSpec(shape, lambda i: (0,) * len(shape))


def _table_spec(tm, seq):
    nblk = seq // tm
    return pl.BlockSpec((tm, LANES), lambda i: (i % nblk, 0))


def _proj_ret(x2, w, cos, sin, seq, tm):
    m = x2.shape[0]
    out = [jax.ShapeDtypeStruct((m, RET_W), BF16)] * 3 + [jax.ShapeDtypeStruct((m, RET_W), F32)]
    return pl.pallas_call(
        _proj_ret_kernel, out_shape=out, grid=(m // tm,),
        in_specs=[_row_spec(tm, D_MODEL), _const_spec(w.shape), _table_spec(tm, seq), _table_spec(tm, seq)],
        out_specs=[_row_spec(tm, RET_W)] * 4,
        compiler_params=_params(("parallel",)), name="proj_ret")(x2, w, cos, sin)


def _proj_na(x2, w, tm):
    m = x2.shape[0]
    out = [jax.ShapeDtypeStruct((m, NA_W), BF16)] * 3 + [jax.ShapeDtypeStruct((m, NA_W), F32)]
    return pl.pallas_call(
        _proj_na_kernel, out_shape=out, grid=(m // tm,),
        in_specs=[_row_spec(tm, D_MODEL), _const_spec(w.shape)],
        out_specs=[_row_spec(tm, NA_W)] * 4,
        compiler_params=_params(("parallel",)), name="proj_na")(x2, w)


def _proj_gqa(x2, w, cos, sin, gmat, qn, kn, seq, tm):
    m = x2.shape[0]
    out = [jax.ShapeDtypeStruct((m, GQA_W), BF16), jax.ShapeDtypeStruct((m, GQA_KV_W), BF16),
           jax.ShapeDtypeStruct((m, GQA_KV_W), BF16), jax.ShapeDtypeStruct((m, GQA_W), F32)]
    return pl.pallas_call(
        _proj_gqa_kernel, out_shape=out, grid=(m // tm,),
        in_specs=[_row_spec(tm, D_MODEL), _const_spec(w.shape), _table_spec(tm, seq), _table_spec(tm, seq),
                  _const_spec(gmat.shape), _const_spec(qn.shape), _const_spec(kn.shape)],
        out_specs=[_row_spec(tm, GQA_W), _row_spec(tm, GQA_KV_W), _row_spec(tm, GQA_KV_W), _row_spec(tm, GQA_W)],
        compiler_params=_params(("parallel",)), name="proj_gqa")(x2, w, cos, sin, gmat, qn, kn)


def _retention_kernel(dec_ref, qf_ref, kf_ref, vf_ref, qb_ref, kb_ref, vb_ref,
                      of_ref, ob_ref, sf_ref, sb_ref, *, chunks):
    c = RET_CHUNK

    @pl.when(pl.program_id(2) == 0)
    def _():
        sf_ref[...] = jnp.zeros_like(sf_ref)
        sb_ref[...] = jnp.zeros_like(sb_ref)

    lf = dec_ref[0:1, :]
    lb = dec_ref[1:2, :]
    row = lax.broadcasted_iota(jnp.int32, (c, c), 0).astype(F32)
    col = lax.broadcasted_iota(jnp.int32, (c, c), 1).astype(F32)
    diff = row - col
    d_intra = jnp.where(diff >= 0, jnp.exp(jnp.maximum(diff, 0.0) * lf), jnp.exp(jnp.maximum(-diff, 0.0) * lb))
    w_kf = jnp.exp((c - 1.0 - row) * lf)
    w_qf = jnp.exp((row + 1.0) * lf)
    w_kb = jnp.exp(row * lb)
    w_qb = jnp.exp((c - row) * lb)
    chunk_f = jnp.exp(c * lf)
    chunk_b = jnp.exp(c * lb)

    s_f = sf_ref[...]
    for i in range(chunks):
        rows = slice(i * c, (i + 1) * c)
        q = qf_ref[rows, :]
        k = kf_ref[rows, :]
        v = vf_ref[rows, :]
        s = (_dot_nt(q, k) * d_intra).astype(BF16)
        o = _dot(s, v)
        qw = (q.astype(F32) * w_qf).astype(BF16)
        o = o + _dot(qw, s_f.astype(BF16))
        of_ref[rows, :] = o
        kw = (k.astype(F32) * w_kf).astype(BF16)
        s_f = chunk_f * s_f + _dot_tn(kw, v)
    sf_ref[...] = s_f

    s_b = sb_ref[...]
    for i in reversed(range(chunks)):
        rows = slice(i * c, (i + 1) * c)
        q = qb_ref[rows, :]
        k = kb_ref[rows, :]
        v = vb_ref[rows, :]
        qw = (q.astype(F32) * w_qb).astype(BF16)
        ob_ref[rows, :] = _dot(qw, s_b.astype(BF16))
        kw = (k.astype(F32) * w_kb).astype(BF16)
        s_b = chunk_b * s_b + _dot_tn(kw, v)
    sb_ref[...] = s_b


def _retention(rq, rk, rv, decay, batch, seq, chunks):
    m = rq.shape[0]
    rows = chunks * RET_CHUNK
    ng = seq // rows
    fwd = pl.BlockSpec((rows, RET_DK), lambda b, h, g: (b * ng + g, h))
    bwd = pl.BlockSpec((rows, RET_DK), lambda b, h, g: (b * ng + ng - 1 - g, h))
    dec = pl.BlockSpec((None, 8, LANES), lambda b, h, g: (h, 0, 0))
    out = [jax.ShapeDtypeStruct((m, RET_W), F32)] * 2
    return pl.pallas_call(
        functools.partial(_retention_kernel, chunks=chunks), out_shape=out,
        grid=(batch, RET_HEADS, ng),
        in_specs=[dec, fwd, fwd, fwd, bwd, bwd, bwd],
        out_specs=[fwd, bwd],
        scratch_shapes=[pltpu.VMEM((RET_DK, RET_DK), F32), pltpu.VMEM((RET_DK, RET_DK), F32)],
        compiler_params=_params(("parallel", "parallel", "arbitrary")),
        name="retention")(decay, rq, rk, rv, rq, rk, rv)


NA_QROWS = 4
NA_KROWS = 3 * NA_QROWS
NA_TQ = NA_QROWS * GRID_W
NA_TK = NA_KROWS * GRID_W


def _na_kernel(q_ref, k0_ref, k1_ref, k2_ref, v0_ref, v1_ref, v2_ref, bias_ref, z_ref, o_ref):
    q = q_ref[...]
    k = jnp.concatenate([k0_ref[...], k1_ref[...], k2_ref[...]], axis=0)
    v = jnp.concatenate([v0_ref[...], v1_ref[...], v2_ref[...]], axis=0)
    lane = lax.broadcasted_iota(jnp.int32, k.shape, 1)
    zero = jnp.zeros_like(k)
    acc = None
    for half in range(2):
        own = (lane < NA_DH) if half == 0 else (lane >= NA_DH)
        s = _dot_nt(q, jnp.where(own, k, zero)) + bias_ref[half]
        mx = jnp.max(s, axis=-1, keepdims=True)
        p = jnp.exp(s - mx)
        den = jnp.sum(p, axis=-1, keepdims=True)
        o = _dot(p.astype(BF16), jnp.where(own, v, zero)) / den
        acc = o if acc is None else acc + o
    o_ref[...] = (acc * z_ref[...]).astype(o_ref.dtype)


def _na_bias_tables(rpb, rows):
    nblk = rows // NA_QROWS
    nrow, ncol = 2 * NA_KH - 1, 2 * NA_KW - 1
    col = np.arange(GRID_W)
    c_start = np.clip(col - NA_KW // 2, 0, GRID_W - NA_KW)
    col_ok = (col[None, :] >= c_start[:, None]) & (col[None, :] < c_start[:, None] + NA_KW)
    cidx = np.clip(col[None, :] - col[:, None] + NA_KW - 1, 0, ncol - 1)
    col_sel = (cidx[:, :, None] == np.arange(ncol)).astype(np.float32)
    row_sel = np.zeros((3, NA_QROWS, NA_KROWS, nrow), np.float32)
    row_ok = np.zeros((3, NA_QROWS, NA_KROWS), bool)
    for cls, blk in enumerate((0, 1, nblk - 1)):
        first_kblk = min(max(blk - 1, 0), nblk - 3)
        qr = blk * NA_QROWS + np.arange(NA_QROWS)
        kr = first_kblk * NA_QROWS + np.arange(NA_KROWS)
        r_start = np.clip(qr - NA_KH // 2, 0, rows - NA_KH)
        ok = (kr[None, :] >= r_start[:, None]) & (kr[None, :] < r_start[:, None] + NA_KH)
        ridx = kr[None, :] - qr[:, None] + NA_KH - 1
        row_ok[cls] = ok
        row_sel[cls] = ((ridx[:, :, None] == np.arange(nrow)) & ok[:, :, None])
    by_col = jnp.einsum('har,qkr->haqk', rpb.astype(F32), col_sel, precision=lax.Precision.HIGHEST)
    vals = jnp.einsum('cija,haqk->chiqjk', row_sel, by_col, precision=lax.Precision.HIGHEST)
    valid = row_ok[:, None, :, None, :, None] & col_ok[None, None, None, :, None, :]
    table = jnp.where(valid, vals, -jnp.inf)
    return table.reshape(3, NA_HEADS, NA_TQ, NA_TK)


def _na_attention(nq, nk, nv, nz, bias, batch, seq):
    m = nq.shape[0]
    rows = seq // GRID_W
    nblk = rows // NA_QROWS
    pairs = NA_HEADS // 2

    def kblk(j):
        return lambda b, hp, i: (b * nblk + jnp.clip(i - 1, 0, nblk - 3) + j, hp)

    def bias_idx(b, hp, i):
        cls = jnp.where(i == 0, 0, jnp.where(i == nblk - 1, 2, 1))
        return (cls, hp, 0, 0)

    qspec = pl.BlockSpec((NA_TQ, LANES), lambda b, hp, i: (b * nblk + i, hp))
    kspecs = [pl.BlockSpec((NA_TQ, LANES), kblk(j)) for j in range(3)]
    return pl.pallas_call(
        _na_kernel, out_shape=jax.ShapeDtypeStruct((m, NA_W), BF16),
        grid=(batch, pairs, nblk),
        in_specs=[qspec] + kspecs + kspecs + [pl.BlockSpec((None, 2, NA_TQ, NA_TK), bias_idx), qspec],
        out_specs=qspec,
        compiler_params=_params(("parallel", "parallel", "arbitrary")),
        name="na_attention")(nq, nk, nk, nk, nv, nv, nv, bias, nz)


GQA_GROUP = GQA_HEADS // GQA_KV_HEADS
LOG2E = math.log2(math.e)


def _gqa_kernel(q_ref, kt_ref, v_ref, z_ref, o_ref,
                qs_ref, s_ref, p_ref, mb_ref, a_ref, m_ref, acc_ref, *, tk, rb, unroll):
    tq = q_ref.shape[0]
    rows = GQA_GROUP * tq
    nk = kt_ref.shape[1] // tk
    assert nk % 2 == 0 and unroll % 2 == 0 and nk >= 2 + unroll

    def chunk_start(c):
        return c * tk if isinstance(c, int) else pl.multiple_of(c * tk, tk)
    for g in range(GQA_GROUP):
        qs_ref[g * tq:(g + 1) * tq, :] = q_ref[:, g * GQA_DH:(g + 1) * GQA_DH]
    m_ref[...] = jnp.full(m_ref.shape, -jnp.inf, F32)
    acc_ref[...] = jnp.zeros_like(acc_ref)

    def scores(c, par):
        start = chunk_start(c)
        kc = kt_ref[:, pl.ds(start, tk)]
        for r in range(0, rows, rb):
            rs = slice(r, r + rb)
            s = _dot(qs_ref[rs, :], kc)
            s_ref[par, rs, :] = s
            mg = functools.reduce(jnp.maximum, [s[:, j * LANES:(j + 1) * LANES] for j in range(tk // LANES)])
            mb_ref[par, rs, :] = jnp.broadcast_to(jnp.max(mg, axis=-1, keepdims=True), (rb, LANES))

    def softmax(par):
        m_old = m_ref[...]
        m_new = jnp.maximum(m_old, mb_ref[par])
        a_ref[par] = jnp.exp2(m_old - m_new)
        m_ref[...] = m_new
        for j in range(tk // LANES):
            cs = slice(j * LANES, (j + 1) * LANES)
            p_ref[par, :, cs] = jnp.exp2(s_ref[par, :, cs] - m_new).astype(BF16)

    def values(c, par):
        start = chunk_start(c)
        acc_ref[...] = a_ref[par] * acc_ref[...] + _dot(p_ref[par], v_ref[pl.ds(start, tk), :])

    def tick(i, par):
        scores(i, par)
        softmax(1 - par)
        values(i - 2, par)

    scores(0, 0)
    scores(1, 1)
    softmax(0)
    first = 2 + (nk - 2) % unroll
    for i in range(2, first):
        tick(i, i % 2)

    def body(j, carry):
        i = first + unroll * j
        for u in range(unroll):
            tick(i + u, (first + u) % 2)
        return carry

    lax.fori_loop(0, (nk - first) // unroll, body, 0)
    softmax(1)
    values(nk - 2, 0)
    values(nk - 1, 1)

    outs = []
    for g in range(GQA_GROUP):
        a = acc_ref[g * tq:(g + 1) * tq, :]
        outs.append(a[:, 0:GQA_DH] / a[:, GQA_DH:GQA_DH + 1])
    o_ref[...] = (jnp.concatenate(outs, axis=-1) * z_ref[...]).astype(o_ref.dtype)


def _gqa_attention(cq, kt, va, cz, batch, seq, tq, tk, rb, unroll):
    m = cq.shape[0]
    nq = seq // tq
    width = GQA_GROUP * GQA_DH
    rows = GQA_GROUP * tq
    qspec = pl.BlockSpec((tq, width), lambda b, n, i: (b * nq + i, n))
    return pl.pallas_call(
        functools.partial(_gqa_kernel, tk=tk, rb=rb, unroll=unroll), out_shape=jax.ShapeDtypeStruct((m, GQA_W), BF16),
        grid=(batch, GQA_KV_HEADS, nq),
        in_specs=[qspec,
                  pl.BlockSpec((None, GQA_DH, seq), lambda b, n, i: (b * GQA_KV_HEADS + n, 0, 0)),
                  pl.BlockSpec((None, seq, LANES), lambda b, n, i: (b * GQA_KV_HEADS + n, 0, 0)),
                  qspec],
        out_specs=qspec,
        scratch_shapes=[pltpu.VMEM((rows, GQA_DH), BF16),
                        pltpu.VMEM((2, rows, tk), F32),
                        pltpu.VMEM((2, rows, tk), BF16),
                        pltpu.VMEM((2, rows, LANES), F32),
                        pltpu.VMEM((2, rows, LANES), F32),
                        pltpu.VMEM((rows, LANES), F32),
                        pltpu.VMEM((rows, LANES), F32)],
        compiler_params=_params(("parallel", "parallel", "arbitrary")),
        name="gqa_attention")(cq, kt, va, cz)


def _merge_kernel(x_ref, of_ref, ob_ref, rz_ref, gn_ref, onb_ref, ogq_ref,
                  wg_ref, wa_ref, wb_ref, wc_ref, wo_ref, lg_ref, lb_ref, o_ref, *, alpha):
    x = x_ref[...]
    xb = x.astype(BF16)
    ra = of_ref[...] + ob_ref[...]
    parts = []
    for h in range(RET_HEADS):
        t = ra[:, h * RET_DK:(h + 1) * RET_DK]
        mu = jnp.mean(t, axis=-1, keepdims=True)
        d = t - mu
        var = jnp.mean(d * d, axis=-1, keepdims=True)
        parts.append(d * lax.rsqrt(var + LN_EPS))
    o_a = (jnp.concatenate(parts, axis=-1) * gn_ref[...] * rz_ref[...]).astype(BF16)
    y = _sigmoid(_dot(xb, wg_ref[:, 0:D_MODEL])) * _dot(o_a, wa_ref[...])
    y = y + _sigmoid(_dot(xb, wg_ref[:, D_MODEL:2 * D_MODEL])) * _dot(onb_ref[...], wb_ref[...])
    y = y + _sigmoid(_dot(xb, wg_ref[:, 2 * D_MODEL:3 * D_MODEL])) * _dot(ogq_ref[...], wc_ref[...])
    z = alpha * x + _dot(y.astype(BF16), wo_ref[...])
    mu = jnp.mean(z, axis=-1, keepdims=True)
    d = z - mu
    var = jnp.mean(d * d, axis=-1, keepdims=True)
    o_ref[...] = d * lax.rsqrt(var + LN_EPS) * lg_ref[...] + lb_ref[...]


def _merge(x2, o_f, o_b, rz, gn, o_nb, o_gq, wg, wa, wb, wc, wo, lg, lb, alpha, tm):
    m = x2.shape[0]
    return pl.pallas_call(
        functools.partial(_merge_kernel, alpha=alpha), out_shape=jax.ShapeDtypeStruct((m, D_MODEL), F32),
        grid=(m // tm,),
        in_specs=[_row_spec(tm, D_MODEL), _row_spec(tm, RET_W), _row_spec(tm, RET_W), _row_spec(tm, RET_W),
                  _const_spec(gn.shape), _row_spec(tm, NA_W), _row_spec(tm, GQA_W),
                  _const_spec(wg.shape), _const_spec(wa.shape), _const_spec(wb.shape), _const_spec(wc.shape),
                  _const_spec(wo.shape), _const_spec(lg.shape), _const_spec(lb.shape)],
        out_specs=_row_spec(tm, D_MODEL),
        compiler_params=_params(("parallel",)), name="merge")(
            x2, o_f, o_b, rz, gn, o_nb, o_gq, wg, wa, wb, wc, wo, lg, lb)


PROJ_TM = 512
MERGE_TM = 256
RET_CHUNKS_PER_STEP = 4
GQA_TQ = 256
GQA_TK = 512
GQA_RB = 256
GQA_UNROLL = 4


def _rope_tables(seq):
    t = jnp.arange(seq)
    tf = t.astype(F32)
    row = (t // GRID_W).astype(F32)
    col = (t % GRID_W).astype(F32)
    inv_ret = 1.0 / (ROPE_THETA ** jnp.linspace(0.0, 1.0, RET_DK // 2, dtype=F32))
    ang_ret = tf[:, None] * inv_ret[None, :]
    n_ax = GQA_DH // 4
    inv_ax = ROPE_THETA ** (-jnp.arange(n_ax, dtype=F32) / n_ax)
    ang_ax = jnp.concatenate([row[:, None] * inv_ax[None, :], col[:, None] * inv_ax[None, :]], -1)

    def full(ang, reps):
        c, s = jnp.cos(ang), jnp.sin(ang)
        return jnp.tile(jnp.concatenate([c, c], -1), (1, reps)), jnp.tile(jnp.concatenate([-s, s], -1), (1, reps))

    return full(ang_ret, 1), full(ang_ax, LANES // GQA_DH)


def kernel(x, w_in, ret_theta_fwd, ret_theta_bwd, ret_gn_gain, na_rpb, gqa_q_norm, gqa_k_norm,
           w_branch_a, w_branch_b, w_branch_c, w_out, ln_gain, ln_bias):
    batch, seq, d_model = x.shape
    depth = w_in.shape[0]
    assert d_model == D_MODEL and w_in.shape[-1] == D_IN and seq % (GRID_W * NA_QROWS) == 0
    alpha = (2.0 * depth) ** 0.25
    m = batch * seq
    (ret_cos, ret_sin), (ax_cos, ax_sin) = _rope_tables(seq)
    gidx = np.arange(GQA_W) // GQA_DH
    gmat = jnp.asarray(gidx[:, None] == gidx[None, :], BF16)
    ones_col = jnp.concatenate([jnp.ones((m, 1), BF16), jnp.zeros((m, LANES - GQA_DH - 1), BF16)], -1)

    x2 = x.reshape(m, D_MODEL)
    for l in range(depth):
        w = w_in[l].astype(BF16)
        rq, rk, rv, rz = _proj_ret(x2, w[:, OFF_RET:OFF_NA], ret_cos, ret_sin, seq, PROJ_TM)
        nq, nk, nv, nz = _proj_na(x2, w[:, OFF_NA:OFF_GQA], PROJ_TM)
        qn = jnp.tile(gqa_q_norm[l], GQA_HEADS)[None, :]
        kn = jnp.tile(gqa_k_norm[l], GQA_KV_HEADS)[None, :]
        cq, ck, cv, cz = _proj_gqa(x2, w[:, OFF_GQA:OFF_GATES], ax_cos, ax_sin, gmat, qn, kn, seq, PROJ_TM)

        decay = jnp.stack([jax.nn.log_sigmoid(ret_theta_fwd[l].astype(F32)),
                           jax.nn.log_sigmoid(ret_theta_bwd[l].astype(F32))], axis=1)
        decay = jnp.broadcast_to(jnp.pad(decay, ((0, 0), (0, 6)))[:, :, None], (RET_HEADS, 8, LANES))
        o_f, o_b = _retention(rq, rk, rv, decay, batch, seq, RET_CHUNKS_PER_STEP)

        bias = _na_bias_tables(na_rpb[l], seq // GRID_W)
        o_nb = _na_attention(nq, nk, nv, nz, bias, batch, seq)

        kt = ck.reshape(batch, seq, GQA_KV_HEADS, GQA_DH).transpose(0, 2, 3, 1).reshape(
            batch * GQA_KV_HEADS, GQA_DH, seq)
        va = jnp.concatenate([cv.reshape(m, GQA_KV_HEADS, GQA_DH),
                              jnp.broadcast_to(ones_col[:, None, :], (m, GQA_KV_HEADS, LANES - GQA_DH))], -1)
        va = va.reshape(batch, seq, GQA_KV_HEADS, LANES).transpose(0, 2, 1, 3).reshape(
            batch * GQA_KV_HEADS, seq, LANES)
        o_gq = _gqa_attention(cq, kt, va, cz, batch, seq, GQA_TQ, GQA_TK, GQA_RB, GQA_UNROLL)

        x2 = _merge(x2, o_f, o_b, rz, ret_gn_gain[l][None, :], o_nb, o_gq,
                    w[:, OFF_GATES:D_IN], w_branch_a[l].astype(BF16), w_branch_b[l].astype(BF16),
                    w_branch_c[l].astype(BF16), w_out[l].astype(BF16),
                    ln_gain[l][None, :], ln_bias[l][None, :], alpha, MERGE_TM)
    return x2.reshape(batch, seq, D_MODEL)
```

```python
import functools
import math

import numpy as np
import jax
import jax.numpy as jnp
from jax import lax
from jax.experimental import pallas as pl
from jax.experimental.pallas import tpu as pltpu

D_MODEL = 1024
GRID_W = 64
RET_HEADS = 4
RET_DK = 128
RET_CHUNK = 128
NA_HEADS = 8
NA_DH = 64
NA_KH = 8
NA_KW = 16
GQA_HEADS = 8
GQA_KV_HEADS = 2
GQA_DH = 64
ROPE_THETA = 10000.0
RMS_EPS = 1e-6
LN_EPS = 1e-5

RET_W = RET_HEADS * RET_DK
NA_W = NA_HEADS * NA_DH
GQA_W = GQA_HEADS * GQA_DH
GQA_KV_W = GQA_KV_HEADS * GQA_DH

OFF_RET = 0
OFF_NA = 4 * RET_W
OFF_GQA = OFF_NA + 4 * NA_W
OFF_GATES = OFF_GQA + 2 * GQA_W + 2 * GQA_KV_W
D_IN = OFF_GATES + 3 * D_MODEL

LANES = 128
VMEM_LIMIT = 56 * 1024 * 1024

BF16 = jnp.bfloat16
F32 = jnp.float32


def _dot(a, b):
    return jnp.dot(a, b, preferred_element_type=F32)


def _dot_nt(a, b):
    return lax.dot_general(a, b, (((1,), (1,)), ((), ())), preferred_element_type=F32)


def _dot_tn(a, b):
    return lax.dot_general(a, b, (((0,), (0,)), ((), ())), preferred_element_type=F32)


def _silu(z):
    return z / (1.0 + jnp.exp(-z))


def _sigmoid(z):
    return 1.0 / (1.0 + jnp.exp(-z))


def _params(sem):
    return pltpu.CompilerParams(dimension_semantics=sem, vmem_limit_bytes=VMEM_LIMIT)


def _proj_ret_kernel(x_ref, w_ref, cos_ref, sin_ref, q_ref, k_ref, v_ref, z_ref):
    xb = x_ref[...].astype(BF16)
    cos = cos_ref[...]
    sin = sin_ref[...]
    scale = RET_DK ** -0.5
    yq = _dot(xb, w_ref[:, 0:RET_W])
    yk = _dot(xb, w_ref[:, RET_W:2 * RET_W])
    for h in range(RET_HEADS):
        sl = slice(h * RET_DK, (h + 1) * RET_DK)
        t = yq[:, sl]
        q_ref[:, sl] = ((t * cos + pltpu.roll(t, RET_DK // 2, 1) * sin) * scale).astype(BF16)
        t = yk[:, sl]
        k_ref[:, sl] = (t * cos + pltpu.roll(t, RET_DK // 2, 1) * sin).astype(BF16)
    v_ref[...] = _dot(xb, w_ref[:, 2 * RET_W:3 * RET_W]).astype(BF16)
    z_ref[...] = _silu(_dot(xb, w_ref[:, 3 * RET_W:4 * RET_W]))


def _proj_na_kernel(x_ref, w_ref, q_ref, k_ref, v_ref, z_ref):
    xb = x_ref[...].astype(BF16)
    q_ref[...] = (_dot(xb, w_ref[:, 0:NA_W]) * (NA_DH ** -0.5)).astype(BF16)
    k_ref[...] = _dot(xb, w_ref[:, NA_W:2 * NA_W]).astype(BF16)
    v_ref[...] = _dot(xb, w_ref[:, 2 * NA_W:3 * NA_W]).astype(BF16)
    z_ref[...] = _silu(_dot(xb, w_ref[:, 3 * NA_W:4 * NA_W]))


def _group_mean_sq(y, g_ref, width):
    sq = y * y
    hi = sq.astype(BF16)
    lo = (sq - hi.astype(F32)).astype(BF16)
    g = g_ref[0:width, 0:width]
    return (_dot(hi, g) + _dot(lo, g)) * (1.0 / GQA_DH)


def _rope64(t, cos, sin, first_half):
    swapped = jnp.where(first_half, pltpu.roll(t, LANES - GQA_DH // 2, 1), pltpu.roll(t, GQA_DH // 2, 1))
    return t * cos + swapped * sin


def _proj_gqa_kernel(x_ref, w_ref, cos_ref, sin_ref, g_ref, qn_ref, kn_ref,
                     q_ref, k_ref, v_ref, z_ref):
    xb = x_ref[...].astype(BF16)
    cos = cos_ref[...]
    sin = sin_ref[...]
    lane = lax.broadcasted_iota(jnp.int32, cos.shape, 1)
    first_half = (lane % GQA_DH) < (GQA_DH // 2)
    yq = _dot(xb, w_ref[:, 0:GQA_W])
    qn = yq * lax.rsqrt(_group_mean_sq(yq, g_ref, GQA_W) + RMS_EPS) * qn_ref[...]
    for j in range(GQA_W // LANES):
        sl = slice(j * LANES, (j + 1) * LANES)
        q_ref[:, sl] = (_rope64(qn[:, sl], cos, sin, first_half) * (LOG2E * GQA_DH ** -0.5)).astype(BF16)
    yk = _dot(xb, w_ref[:, GQA_W:GQA_W + GQA_KV_W])
    kn = yk * lax.rsqrt(_group_mean_sq(yk, g_ref, GQA_KV_W) + RMS_EPS) * kn_ref[...]
    k_ref[...] = _rope64(kn, cos, sin, first_half).astype(BF16)
    v_ref[...] = _dot(xb, w_ref[:, GQA_W + GQA_KV_W:GQA_W + 2 * GQA_KV_W]).astype(BF16)
    z_ref[...] = _silu(_dot(xb, w_ref[:, GQA_W + 2 * GQA_KV_W:2 * GQA_W + 2 * GQA_KV_W]))


def _row_spec(tm, width):
    return pl.BlockSpec((tm, width), lambda i: (i, 0))


def _const_spec(shape):
    return pl.BlockSpec(shape, lambda i: (0,) * len(shape))


def _table_spec(tm, seq):
    nblk = seq // tm
    return pl.BlockSpec((tm, LANES), lambda i: (i % nblk, 0))


def _proj_ret(x2, w, cos, sin, seq, tm):
    m = x2.shape[0]
    out = [jax.ShapeDtypeStruct((m, RET_W), BF16)] * 3 + [jax.ShapeDtypeStruct((m, RET_W), F32)]
    return pl.pallas_call(
        _proj_ret_kernel, out_shape=out, grid=(m // tm,),
        in_specs=[_row_spec(tm, D_MODEL), _const_spec(w.shape), _table_spec(tm, seq), _table_spec(tm, seq)],
        out_specs=[_row_spec(tm, RET_W)] * 4,
        compiler_params=_params(("parallel",)), name="proj_ret")(x2, w, cos, sin)


def _proj_na(x2, w, tm):
    m = x2.shape[0]
    out = [jax.ShapeDtypeStruct((m, NA_W), BF16)] * 3 + [jax.ShapeDtypeStruct((m, NA_W), F32)]
    return pl.pallas_call(
        _proj_na_kernel, out_shape=out, grid=(m // tm,),
        in_specs=[_row_spec(tm, D_MODEL), _const_spec(w.shape)],
        out_specs=[_row_spec(tm, NA_W)] * 4,
        compiler_params=_params(("parallel",)), name="proj_na")(x2, w)


def _proj_gqa(x2, w, cos, sin, gmat, qn, kn, seq, tm):
    m = x2.shape[0]
    out = [jax.ShapeDtypeStruct((m, GQA_W), BF16), jax.ShapeDtypeStruct((m, GQA_KV_W), BF16),
           jax.ShapeDtypeStruct((m, GQA_KV_W), BF16), jax.ShapeDtypeStruct((m, GQA_W), F32)]
    return pl.pallas_call(
        _proj_gqa_kernel, out_shape=out, grid=(m // tm,),
        in_specs=[_row_spec(tm, D_MODEL), _const_spec(w.shape), _table_spec(tm, seq), _table_spec(tm, seq),
                  _const_spec(gmat.shape), _const_spec(qn.shape), _const_spec(kn.shape)],
        out_specs=[_row_spec(tm, GQA_W), _row_spec(tm, GQA_KV_W), _row_spec(tm, GQA_KV_W), _row_spec(tm, GQA_W)],
        compiler_params=_params(("parallel",)), name="proj_gqa")(x2, w, cos, sin, gmat, qn, kn)


def _retention_kernel(dec_ref, qf_ref, kf_ref, vf_ref, qb_ref, kb_ref, vb_ref,
                      of_ref, ob_ref, sf_ref, sb_ref, *, chunks):
    c = RET_CHUNK

    @pl.when(pl.program_id(2) == 0)
    def _():
        sf_ref[...] = jnp.zeros_like(sf_ref)
        sb_ref[...] = jnp.zeros_like(sb_ref)

    lf = dec_ref[0:1, :]
    lb = dec_ref[1:2, :]
    row = lax.broadcasted_iota(jnp.int32, (c, c), 0).astype(F32)
    col = lax.broadcasted_iota(jnp.int32, (c, c), 1).astype(F32)
    diff = row - col
    d_intra = jnp.where(diff >= 0, jnp.exp(jnp.maximum(diff, 0.0) * lf), jnp.exp(jnp.maximum(-diff, 0.0) * lb))
    w_kf = jnp.exp((c - 1.0 - row) * lf)
    w_qf = jnp.exp((row + 1.0) * lf)
    w_kb = jnp.exp(row * lb)
    w_qb = jnp.exp((c - row) * lb)
    chunk_f = jnp.exp(c * lf)
    chunk_b = jnp.exp(c * lb)

    s_f = sf_ref[...]
    for i in range(chunks):
        rows = slice(i * c, (i + 1) * c)
        q = qf_ref[rows, :]
        k = kf_ref[rows, :]
        v = vf_ref[rows, :]
        s = (_dot_nt(q, k) * d_intra).astype(BF16)
        o = _dot(s, v)
        qw = (q.astype(F32) * w_qf).astype(BF16)
        o = o + _dot(qw, s_f.astype(BF16))
        of_ref[rows, :] = o
        kw = (k.astype(F32) * w_kf).astype(BF16)
        s_f = chunk_f * s_f + _dot_tn(kw, v)
    sf_ref[...] = s_f

    s_b = sb_ref[...]
    for i in reversed(range(chunks)):
        rows = slice(i * c, (i + 1) * c)
        q = qb_ref[rows, :]
        k = kb_ref[rows, :]
        v = vb_ref[rows, :]
        qw = (q.astype(F32) * w_qb).astype(BF16)
        ob_ref[rows, :] = _dot(qw, s_b.astype(BF16))
        kw = (k.astype(F32) * w_kb).astype(BF16)
        s_b = chunk_b * s_b + _dot_tn(kw, v)
    sb_ref[...] = s_b


def _retention(rq, rk, rv, decay, batch, seq, chunks):
    m = rq.shape[0]
    rows = chunks * RET_CHUNK
    ng = seq // rows
    fwd = pl.BlockSpec((rows, RET_DK), lambda b, h, g: (b * ng + g, h))
    bwd = pl.BlockSpec((rows, RET_DK), lambda b, h, g: (b * ng + ng - 1 - g, h))
    dec = pl.BlockSpec((None, 8, LANES), lambda b, h, g: (h, 0, 0))
    out = [jax.ShapeDtypeStruct((m, RET_W), F32)] * 2
    return pl.pallas_call(
        functools.partial(_retention_kernel, chunks=chunks), out_shape=out,
        grid=(batch, RET_HEADS, ng),
        in_specs=[dec, fwd, fwd, fwd, bwd, bwd, bwd],
        out_specs=[fwd, bwd],
        scratch_shapes=[pltpu.VMEM((RET_DK, RET_DK), F32), pltpu.VMEM((RET_DK, RET_DK), F32)],
        compiler_params=_params(("parallel", "parallel", "arbitrary")),
        name="retention")(decay, rq, rk, rv, rq, rk, rv)


NA_QROWS = 4
NA_KROWS = 3 * NA_QROWS
NA_TQ = NA_QROWS * GRID_W
NA_TK = NA_KROWS * GRID_W
NA_PAIRS = 4


def _na_kernel(q_ref, k0_ref, k1_ref, k2_ref, v0_ref, v1_ref, v2_ref, bias_ref, z_ref, o_ref):
    lane = lax.broadcasted_iota(jnp.int32, (NA_TK, LANES), 1)
    zero = jnp.zeros((NA_TK, LANES), BF16)
    for pair in range(q_ref.shape[1] // LANES):
        cs = slice(pair * LANES, (pair + 1) * LANES)
        q = q_ref[:, cs]
        k = jnp.concatenate([k0_ref[:, cs], k1_ref[:, cs], k2_ref[:, cs]], axis=0)
        v = jnp.concatenate([v0_ref[:, cs], v1_ref[:, cs], v2_ref[:, cs]], axis=0)
        acc = None
        for half in range(2):
            own = (lane < NA_DH) if half == 0 else (lane >= NA_DH)
            s = _dot_nt(q, jnp.where(own, k, zero)) + bias_ref[2 * pair + half]
            mx = jnp.max(s, axis=-1, keepdims=True)
            p = jnp.exp(s - mx)
            den = jnp.sum(p, axis=-1, keepdims=True)
            o = _dot(p.astype(BF16), jnp.where(own, v, zero)) / den
            acc = o if acc is None else acc + o
        o_ref[:, cs] = (acc * z_ref[:, cs]).astype(o_ref.dtype)


def _na_bias_tables(rpb, rows):
    nblk = rows // NA_QROWS
    nrow, ncol = 2 * NA_KH - 1, 2 * NA_KW - 1
    col = np.arange(GRID_W)
    c_start = np.clip(col - NA_KW // 2, 0, GRID_W - NA_KW)
    col_ok = (col[None, :] >= c_start[:, None]) & (col[None, :] < c_start[:, None] + NA_KW)
    cidx = np.clip(col[None, :] - col[:, None] + NA_KW - 1, 0, ncol - 1)
    col_sel = (cidx[:, :, None] == np.arange(ncol)).astype(np.float32)
    row_sel = np.zeros((3, NA_QROWS, NA_KROWS, nrow), np.float32)
    row_ok = np.zeros((3, NA_QROWS, NA_KROWS), bool)
    for cls, blk in enumerate((0, 1, nblk - 1)):
        first_kblk = min(max(blk - 1, 0), nblk - 3)
        qr = blk * NA_QROWS + np.arange(NA_QROWS)
        kr = first_kblk * NA_QROWS + np.arange(NA_KROWS)
        r_start = np.clip(qr - NA_KH // 2, 0, rows - NA_KH)
        ok = (kr[None, :] >= r_start[:, None]) & (kr[None, :] < r_start[:, None] + NA_KH)
        ridx = kr[None, :] - qr[:, None] + NA_KH - 1
        row_ok[cls] = ok
        row_sel[cls] = ((ridx[:, :, None] == np.arange(nrow)) & ok[:, :, None])
    by_col = jnp.einsum('har,qkr->haqk', rpb.astype(F32), col_sel, precision=lax.Precision.HIGHEST)
    vals = jnp.einsum('cija,haqk->chiqjk', row_sel, by_col, precision=lax.Precision.HIGHEST)
    valid = row_ok[:, None, :, None, :, None] & col_ok[None, None, None, :, None, :]
    table = jnp.where(valid, vals, -jnp.inf)
    return table.reshape(3, NA_HEADS, NA_TQ, NA_TK)


def _na_attention(nq, nk, nv, nz, bias, batch, seq):
    m = nq.shape[0]
    rows = seq // GRID_W
    nblk = rows // NA_QROWS
    groups = NA_HEADS // (2 * NA_PAIRS)
    width = NA_PAIRS * LANES

    def kblk(j):
        return lambda b, hp, i: (b * nblk + jnp.clip(i - 1, 0, nblk - 3) + j, hp)

    def bias_idx(b, hp, i):
        cls = jnp.where(i == 0, 0, jnp.where(i == nblk - 1, 2, 1))
        return (cls, hp, 0, 0)

    qspec = pl.BlockSpec((NA_TQ, width), lambda b, hp, i: (b * nblk + i, hp))
    kspecs = [pl.BlockSpec((NA_TQ, width), kblk(j)) for j in range(3)]
    return pl.pallas_call(
        _na_kernel, out_shape=jax.ShapeDtypeStruct((m, NA_W), BF16),
        grid=(batch, groups, nblk),
        in_specs=[qspec] + kspecs + kspecs + [pl.BlockSpec((None, 2 * NA_PAIRS, NA_TQ, NA_TK), bias_idx), qspec],
        out_specs=qspec,
        compiler_params=_params(("parallel", "parallel", "arbitrary")),
        name="na_attention")(nq, nk, nk, nk, nv, nv, nv, bias, nz)


GQA_GROUP = GQA_HEADS // GQA_KV_HEADS
LOG2E = math.log2(math.e)


def _gqa_kernel(q_ref, kt_ref, v_ref, z_ref, o_ref,
                qs_ref, s_ref, p_ref, mb_ref, a_ref, m_ref, acc_ref, *, tk, rb, unroll):
    tq = q_ref.shape[0]
    rows = GQA_GROUP * tq
    nk = kt_ref.shape[1] // tk
    assert nk % 2 == 0 and unroll % 2 == 0 and nk >= 2 + unroll

    def chunk_start(c):
        return c * tk if isinstance(c, int) else pl.multiple_of(c * tk, tk)
    for g in range(GQA_GROUP):
        qs_ref[g * tq:(g + 1) * tq, :] = q_ref[:, g * GQA_DH:(g + 1) * GQA_DH]
    m_ref[...] = jnp.full(m_ref.shape, -jnp.inf, F32)
    acc_ref[...] = jnp.zeros_like(acc_ref)

    def scores(c, par):
        start = chunk_start(c)
        kc = kt_ref[:, pl.ds(start, tk)]
        for r in range(0, rows, rb):
            rs = slice(r, r + rb)
            s = _dot(qs_ref[rs, :], kc)
            s_ref[par, rs, :] = s
            mg = functools.reduce(jnp.maximum, [s[:, j * LANES:(j + 1) * LANES] for j in range(tk // LANES)])
            mb_ref[par, rs, :] = jnp.broadcast_to(jnp.max(mg, axis=-1, keepdims=True), (rb, LANES))

    def softmax(par):
        m_old = m_ref[...]
        m_new = jnp.maximum(m_old, mb_ref[par])
        a_ref[par] = jnp.exp2(m_old - m_new)
        m_ref[...] = m_new
        for j in range(tk // LANES):
            cs = slice(j * LANES, (j + 1) * LANES)
            p_ref[par, :, cs] = jnp.exp2(s_ref[par, :, cs] - m_new).astype(BF16)

    def values(c, par):
        start = chunk_start(c)
        acc_ref[...] = a_ref[par] * acc_ref[...] + _dot(p_ref[par], v_ref[pl.ds(start, tk), :])

    def tick(i, par):
        scores(i, par)
        softmax(1 - par)
        values(i - 2, par)

    scores(0, 0)
    scores(1, 1)
    softmax(0)
    first = 2 + (nk - 2) % unroll
    for i in range(2, first):
        tick(i, i % 2)

    def body(j, carry):
        i = first + unroll * j
        for u in range(unroll):
            tick(i + u, (first + u) % 2)
        return carry

    lax.fori_loop(0, (nk - first) // unroll, body, 0)
    softmax(1)
    values(nk - 2, 0)
    values(nk - 1, 1)

    outs = []
    for g in range(GQA_GROUP):
        a = acc_ref[g * tq:(g + 1) * tq, :]
        outs.append(a[:, 0:GQA_DH] / a[:, GQA_DH:GQA_DH + 1])
    o_ref[...] = (jnp.concatenate(outs, axis=-1) * z_ref[...]).astype(o_ref.dtype)


def _gqa_attention(cq, kt, va, cz, batch, seq, tq, tk, rb, unroll):
    m = cq.shape[0]
    nq = seq // tq
    width = GQA_GROUP * GQA_DH
    rows = GQA_GROUP * tq
    qspec = pl.BlockSpec((tq, width), lambda b, n, i: (b * nq + i, n))
    return pl.pallas_call(
        functools.partial(_gqa_kernel, tk=tk, rb=rb, unroll=unroll), out_shape=jax.ShapeDtypeStruct((m, GQA_W), BF16),
        grid=(batch, GQA_KV_HEADS, nq),
        in_specs=[qspec,
                  pl.BlockSpec((None, GQA_DH, seq), lambda b, n, i: (b * GQA_KV_HEADS + n, 0, 0)),
                  pl.BlockSpec((None, seq, LANES), lambda b, n, i: (b * GQA_KV_HEADS + n, 0, 0)),
                  qspec],
        out_specs=qspec,
        scratch_shapes=[pltpu.VMEM((rows, GQA_DH), BF16),
                        pltpu.VMEM((2, rows, tk), F32),
                        pltpu.VMEM((2, rows, tk), BF16),
                        pltpu.VMEM((2, rows, LANES), F32),
                        pltpu.VMEM((2, rows, LANES), F32),
                        pltpu.VMEM((rows, LANES), F32),
                        pltpu.VMEM((rows, LANES), F32)],
        compiler_params=_params(("parallel", "parallel", "arbitrary")),
        name="gqa_attention")(cq, kt, va, cz)


def _merge_kernel(x_ref, of_ref, ob_ref, rz_ref, gn_ref, onb_ref, ogq_ref,
                  wg_ref, wa_ref, wb_ref, wc_ref, wo_ref, lg_ref, lb_ref, o_ref, *, alpha):
    x = x_ref[...]
    xb = x.astype(BF16)
    ra = of_ref[...] + ob_ref[...]
    parts = []
    for h in range(RET_HEADS):
        t = ra[:, h * RET_DK:(h + 1) * RET_DK]
        mu = jnp.mean(t, axis=-1, keepdims=True)
        d = t - mu
        var = jnp.mean(d * d, axis=-1, keepdims=True)
        parts.append(d * lax.rsqrt(var + LN_EPS))
    o_a = (jnp.concatenate(parts, axis=-1) * gn_ref[...] * rz_ref[...]).astype(BF16)
    y = _sigmoid(_dot(xb, wg_ref[:, 0:D_MODEL])) * _dot(o_a, wa_ref[...])
    y = y + _sigmoid(_dot(xb, wg_ref[:, D_MODEL:2 * D_MODEL])) * _dot(onb_ref[...], wb_ref[...])
    y = y + _sigmoid(_dot(xb, wg_ref[:, 2 * D_MODEL:3 * D_MODEL])) * _dot(ogq_ref[...], wc_ref[...])
    z = alpha * x + _dot(y.astype(BF16), wo_ref[...])
    mu = jnp.mean(z, axis=-1, keepdims=True)
    d = z - mu
    var = jnp.mean(d * d, axis=-1, keepdims=True)
    o_ref[...] = d * lax.rsqrt(var + LN_EPS) * lg_ref[...] + lb_ref[...]


def _merge(x2, o_f, o_b, rz, gn, o_nb, o_gq, wg, wa, wb, wc, wo, lg, lb, alpha, tm):
    m = x2.shape[0]
    return pl.pallas_call(
        functools.partial(_merge_kernel, alpha=alpha), out_shape=jax.ShapeDtypeStruct((m, D_MODEL), F32),
        grid=(m // tm,),
        in_specs=[_row_spec(tm, D_MODEL), _row_spec(tm, RET_W), _row_spec(tm, RET_W), _row_spec(tm, RET_W),
                  _const_spec(gn.shape), _row_spec(tm, NA_W), _row_spec(tm, GQA_W),
                  _const_spec(wg.shape), _const_spec(wa.shape), _const_spec(wb.shape), _const_spec(wc.shape),
                  _const_spec(wo.shape), _const_spec(lg.shape), _const_spec(lb.shape)],
        out_specs=_row_spec(tm, D_MODEL),
        compiler_params=_params(("parallel",)), name="merge")(
            x2, o_f, o_b, rz, gn, o_nb, o_gq, wg, wa, wb, wc, wo, lg, lb)


PROJ_TM = 512
MERGE_TM = 256
RET_CHUNKS_PER_STEP = 4
GQA_TQ = 256
GQA_TK = 512
GQA_RB = 256
GQA_UNROLL = 6


def _rope_tables(seq):
    t = jnp.arange(seq)
    tf = t.astype(F32)
    row = (t // GRID_W).astype(F32)
    col = (t % GRID_W).astype(F32)
    inv_ret = 1.0 / (ROPE_THETA ** jnp.linspace(0.0, 1.0, RET_DK // 2, dtype=F32))
    ang_ret = tf[:, None] * inv_ret[None, :]
    n_ax = GQA_DH // 4
    inv_ax = ROPE_THETA ** (-jnp.arange(n_ax, dtype=F32) / n_ax)
    ang_ax = jnp.concatenate([row[:, None] * inv_ax[None, :], col[:, None] * inv_ax[None, :]], -1)

    def full(ang, reps):
        c, s = jnp.cos(ang), jnp.sin(ang)
        return jnp.tile(jnp.concatenate([c, c], -1), (1, reps)), jnp.tile(jnp.concatenate([-s, s], -1), (1, reps))

    return full(ang_ret, 1), full(ang_ax, LANES // GQA_DH)


def kernel(x, w_in, ret_theta_fwd, ret_theta_bwd, ret_gn_gain, na_rpb, gqa_q_norm, gqa_k_norm,
           w_branch_a, w_branch_b, w_branch_c, w_out, ln_gain, ln_bias):
    batch, seq, d_model = x.shape
    depth = w_in.shape[0]
    assert d_model == D_MODEL and w_in.shape[-1] == D_IN and seq % (GRID_W * NA_QROWS) == 0
    alpha = (2.0 * depth) ** 0.25
    m = batch * seq
    (ret_cos, ret_sin), (ax_cos, ax_sin) = _rope_tables(seq)
    gidx = np.arange(GQA_W) // GQA_DH
    gmat = jnp.asarray(gidx[:, None] == gidx[None, :], BF16)
    ones_col = jnp.concatenate([jnp.ones((m, 1), BF16), jnp.zeros((m, LANES - GQA_DH - 1), BF16)], -1)

    x2 = x.reshape(m, D_MODEL)
    for l in range(depth):
        w = w_in[l].astype(BF16)
        rq, rk, rv, rz = _proj_ret(x2, w[:, OFF_RET:OFF_NA], ret_cos, ret_sin, seq, PROJ_TM)
        nq, nk, nv, nz = _proj_na(x2, w[:, OFF_NA:OFF_GQA], PROJ_TM)
        qn = jnp.tile(gqa_q_norm[l], GQA_HEADS)[None, :]
        kn = jnp.tile(gqa_k_norm[l], GQA_KV_HEADS)[None, :]
        cq, ck, cv, cz = _proj_gqa(x2, w[:, OFF_GQA:OFF_GATES], ax_cos, ax_sin, gmat, qn, kn, seq, PROJ_TM)

        decay = jnp.stack([jax.nn.log_sigmoid(ret_theta_fwd[l].astype(F32)),
                           jax.nn.log_sigmoid(ret_theta_bwd[l].astype(F32))], axis=1)
        decay = jnp.broadcast_to(jnp.pad(decay, ((0, 0), (0, 6)))[:, :, None], (RET_HEADS, 8, LANES))
        o_f, o_b = _retention(rq, rk, rv, decay, batch, seq, RET_CHUNKS_PER_STEP)

        bias = _na_bias_tables(na_rpb[l], seq // GRID_W)
        o_nb = _na_attention(nq, nk, nv, nz, bias, batch, seq)

        kt = ck.reshape(batch, seq, GQA_KV_HEADS, GQA_DH).transpose(0, 2, 3, 1).reshape(
            batch * GQA_KV_HEADS, GQA_DH, seq)
        va = jnp.concatenate([cv.reshape(m, GQA_KV_HEADS, GQA_DH),
                              jnp.broadcast_to(ones_col[:, None, :], (m, GQA_KV_HEADS, LANES - GQA_DH))], -1)
        va = va.reshape(batch, seq, GQA_KV_HEADS, LANES).transpose(0, 2, 1, 3).reshape(
            batch * GQA_KV_HEADS, seq, LANES)
        o_gq = _gqa_attention(cq, kt, va, cz, batch, seq, GQA_TQ, GQA_TK, GQA_RB, GQA_UNROLL)

        x2 = _merge(x2, o_f, o_b, rz, ret_gn_gain[l][None, :], o_nb, o_gq,
                    w[:, OFF_GATES:D_IN], w_branch_a[l].astype(BF16), w_branch_b[l].astype(BF16),
                    w_branch_c[l].astype(BF16), w_out[l].astype(BF16),
                    ln_gain[l][None, :], ln_bias[l][None, :], alpha, MERGE_TM)
    return x2.reshape(batch, seq, D_MODEL)
```

```python
import functools
import math

import numpy as np
import jax
import jax.numpy as jnp
from jax import lax
from jax.experimental import pallas as pl
from jax.experimental.pallas import tpu as pltpu

D_MODEL = 1024
GRID_W = 64
RET_HEADS = 4
RET_DK = 128
RET_CHUNK = 128
NA_HEADS = 8
NA_DH = 64
NA_KH = 8
NA_KW = 16
GQA_HEADS = 8
GQA_KV_HEADS = 2
GQA_DH = 64
ROPE_THETA = 10000.0
RMS_EPS = 1e-6
LN_EPS = 1e-5

RET_W = RET_HEADS * RET_DK
NA_W = NA_HEADS * NA_DH
GQA_W = GQA_HEADS * GQA_DH
GQA_KV_W = GQA_KV_HEADS * GQA_DH

OFF_RET = 0
OFF_NA = 4 * RET_W
OFF_GQA = OFF_NA + 4 * NA_W
OFF_GATES = OFF_GQA + 2 * GQA_W + 2 * GQA_KV_W
D_IN = OFF_GATES + 3 * D_MODEL

LANES = 128
VMEM_LIMIT = 56 * 1024 * 1024

BF16 = jnp.bfloat16
F32 = jnp.float32


def _dot(a, b):
    return jnp.dot(a, b, preferred_element_type=F32)


def _dot_nt(a, b):
    return lax.dot_general(a, b, (((1,), (1,)), ((), ())), preferred_element_type=F32)


def _dot_tn(a, b):
    return lax.dot_general(a, b, (((0,), (0,)), ((), ())), preferred_element_type=F32)


def _silu(z):
    return z / (1.0 + jnp.exp(-z))


def _sigmoid(z):
    return 1.0 / (1.0 + jnp.exp(-z))


def _params(sem):
    return pltpu.CompilerParams(dimension_semantics=sem, vmem_limit_bytes=VMEM_LIMIT)


def _proj_ret_kernel(x_ref, w_ref, cos_ref, sin_ref, q_ref, k_ref, v_ref, z_ref):
    xb = x_ref[...].astype(BF16)
    cos = cos_ref[...]
    sin = sin_ref[...]
    scale = RET_DK ** -0.5
    yq = _dot(xb, w_ref[:, 0:RET_W])
    yk = _dot(xb, w_ref[:, RET_W:2 * RET_W])
    for h in range(RET_HEADS):
        sl = slice(h * RET_DK, (h + 1) * RET_DK)
        t = yq[:, sl]
        q_ref[:, sl] = ((t * cos + pltpu.roll(t, RET_DK // 2, 1) * sin) * scale).astype(BF16)
        t = yk[:, sl]
        k_ref[:, sl] = (t * cos + pltpu.roll(t, RET_DK // 2, 1) * sin).astype(BF16)
    v_ref[...] = _dot(xb, w_ref[:, 2 * RET_W:3 * RET_W]).astype(BF16)
    z_ref[...] = _silu(_dot(xb, w_ref[:, 3 * RET_W:4 * RET_W]))


def _proj_na_kernel(x_ref, w_ref, q_ref, k_ref, v_ref, z_ref):
    xb = x_ref[...].astype(BF16)
    q_ref[...] = (_dot(xb, w_ref[:, 0:NA_W]) * (NA_DH ** -0.5)).astype(BF16)
    k_ref[...] = _dot(xb, w_ref[:, NA_W:2 * NA_W]).astype(BF16)
    v_ref[...] = _dot(xb, w_ref[:, 2 * NA_W:3 * NA_W]).astype(BF16)
    z_ref[...] = _silu(_dot(xb, w_ref[:, 3 * NA_W:4 * NA_W]))


def _group_mean_sq(y, g_ref, width):
    sq = y * y
    hi = sq.astype(BF16)
    lo = (sq - hi.astype(F32)).astype(BF16)
    g = g_ref[0:width, 0:width]
    return (_dot(hi, g) + _dot(lo, g)) * (1.0 / GQA_DH)


def _rope64(t, cos, sin, first_half):
    swapped = jnp.where(first_half, pltpu.roll(t, LANES - GQA_DH // 2, 1), pltpu.roll(t, GQA_DH // 2, 1))
    return t * cos + swapped * sin


def _proj_gqa_kernel(x_ref, w_ref, cos_ref, sin_ref, g_ref, qn_ref, kn_ref,
                     q_ref, k_ref, v_ref, z_ref):
    xb = x_ref[...].astype(BF16)
    cos = cos_ref[...]
    sin = sin_ref[...]
    lane = lax.broadcasted_iota(jnp.int32, cos.shape, 1)
    first_half = (lane % GQA_DH) < (GQA_DH // 2)
    yq = _dot(xb, w_ref[:, 0:GQA_W])
    qn = yq * lax.rsqrt(_group_mean_sq(yq, g_ref, GQA_W) + RMS_EPS) * qn_ref[...]
    for j in range(GQA_W // LANES):
        sl = slice(j * LANES, (j + 1) * LANES)
        q_ref[:, sl] = (_rope64(qn[:, sl], cos, sin, first_half) * (LOG2E * GQA_DH ** -0.5)).astype(BF16)
    yk = _dot(xb, w_ref[:, GQA_W:GQA_W + GQA_KV_W])
    kn = yk * lax.rsqrt(_group_mean_sq(yk, g_ref, GQA_KV_W) + RMS_EPS) * kn_ref[...]
    k_ref[...] = _rope64(kn, cos, sin, first_half).astype(BF16)
    v_ref[...] = _dot(xb, w_ref[:, GQA_W + GQA_KV_W:GQA_W + 2 * GQA_KV_W]).astype(BF16)
    z_ref[...] = _silu(_dot(xb, w_ref[:, GQA_W + 2 * GQA_KV_W:2 * GQA_W + 2 * GQA_KV_W]))


def _row_spec(tm, width):
    return pl.BlockSpec((tm, width), lambda i: (i, 0))


def _const_spec(shape):
    return pl.BlockSpec(shape, lambda i: (0,) * len(shape))


def _table_spec(tm, seq):
    nblk = seq // tm
    return pl.BlockSpec((tm, LANES), lambda i: (i % nblk, 0))


def _proj_ret(x2, w, cos, sin, seq, tm):
    m = x2.shape[0]
    out = [jax.ShapeDtypeStruct((m, RET_W), BF16)] * 3 + [jax.ShapeDtypeStruct((m, RET_W), F32)]
    return pl.pallas_call(
        _proj_ret_kernel, out_shape=out, grid=(m // tm,),
        in_specs=[_row_spec(tm, D_MODEL), _const_spec(w.shape), _table_spec(tm, seq), _table_spec(tm, seq)],
        out_specs=[_row_spec(tm, RET_W)] * 4,
        compiler_params=_params(("parallel",)), name="proj_ret")(x2, w, cos, sin)


def _proj_na(x2, w, tm):
    m = x2.shape[0]
    out = [jax.ShapeDtypeStruct((m, NA_W), BF16)] * 3 + [jax.ShapeDtypeStruct((m, NA_W), F32)]
    return pl.pallas_call(
        _proj_na_kernel, out_shape=out, grid=(m // tm,),
        in_specs=[_row_spec(tm, D_MODEL), _const_spec(w.shape)],
        out_specs=[_row_spec(tm, NA_W)] * 4,
        compiler_params=_params(("parallel",)), name="proj_na")(x2, w)


def _proj_gqa(x2, w, cos, sin, gmat, qn, kn, seq, tm):
    m = x2.shape[0]
    out = [jax.ShapeDtypeStruct((m, GQA_W), BF16), jax.ShapeDtypeStruct((m, GQA_KV_W), BF16),
           jax.ShapeDtypeStruct((m, GQA_KV_W), BF16), jax.ShapeDtypeStruct((m, GQA_W), F32)]
    return pl.pallas_call(
        _proj_gqa_kernel, out_shape=out, grid=(m // tm,),
        in_specs=[_row_spec(tm, D_MODEL), _const_spec(w.shape), _table_spec(tm, seq), _table_spec(tm, seq),
                  _const_spec(gmat.shape), _const_spec(qn.shape), _const_spec(kn.shape)],
        out_specs=[_row_spec(tm, GQA_W), _row_spec(tm, GQA_KV_W), _row_spec(tm, GQA_KV_W), _row_spec(tm, GQA_W)],
        compiler_params=_params(("parallel",)), name="proj_gqa")(x2, w, cos, sin, gmat, qn, kn)


def _retention_kernel(dec_ref, qf_ref, kf_ref, vf_ref, qb_ref, kb_ref, vb_ref,
                      of_ref, ob_ref, sf_ref, sb_ref, *, chunks):
    c = RET_CHUNK

    @pl.when(pl.program_id(2) == 0)
    def _():
        sf_ref[...] = jnp.zeros_like(sf_ref)
        sb_ref[...] = jnp.zeros_like(sb_ref)

    lf = dec_ref[0:1, :]
    lb = dec_ref[1:2, :]
    row = lax.broadcasted_iota(jnp.int32, (c, c), 0).astype(F32)
    col = lax.broadcasted_iota(jnp.int32, (c, c), 1).astype(F32)
    diff = row - col
    d_intra = jnp.where(diff >= 0, jnp.exp(jnp.maximum(diff, 0.0) * lf), jnp.exp(jnp.maximum(-diff, 0.0) * lb))
    w_kf = jnp.exp((c - 1.0 - row) * lf)
    w_qf = jnp.exp((row + 1.0) * lf)
    w_kb = jnp.exp(row * lb)
    w_qb = jnp.exp((c - row) * lb)
    chunk_f = jnp.exp(c * lf)
    chunk_b = jnp.exp(c * lb)

    s_f = sf_ref[...]
    for i in range(chunks):
        rows = slice(i * c, (i + 1) * c)
        q = qf_ref[rows, :]
        k = kf_ref[rows, :]
        v = vf_ref[rows, :]
        s = (_dot_nt(q, k) * d_intra).astype(BF16)
        o = _dot(s, v)
        qw = (q.astype(F32) * w_qf).astype(BF16)
        o = o + _dot(qw, s_f.astype(BF16))
        of_ref[rows, :] = o
        kw = (k.astype(F32) * w_kf).astype(BF16)
        s_f = chunk_f * s_f + _dot_tn(kw, v)
    sf_ref[...] = s_f

    s_b = sb_ref[...]
    for i in reversed(range(chunks)):
        rows = slice(i * c, (i + 1) * c)
        q = qb_ref[rows, :]
        k = kb_ref[rows, :]
        v = vb_ref[rows, :]
        qw = (q.astype(F32) * w_qb).astype(BF16)
        ob_ref[rows, :] = _dot(qw, s_b.astype(BF16))
        kw = (k.astype(F32) * w_kb).astype(BF16)
        s_b = chunk_b * s_b + _dot_tn(kw, v)
    sb_ref[...] = s_b


def _retention(rq, rk, rv, decay, batch, seq, chunks):
    m = rq.shape[0]
    rows = chunks * RET_CHUNK
    ng = seq // rows
    fwd = pl.BlockSpec((rows, RET_DK), lambda b, h, g: (b * ng + g, h))
    bwd = pl.BlockSpec((rows, RET_DK), lambda b, h, g: (b * ng + ng - 1 - g, h))
    dec = pl.BlockSpec((None, 8, LANES), lambda b, h, g: (h, 0, 0))
    out = [jax.ShapeDtypeStruct((m, RET_W), F32)] * 2
    return pl.pallas_call(
        functools.partial(_retention_kernel, chunks=chunks), out_shape=out,
        grid=(batch, RET_HEADS, ng),
        in_specs=[dec, fwd, fwd, fwd, bwd, bwd, bwd],
        out_specs=[fwd, bwd],
        scratch_shapes=[pltpu.VMEM((RET_DK, RET_DK), F32), pltpu.VMEM((RET_DK, RET_DK), F32)],
        compiler_params=_params(("parallel", "parallel", "arbitrary")),
        name="retention")(decay, rq, rk, rv, rq, rk, rv)


NA_QROWS = 4
NA_KROWS = 3 * NA_QROWS
NA_TQ = NA_QROWS * GRID_W
NA_TK = NA_KROWS * GRID_W
NA_PAIRS = 4


def _na_kernel(q_ref, k0_ref, k1_ref, k2_ref, v0_ref, v1_ref, v2_ref, bias_ref, z_ref, o_ref):
    lane = lax.broadcasted_iota(jnp.int32, (NA_TK, LANES), 1)
    zero = jnp.zeros((NA_TK, LANES), BF16)
    for pair in range(q_ref.shape[1] // LANES):
        cs = slice(pair * LANES, (pair + 1) * LANES)
        q = q_ref[:, cs]
        k = jnp.concatenate([k0_ref[:, cs], k1_ref[:, cs], k2_ref[:, cs]], axis=0)
        v = jnp.concatenate([v0_ref[:, cs], v1_ref[:, cs], v2_ref[:, cs]], axis=0)
        acc = None
        for half in range(2):
            own = (lane < NA_DH) if half == 0 else (lane >= NA_DH)
            s = _dot_nt(q, jnp.where(own, k, zero)) + bias_ref[2 * pair + half]
            mx = jnp.max(s, axis=-1, keepdims=True)
            p = jnp.exp(s - mx)
            den = jnp.sum(p, axis=-1, keepdims=True)
            o = _dot(p.astype(BF16), jnp.where(own, v, zero)) / den
            acc = o if acc is None else acc + o
        o_ref[:, cs] = (acc * z_ref[:, cs]).astype(o_ref.dtype)


def _na_bias_tables(rpb, rows):
    nblk = rows // NA_QROWS
    nrow, ncol = 2 * NA_KH - 1, 2 * NA_KW - 1
    col = np.arange(GRID_W)
    c_start = np.clip(col - NA_KW // 2, 0, GRID_W - NA_KW)
    col_ok = (col[None, :] >= c_start[:, None]) & (col[None, :] < c_start[:, None] + NA_KW)
    cidx = np.clip(col[None, :] - col[:, None] + NA_KW - 1, 0, ncol - 1)
    col_sel = (cidx[:, :, None] == np.arange(ncol)).astype(np.float32)
    row_sel = np.zeros((3, NA_QROWS, NA_KROWS, nrow), np.float32)
    row_ok = np.zeros((3, NA_QROWS, NA_KROWS), bool)
    for cls, blk in enumerate((0, 1, nblk - 1)):
        first_kblk = min(max(blk - 1, 0), nblk - 3)
        qr = blk * NA_QROWS + np.arange(NA_QROWS)
        kr = first_kblk * NA_QROWS + np.arange(NA_KROWS)
        r_start = np.clip(qr - NA_KH // 2, 0, rows - NA_KH)
        ok = (kr[None, :] >= r_start[:, None]) & (kr[None, :] < r_start[:, None] + NA_KH)
        ridx = kr[None, :] - qr[:, None] + NA_KH - 1
        row_ok[cls] = ok
        row_sel[cls] = ((ridx[:, :, None] == np.arange(nrow)) & ok[:, :, None])
    by_col = jnp.einsum('har,qkr->haqk', rpb.astype(F32), col_sel, precision=lax.Precision.HIGHEST)
    vals = jnp.einsum('cija,haqk->chiqjk', row_sel, by_col, precision=lax.Precision.HIGHEST)
    valid = row_ok[:, None, :, None, :, None] & col_ok[None, None, None, :, None, :]
    table = jnp.where(valid, vals, -jnp.inf)
    return table.reshape(3, NA_HEADS, NA_TQ, NA_TK)


def _na_attention(nq, nk, nv, nz, bias, batch, seq):
    m = nq.shape[0]
    rows = seq // GRID_W
    nblk = rows // NA_QROWS
    groups = NA_HEADS // (2 * NA_PAIRS)
    width = NA_PAIRS * LANES

    def kblk(j):
        return lambda b, hp, i: (b * nblk + jnp.clip(i - 1, 0, nblk - 3) + j, hp)

    def bias_idx(b, hp, i):
        cls = jnp.where(i == 0, 0, jnp.where(i == nblk - 1, 2, 1))
        return (cls, hp, 0, 0)

    qspec = pl.BlockSpec((NA_TQ, width), lambda b, hp, i: (b * nblk + i, hp))
    kspecs = [pl.BlockSpec((NA_TQ, width), kblk(j)) for j in range(3)]
    return pl.pallas_call(
        _na_kernel, out_shape=jax.ShapeDtypeStruct((m, NA_W), BF16),
        grid=(batch, groups, nblk),
        in_specs=[qspec] + kspecs + kspecs + [pl.BlockSpec((None, 2 * NA_PAIRS, NA_TQ, NA_TK), bias_idx), qspec],
        out_specs=qspec,
        compiler_params=_params(("parallel", "parallel", "arbitrary")),
        name="na_attention")(nq, nk, nk, nk, nv, nv, nv, bias, nz)


GQA_GROUP = GQA_HEADS // GQA_KV_HEADS
LOG2E = math.log2(math.e)
GQA_VROWS = 80


def _gqa_kernel(q_ref, k_ref, vt_ref, z_ref, o_ref,
                qs_ref, s_ref, mb_ref, m_ref, acc_ref, *, tk, unroll):
    tq = q_ref.shape[0]
    nk = k_ref.shape[0] // tk
    assert nk % 2 == 0 and unroll % 2 == 0 and nk >= 1 + unroll

    def chunk_start(c):
        return c * tk if isinstance(c, int) else pl.multiple_of(c * tk, tk)
    for g in range(GQA_GROUP):
        qs_ref[g * tq:(g + 1) * tq, :] = q_ref[:, g * GQA_DH:(g + 1) * GQA_DH]
    m_ref[...] = jnp.full(m_ref.shape, -jnp.inf, F32)
    acc_ref[...] = jnp.zeros_like(acc_ref)

    def scores(c, par):
        s = _dot_nt(k_ref[pl.ds(chunk_start(c), tk), :], qs_ref[...])
        s_ref[par] = s
        mb_ref[par] = jnp.max(s, axis=0, keepdims=True)

    def softmax_values(c, par):
        m_old = m_ref[...]
        m_new = jnp.maximum(m_old, mb_ref[par])
        m_ref[...] = m_new
        p = jnp.exp2(s_ref[par] - m_new).astype(BF16)
        pv = _dot(vt_ref[:, pl.ds(chunk_start(c), tk)], p)
        acc_ref[...] = jnp.exp2(m_old - m_new) * acc_ref[...] + pv

    def tick(i, par):
        scores(i, par)
        softmax_values(i - 1, 1 - par)

    scores(0, 0)
    first = 1 + (nk - 1) % unroll
    for i in range(1, first):
        tick(i, i % 2)

    def body(j, carry):
        i = first + unroll * j
        for u in range(unroll):
            tick(i + u, (first + u) % 2)
        return carry

    lax.fori_loop(0, (nk - first) // unroll, body, 0)
    softmax_values(nk - 1, 1)

    inv = 1.0 / acc_ref[GQA_DH:GQA_DH + 1, :]
    out = acc_ref[...] * inv
    out_t = jnp.concatenate([out, jnp.zeros((LANES - GQA_VROWS, out.shape[1]), F32)], axis=0).T
    outs = [out_t[g * tq:(g + 1) * tq, 0:GQA_DH] for g in range(GQA_GROUP)]
    o_ref[...] = (jnp.concatenate(outs, axis=-1) * z_ref[...]).astype(o_ref.dtype)


def _gqa_attention(cq, k2, vt, cz, batch, seq, tq, tk, unroll):
    m = cq.shape[0]
    nq = seq // tq
    width = GQA_GROUP * GQA_DH
    cols = GQA_GROUP * tq
    qspec = pl.BlockSpec((tq, width), lambda b, n, i: (b * nq + i, n))
    return pl.pallas_call(
        functools.partial(_gqa_kernel, tk=tk, unroll=unroll), out_shape=jax.ShapeDtypeStruct((m, GQA_W), BF16),
        grid=(batch, GQA_KV_HEADS, nq),
        in_specs=[qspec,
                  pl.BlockSpec((None, seq, GQA_DH), lambda b, n, i: (b * GQA_KV_HEADS + n, 0, 0)),
                  pl.BlockSpec((None, GQA_VROWS, seq), lambda b, n, i: (b * GQA_KV_HEADS + n, 0, 0)),
                  qspec],
        out_specs=qspec,
        scratch_shapes=[pltpu.VMEM((cols, GQA_DH), BF16),
                        pltpu.VMEM((2, tk, cols), F32),
                        pltpu.VMEM((2, 1, cols), F32),
                        pltpu.VMEM((1, cols), F32),
                        pltpu.VMEM((GQA_VROWS, cols), F32)],
        compiler_params=_params(("parallel", "parallel", "arbitrary")),
        name="gqa_attention")(cq, k2, vt, cz)


def _merge_kernel(x_ref, of_ref, ob_ref, rz_ref, gn_ref, onb_ref, ogq_ref,
                  wg_ref, wa_ref, wb_ref, wc_ref, wo_ref, lg_ref, lb_ref, o_ref, *, alpha):
    x = x_ref[...]
    xb = x.astype(BF16)
    ra = of_ref[...] + ob_ref[...]
    parts = []
    for h in range(RET_HEADS):
        t = ra[:, h * RET_DK:(h + 1) * RET_DK]
        mu = jnp.mean(t, axis=-1, keepdims=True)
        d = t - mu
        var = jnp.mean(d * d, axis=-1, keepdims=True)
        parts.append(d * lax.rsqrt(var + LN_EPS))
    o_a = (jnp.concatenate(parts, axis=-1) * gn_ref[...] * rz_ref[...]).astype(BF16)
    y = _sigmoid(_dot(xb, wg_ref[:, 0:D_MODEL])) * _dot(o_a, wa_ref[...])
    y = y + _sigmoid(_dot(xb, wg_ref[:, D_MODEL:2 * D_MODEL])) * _dot(onb_ref[...], wb_ref[...])
    y = y + _sigmoid(_dot(xb, wg_ref[:, 2 * D_MODEL:3 * D_MODEL])) * _dot(ogq_ref[...], wc_ref[...])
    z = alpha * x + _dot(y.astype(BF16), wo_ref[...])
    mu = jnp.mean(z, axis=-1, keepdims=True)
    d = z - mu
    var = jnp.mean(d * d, axis=-1, keepdims=True)
    o_ref[...] = d * lax.rsqrt(var + LN_EPS) * lg_ref[...] + lb_ref[...]


def _merge(x2, o_f, o_b, rz, gn, o_nb, o_gq, wg, wa, wb, wc, wo, lg, lb, alpha, tm):
    m = x2.shape[0]
    return pl.pallas_call(
        functools.partial(_merge_kernel, alpha=alpha), out_shape=jax.ShapeDtypeStruct((m, D_MODEL), F32),
        grid=(m // tm,),
        in_specs=[_row_spec(tm, D_MODEL), _row_spec(tm, RET_W), _row_spec(tm, RET_W), _row_spec(tm, RET_W),
                  _const_spec(gn.shape), _row_spec(tm, NA_W), _row_spec(tm, GQA_W),
                  _const_spec(wg.shape), _const_spec(wa.shape), _const_spec(wb.shape), _const_spec(wc.shape),
                  _const_spec(wo.shape), _const_spec(lg.shape), _const_spec(lb.shape)],
        out_specs=_row_spec(tm, D_MODEL),
        compiler_params=_params(("parallel",)), name="merge")(
            x2, o_f, o_b, rz, gn, o_nb, o_gq, wg, wa, wb, wc, wo, lg, lb)


PROJ_TM = 512
MERGE_TM = 256
RET_CHUNKS_PER_STEP = 4
GQA_TQ = 256
GQA_TK = 512
GQA_UNROLL = 6


def _rope_tables(seq):
    t = jnp.arange(seq)
    tf = t.astype(F32)
    row = (t // GRID_W).astype(F32)
    col = (t % GRID_W).astype(F32)
    inv_ret = 1.0 / (ROPE_THETA ** jnp.linspace(0.0, 1.0, RET_DK // 2, dtype=F32))
    ang_ret = tf[:, None] * inv_ret[None, :]
    n_ax = GQA_DH // 4
    inv_ax = ROPE_THETA ** (-jnp.arange(n_ax, dtype=F32) / n_ax)
    ang_ax = jnp.concatenate([row[:, None] * inv_ax[None, :], col[:, None] * inv_ax[None, :]], -1)

    def full(ang, reps):
        c, s = jnp.cos(ang), jnp.sin(ang)
        return jnp.tile(jnp.concatenate([c, c], -1), (1, reps)), jnp.tile(jnp.concatenate([-s, s], -1), (1, reps))

    return full(ang_ret, 1), full(ang_ax, LANES // GQA_DH)


def kernel(x, w_in, ret_theta_fwd, ret_theta_bwd, ret_gn_gain, na_rpb, gqa_q_norm, gqa_k_norm,
           w_branch_a, w_branch_b, w_branch_c, w_out, ln_gain, ln_bias):
    batch, seq, d_model = x.shape
    depth = w_in.shape[0]
    assert d_model == D_MODEL and w_in.shape[-1] == D_IN and seq % (GRID_W * NA_QROWS) == 0
    alpha = (2.0 * depth) ** 0.25
    m = batch * seq
    (ret_cos, ret_sin), (ax_cos, ax_sin) = _rope_tables(seq)
    gidx = np.arange(GQA_W) // GQA_DH
    gmat = jnp.asarray(gidx[:, None] == gidx[None, :], BF16)
    ones_col = jnp.concatenate([jnp.ones((m, 1), BF16), jnp.zeros((m, GQA_VROWS - GQA_DH - 1), BF16)], -1)

    x2 = x.reshape(m, D_MODEL)
    for l in range(depth):
        w = w_in[l].astype(BF16)
        rq, rk, rv, rz = _proj_ret(x2, w[:, OFF_RET:OFF_NA], ret_cos, ret_sin, seq, PROJ_TM)
        nq, nk, nv, nz = _proj_na(x2, w[:, OFF_NA:OFF_GQA], PROJ_TM)
        qn = jnp.tile(gqa_q_norm[l], GQA_HEADS)[None, :]
        kn = jnp.tile(gqa_k_norm[l], GQA_KV_HEADS)[None, :]
        cq, ck, cv, cz = _proj_gqa(x2, w[:, OFF_GQA:OFF_GATES], ax_cos, ax_sin, gmat, qn, kn, seq, PROJ_TM)

        decay = jnp.stack([jax.nn.log_sigmoid(ret_theta_fwd[l].astype(F32)),
                           jax.nn.log_sigmoid(ret_theta_bwd[l].astype(F32))], axis=1)
        decay = jnp.broadcast_to(jnp.pad(decay, ((0, 0), (0, 6)))[:, :, None], (RET_HEADS, 8, LANES))
        o_f, o_b = _retention(rq, rk, rv, decay, batch, seq, RET_CHUNKS_PER_STEP)

        bias = _na_bias_tables(na_rpb[l], seq // GRID_W)
        o_nb = _na_attention(nq, nk, nv, nz, bias, batch, seq)

        k2 = ck.reshape(batch, seq, GQA_KV_HEADS, GQA_DH).transpose(0, 2, 1, 3).reshape(
            batch * GQA_KV_HEADS, seq, GQA_DH)
        va = jnp.concatenate([cv.reshape(m, GQA_KV_HEADS, GQA_DH),
                              jnp.broadcast_to(ones_col[:, None, :], (m, GQA_KV_HEADS, GQA_VROWS - GQA_DH))], -1)
        vt = va.reshape(batch, seq, GQA_KV_HEADS, GQA_VROWS).transpose(0, 2, 3, 1).reshape(
            batch * GQA_KV_HEADS, GQA_VROWS, seq)
        o_gq = _gqa_attention(cq, k2, vt, cz, batch, seq, GQA_TQ, GQA_TK, GQA_UNROLL)

        x2 = _merge(x2, o_f, o_b, rz, ret_gn_gain[l][None, :], o_nb, o_gq,
                    w[:, OFF_GATES:D_IN], w_branch_a[l].astype(BF16), w_branch_b[l].astype(BF16),
                    w_branch_c[l].astype(BF16), w_out[l].astype(BF16),
                    ln_gain[l][None, :], ln_bias[l][None, :], alpha, MERGE_TM)
    return x2.reshape(batch, seq, D_MODEL)
```

```python
import functools
import math

import numpy as np
import jax
import jax.numpy as jnp
from jax import lax
from jax.experimental import pallas as pl
from jax.experimental.pallas import tpu as pltpu

D_MODEL = 1024
GRID_W = 64
RET_HEADS = 4
RET_DK = 128
RET_CHUNK = 128
NA_HEADS = 8
NA_DH = 64
NA_KH = 8
NA_KW = 16
GQA_HEADS = 8
GQA_KV_HEADS = 2
GQA_DH = 64
ROPE_THETA = 10000.0
RMS_EPS = 1e-6
LN_EPS = 1e-5

RET_W = RET_HEADS * RET_DK
NA_W = NA_HEADS * NA_DH
GQA_W = GQA_HEADS * GQA_DH
GQA_KV_W = GQA_KV_HEADS * GQA_DH

OFF_RET = 0
OFF_NA = 4 * RET_W
OFF_GQA = OFF_NA + 4 * NA_W
OFF_GATES = OFF_GQA + 2 * GQA_W + 2 * GQA_KV_W
D_IN = OFF_GATES + 3 * D_MODEL

LOG2E = math.log2(math.e)
LANES = 128
VMEM_LIMIT = 56 * 1024 * 1024

BF16 = jnp.bfloat16
F32 = jnp.float32


def _dot(a, b):
    return jnp.dot(a, b, preferred_element_type=F32)


def _dot_nt(a, b):
    return lax.dot_general(a, b, (((1,), (1,)), ((), ())), preferred_element_type=F32)


def _dot_tn(a, b):
    return lax.dot_general(a, b, (((0,), (0,)), ((), ())), preferred_element_type=F32)


def _silu(z):
    return z / (1.0 + jnp.exp(-z))


def _sigmoid(z):
    return 1.0 / (1.0 + jnp.exp(-z))


def _params(sem):
    return pltpu.CompilerParams(dimension_semantics=sem, vmem_limit_bytes=VMEM_LIMIT)


def _proj_ret_kernel(x_ref, w_ref, cos_ref, sin_ref, q_ref, k_ref, v_ref, z_ref):
    xb = x_ref[...].astype(BF16)
    cos = cos_ref[...]
    sin = sin_ref[...]
    scale = RET_DK ** -0.5
    yq = _dot(xb, w_ref[:, 0:RET_W])
    yk = _dot(xb, w_ref[:, RET_W:2 * RET_W])
    for h in range(RET_HEADS):
        sl = slice(h * RET_DK, (h + 1) * RET_DK)
        t = yq[:, sl]
        q_ref[:, sl] = ((t * cos + pltpu.roll(t, RET_DK // 2, 1) * sin) * scale).astype(BF16)
        t = yk[:, sl]
        k_ref[:, sl] = (t * cos + pltpu.roll(t, RET_DK // 2, 1) * sin).astype(BF16)
    v_ref[...] = _dot(xb, w_ref[:, 2 * RET_W:3 * RET_W]).astype(BF16)
    z_ref[...] = _silu(_dot(xb, w_ref[:, 3 * RET_W:4 * RET_W]))


def _proj_na_kernel(x_ref, w_ref, q_ref, k_ref, v_ref, z_ref):
    xb = x_ref[...].astype(BF16)
    q_ref[...] = (_dot(xb, w_ref[:, 0:NA_W]) * (LOG2E * NA_DH ** -0.5)).astype(BF16)
    k_ref[...] = _dot(xb, w_ref[:, NA_W:2 * NA_W]).astype(BF16)
    v_ref[...] = _dot(xb, w_ref[:, 2 * NA_W:3 * NA_W]).astype(BF16)
    z_ref[...] = _silu(_dot(xb, w_ref[:, 3 * NA_W:4 * NA_W]))


def _group_mean_sq(y, g_ref, width):
    sq = y * y
    hi = sq.astype(BF16)
    lo = (sq - hi.astype(F32)).astype(BF16)
    g = g_ref[0:width, 0:width]
    return (_dot(hi, g) + _dot(lo, g)) * (1.0 / GQA_DH)


def _rope64(t, cos, sin, first_half):
    swapped = jnp.where(first_half, pltpu.roll(t, LANES - GQA_DH // 2, 1), pltpu.roll(t, GQA_DH // 2, 1))
    return t * cos + swapped * sin


def _proj_gqa_kernel(x_ref, w_ref, cos_ref, sin_ref, g_ref, qn_ref, kn_ref,
                     q_ref, k_ref, v_ref, z_ref):
    xb = x_ref[...].astype(BF16)
    cos = cos_ref[...]
    sin = sin_ref[...]
    lane = lax.broadcasted_iota(jnp.int32, cos.shape, 1)
    first_half = (lane % GQA_DH) < (GQA_DH // 2)
    yq = _dot(xb, w_ref[:, 0:GQA_W])
    qn = yq * lax.rsqrt(_group_mean_sq(yq, g_ref, GQA_W) + RMS_EPS) * qn_ref[...]
    for j in range(GQA_W // LANES):
        sl = slice(j * LANES, (j + 1) * LANES)
        q_ref[:, sl] = (_rope64(qn[:, sl], cos, sin, first_half) * (LOG2E * GQA_DH ** -0.5)).astype(BF16)
    yk = _dot(xb, w_ref[:, GQA_W:GQA_W + GQA_KV_W])
    kn = yk * lax.rsqrt(_group_mean_sq(yk, g_ref, GQA_KV_W) + RMS_EPS) * kn_ref[...]
    k_ref[...] = _rope64(kn, cos, sin, first_half).astype(BF16)
    v_ref[...] = _dot(xb, w_ref[:, GQA_W + GQA_KV_W:GQA_W + 2 * GQA_KV_W]).astype(BF16)
    z_ref[...] = _silu(_dot(xb, w_ref[:, GQA_W + 2 * GQA_KV_W:2 * GQA_W + 2 * GQA_KV_W]))


def _row_spec(tm, width):
    return pl.BlockSpec((tm, width), lambda i: (i, 0))


def _const_spec(shape):
    return pl.BlockSpec(shape, lambda i: (0,) * len(shape))


def _table_spec(tm, seq):
    nblk = seq // tm
    return pl.BlockSpec((tm, LANES), lambda i: (i % nblk, 0))


def _proj_ret(x2, w, cos, sin, seq, tm):
    m = x2.shape[0]
    out = [jax.ShapeDtypeStruct((m, RET_W), BF16)] * 3 + [jax.ShapeDtypeStruct((m, RET_W), F32)]
    return pl.pallas_call(
        _proj_ret_kernel, out_shape=out, grid=(m // tm,),
        in_specs=[_row_spec(tm, D_MODEL), _const_spec(w.shape), _table_spec(tm, seq), _table_spec(tm, seq)],
        out_specs=[_row_spec(tm, RET_W)] * 4,
        compiler_params=_params(("parallel",)), name="proj_ret")(x2, w, cos, sin)


def _proj_na(x2, w, tm):
    m = x2.shape[0]
    out = [jax.ShapeDtypeStruct((m, NA_W), BF16)] * 3 + [jax.ShapeDtypeStruct((m, NA_W), F32)]
    return pl.pallas_call(
        _proj_na_kernel, out_shape=out, grid=(m // tm,),
        in_specs=[_row_spec(tm, D_MODEL), _const_spec(w.shape)],
        out_specs=[_row_spec(tm, NA_W)] * 4,
        compiler_params=_params(("parallel",)), name="proj_na")(x2, w)


def _proj_gqa(x2, w, cos, sin, gmat, qn, kn, seq, tm):
    m = x2.shape[0]
    out = [jax.ShapeDtypeStruct((m, GQA_W), BF16), jax.ShapeDtypeStruct((m, GQA_KV_W), BF16),
           jax.ShapeDtypeStruct((m, GQA_KV_W), BF16), jax.ShapeDtypeStruct((m, GQA_W), F32)]
    return pl.pallas_call(
        _proj_gqa_kernel, out_shape=out, grid=(m // tm,),
        in_specs=[_row_spec(tm, D_MODEL), _const_spec(w.shape), _table_spec(tm, seq), _table_spec(tm, seq),
                  _const_spec(gmat.shape), _const_spec(qn.shape), _const_spec(kn.shape)],
        out_specs=[_row_spec(tm, GQA_W), _row_spec(tm, GQA_KV_W), _row_spec(tm, GQA_KV_W), _row_spec(tm, GQA_W)],
        compiler_params=_params(("parallel",)), name="proj_gqa")(x2, w, cos, sin, gmat, qn, kn)


def _retention_kernel(dec_ref, qf_ref, kf_ref, vf_ref, qb_ref, kb_ref, vb_ref,
                      of_ref, ob_ref, sf_ref, sb_ref, *, chunks):
    c = RET_CHUNK

    @pl.when(pl.program_id(2) == 0)
    def _():
        sf_ref[...] = jnp.zeros_like(sf_ref)
        sb_ref[...] = jnp.zeros_like(sb_ref)

    lf = dec_ref[0:1, :]
    lb = dec_ref[1:2, :]
    row = lax.broadcasted_iota(jnp.int32, (c, c), 0).astype(F32)
    col = lax.broadcasted_iota(jnp.int32, (c, c), 1).astype(F32)
    diff = row - col
    d_intra = jnp.where(diff >= 0, jnp.exp(jnp.maximum(diff, 0.0) * lf), jnp.exp(jnp.maximum(-diff, 0.0) * lb))
    w_kf = jnp.exp((c - 1.0 - row) * lf)
    w_qf = jnp.exp((row + 1.0) * lf)
    w_kb = jnp.exp(row * lb)
    w_qb = jnp.exp((c - row) * lb)
    chunk_f = jnp.exp(c * lf)
    chunk_b = jnp.exp(c * lb)

    blocks = [slice(i * c, (i + 1) * c) for i in range(chunks)]

    kv_f = [_dot_tn((kf_ref[r, :].astype(F32) * w_kf).astype(BF16), vf_ref[r, :]) for r in blocks]
    kv_b = [_dot_tn((kb_ref[r, :].astype(F32) * w_kb).astype(BF16), vb_ref[r, :]) for r in blocks]
    intra = [_dot((_dot_nt(qf_ref[r, :], kf_ref[r, :]) * d_intra).astype(BF16), vf_ref[r, :]) for r in blocks]

    s_f = sf_ref[...]
    states_f = []
    for i in range(chunks):
        states_f.append(s_f)
        s_f = chunk_f * s_f + kv_f[i]
    sf_ref[...] = s_f
    s_b = sb_ref[...]
    states_b = [None] * chunks
    for i in reversed(range(chunks)):
        states_b[i] = s_b
        s_b = chunk_b * s_b + kv_b[i]
    sb_ref[...] = s_b

    for i, r in enumerate(blocks):
        qw = (qf_ref[r, :].astype(F32) * w_qf).astype(BF16)
        of_ref[r, :] = intra[i] + _dot(qw, states_f[i].astype(BF16))
        qw = (qb_ref[r, :].astype(F32) * w_qb).astype(BF16)
        ob_ref[r, :] = _dot(qw, states_b[i].astype(BF16))


def _retention(rq, rk, rv, decay, batch, seq, chunks):
    m = rq.shape[0]
    rows = chunks * RET_CHUNK
    ng = seq // rows
    fwd = pl.BlockSpec((rows, RET_DK), lambda b, h, g: (b * ng + g, h))
    bwd = pl.BlockSpec((rows, RET_DK), lambda b, h, g: (b * ng + ng - 1 - g, h))
    dec = pl.BlockSpec((None, 8, LANES), lambda b, h, g: (h, 0, 0))
    out = [jax.ShapeDtypeStruct((m, RET_W), F32)] * 2
    return pl.pallas_call(
        functools.partial(_retention_kernel, chunks=chunks), out_shape=out,
        grid=(batch, RET_HEADS, ng),
        in_specs=[dec, fwd, fwd, fwd, bwd, bwd, bwd],
        out_specs=[fwd, bwd],
        scratch_shapes=[pltpu.VMEM((RET_DK, RET_DK), F32), pltpu.VMEM((RET_DK, RET_DK), F32)],
        compiler_params=_params(("parallel", "parallel", "arbitrary")),
        name="retention")(decay, rq, rk, rv, rq, rk, rv)


NA_QROWS = 4
NA_KROWS = 3 * NA_QROWS
NA_TQ = NA_QROWS * GRID_W
NA_TK = NA_KROWS * GRID_W
NA_PAIRS = 4


def _na_kernel(q_ref, k0_ref, k1_ref, k2_ref, v0_ref, v1_ref, v2_ref, bias_ref, z_ref, o_ref,
               s_ref, p_ref, mx_ref, den_ref):
    lane = lax.broadcasted_iota(jnp.int32, (NA_TK, LANES), 1)
    zero = jnp.zeros((NA_TK, LANES), BF16)
    units = 2 * (q_ref.shape[1] // LANES)

    def masked(refs, u):
        cs = slice((u // 2) * LANES, (u // 2 + 1) * LANES)
        x = jnp.concatenate([r[:, cs] for r in refs], axis=0)
        own = (lane < NA_DH) if u % 2 == 0 else (lane >= NA_DH)
        return jnp.where(own, x, zero)

    def scores(u):
        cs = slice((u // 2) * LANES, (u // 2 + 1) * LANES)
        s = _dot_nt(q_ref[:, cs], masked((k0_ref, k1_ref, k2_ref), u)) + bias_ref[u]
        s_ref[u] = s
        mx_ref[u] = jnp.broadcast_to(jnp.max(s, axis=-1, keepdims=True), (NA_TQ, LANES))

    def softmax(u):
        mx = mx_ref[u]
        den = None
        for j in range(NA_TK // LANES):
            cs = slice(j * LANES, (j + 1) * LANES)
            p = jnp.exp2(s_ref[u, :, cs] - mx)
            p_ref[u, :, cs] = p.astype(BF16)
            den = p if den is None else den + p
        den_ref[u] = jnp.broadcast_to(jnp.sum(den, axis=-1, keepdims=True), (NA_TQ, LANES))

    outs = {}

    def values(u):
        o = _dot(p_ref[u], masked((v0_ref, v1_ref, v2_ref), u)) / den_ref[u]
        if u % 2 == 0:
            outs[u // 2] = o
        else:
            cs = slice((u // 2) * LANES, (u // 2 + 1) * LANES)
            o_ref[:, cs] = ((outs.pop(u // 2) + o) * z_ref[:, cs]).astype(o_ref.dtype)

    for t in range(units + 2):
        if t < units:
            scores(t)
        if 1 <= t <= units:
            softmax(t - 1)
        if t >= 2:
            values(t - 2)


def _na_bias_tables(rpb, rows):
    nblk = rows // NA_QROWS
    nrow, ncol = 2 * NA_KH - 1, 2 * NA_KW - 1
    col = np.arange(GRID_W)
    c_start = np.clip(col - NA_KW // 2, 0, GRID_W - NA_KW)
    col_ok = (col[None, :] >= c_start[:, None]) & (col[None, :] < c_start[:, None] + NA_KW)
    cidx = np.clip(col[None, :] - col[:, None] + NA_KW - 1, 0, ncol - 1)
    col_sel = (cidx[:, :, None] == np.arange(ncol)).astype(np.float32)
    row_sel = np.zeros((3, NA_QROWS, NA_KROWS, nrow), np.float32)
    row_ok = np.zeros((3, NA_QROWS, NA_KROWS), bool)
    for cls, blk in enumerate((0, 1, nblk - 1)):
        first_kblk = min(max(blk - 1, 0), nblk - 3)
        qr = blk * NA_QROWS + np.arange(NA_QROWS)
        kr = first_kblk * NA_QROWS + np.arange(NA_KROWS)
        r_start = np.clip(qr - NA_KH // 2, 0, rows - NA_KH)
        ok = (kr[None, :] >= r_start[:, None]) & (kr[None, :] < r_start[:, None] + NA_KH)
        ridx = kr[None, :] - qr[:, None] + NA_KH - 1
        row_ok[cls] = ok
        row_sel[cls] = ((ridx[:, :, None] == np.arange(nrow)) & ok[:, :, None])
    by_col = jnp.einsum('har,qkr->haqk', rpb.astype(F32), col_sel, precision=lax.Precision.HIGHEST)
    vals = jnp.einsum('cija,haqk->chiqjk', row_sel, by_col, precision=lax.Precision.HIGHEST)
    valid = row_ok[:, None, :, None, :, None] & col_ok[None, None, None, :, None, :]
    table = jnp.where(valid, vals * LOG2E, -jnp.inf)
    return table.reshape(3, NA_HEADS, NA_TQ, NA_TK)


def _na_attention(nq, nk, nv, nz, bias, batch, seq):
    m = nq.shape[0]
    rows = seq // GRID_W
    nblk = rows // NA_QROWS
    groups = NA_HEADS // (2 * NA_PAIRS)
    width = NA_PAIRS * LANES

    def kblk(j):
        return lambda b, hp, i: (b * nblk + jnp.clip(i - 1, 0, nblk - 3) + j, hp)

    def bias_idx(b, hp, i):
        cls = jnp.where(i == 0, 0, jnp.where(i == nblk - 1, 2, 1))
        return (cls, hp, 0, 0)

    qspec = pl.BlockSpec((NA_TQ, width), lambda b, hp, i: (b * nblk + i, hp))
    kspecs = [pl.BlockSpec((NA_TQ, width), kblk(j)) for j in range(3)]
    return pl.pallas_call(
        _na_kernel, out_shape=jax.ShapeDtypeStruct((m, NA_W), BF16),
        grid=(batch, groups, nblk),
        in_specs=[qspec] + kspecs + kspecs + [pl.BlockSpec((None, 2 * NA_PAIRS, NA_TQ, NA_TK), bias_idx), qspec],
        out_specs=qspec,
        scratch_shapes=[pltpu.VMEM((2 * NA_PAIRS, NA_TQ, NA_TK), F32),
                        pltpu.VMEM((2 * NA_PAIRS, NA_TQ, NA_TK), BF16),
                        pltpu.VMEM((2 * NA_PAIRS, NA_TQ, LANES), F32),
                        pltpu.VMEM((2 * NA_PAIRS, NA_TQ, LANES), F32)],
        compiler_params=_params(("parallel", "parallel", "arbitrary")),
        name="na_attention")(nq, nk, nk, nk, nv, nv, nv, bias, nz)


GQA_GROUP = GQA_HEADS // GQA_KV_HEADS
GQA_VROWS = 80


def _gqa_kernel(q_ref, k_ref, vt_ref, z_ref, o_ref,
                qs_ref, s_ref, mb_ref, m_ref, acc_ref, *, tk, unroll):
    tq = q_ref.shape[0]
    nk = k_ref.shape[0] // tk
    assert nk % 2 == 0 and unroll % 2 == 0 and nk >= 1 + unroll

    def chunk_start(c):
        return c * tk if isinstance(c, int) else pl.multiple_of(c * tk, tk)
    for g in range(GQA_GROUP):
        qs_ref[g * tq:(g + 1) * tq, :] = q_ref[:, g * GQA_DH:(g + 1) * GQA_DH]
    m_ref[...] = jnp.full(m_ref.shape, -jnp.inf, F32)
    acc_ref[...] = jnp.zeros_like(acc_ref)

    def scores(c, par):
        s = _dot_nt(k_ref[pl.ds(chunk_start(c), tk), :], qs_ref[...])
        s_ref[par] = s
        mb_ref[par] = jnp.max(s, axis=0, keepdims=True)

    def softmax_values(c, par):
        m_old = m_ref[...]
        m_new = jnp.maximum(m_old, mb_ref[par])
        m_ref[...] = m_new
        p = jnp.exp2(s_ref[par] - m_new).astype(BF16)
        pv = _dot(vt_ref[:, pl.ds(chunk_start(c), tk)], p)
        acc_ref[...] = jnp.exp2(m_old - m_new) * acc_ref[...] + pv

    def tick(i, par):
        scores(i, par)
        softmax_values(i - 1, 1 - par)

    scores(0, 0)
    first = 1 + (nk - 1) % unroll
    for i in range(1, first):
        tick(i, i % 2)

    def body(j, carry):
        i = first + unroll * j
        for u in range(unroll):
            tick(i + u, (first + u) % 2)
        return carry

    lax.fori_loop(0, (nk - first) // unroll, body, 0)
    softmax_values(nk - 1, 1)

    inv = 1.0 / acc_ref[GQA_DH:GQA_DH + 1, :]
    out = acc_ref[...] * inv
    out_t = jnp.concatenate([out, jnp.zeros((LANES - GQA_VROWS, out.shape[1]), F32)], axis=0).T
    outs = [out_t[g * tq:(g + 1) * tq, 0:GQA_DH] for g in range(GQA_GROUP)]
    o_ref[...] = (jnp.concatenate(outs, axis=-1) * z_ref[...]).astype(o_ref.dtype)


def _gqa_attention(cq, k2, vt, cz, batch, seq, tq, tk, unroll):
    m = cq.shape[0]
    nq = seq // tq
    width = GQA_GROUP * GQA_DH
    cols = GQA_GROUP * tq
    qspec = pl.BlockSpec((tq, width), lambda b, n, i: (b * nq + i, n))
    return pl.pallas_call(
        functools.partial(_gqa_kernel, tk=tk, unroll=unroll), out_shape=jax.ShapeDtypeStruct((m, GQA_W), BF16),
        grid=(batch, GQA_KV_HEADS, nq),
        in_specs=[qspec,
                  pl.BlockSpec((None, seq, GQA_DH), lambda b, n, i: (b * GQA_KV_HEADS + n, 0, 0)),
                  pl.BlockSpec((None, GQA_VROWS, seq), lambda b, n, i: (b * GQA_KV_HEADS + n, 0, 0)),
                  qspec],
        out_specs=qspec,
        scratch_shapes=[pltpu.VMEM((cols, GQA_DH), BF16),
                        pltpu.VMEM((2, tk, cols), F32),
                        pltpu.VMEM((2, 1, cols), F32),
                        pltpu.VMEM((1, cols), F32),
                        pltpu.VMEM((GQA_VROWS, cols), F32)],
        compiler_params=_params(("parallel", "parallel", "arbitrary")),
        name="gqa_attention")(cq, k2, vt, cz)


def _merge_kernel(x_ref, of_ref, ob_ref, rz_ref, gn_ref, onb_ref, ogq_ref,
                  wg_ref, wa_ref, wb_ref, wc_ref, wo_ref, lg_ref, lb_ref, o_ref, *, alpha):
    x = x_ref[...]
    xb = x.astype(BF16)
    ra = of_ref[...] + ob_ref[...]
    parts = []
    for h in range(RET_HEADS):
        t = ra[:, h * RET_DK:(h + 1) * RET_DK]
        mu = jnp.mean(t, axis=-1, keepdims=True)
        d = t - mu
        var = jnp.mean(d * d, axis=-1, keepdims=True)
        parts.append(d * lax.rsqrt(var + LN_EPS))
    o_a = (jnp.concatenate(parts, axis=-1) * gn_ref[...] * rz_ref[...]).astype(BF16)
    y = _sigmoid(_dot(xb, wg_ref[:, 0:D_MODEL])) * _dot(o_a, wa_ref[...])
    y = y + _sigmoid(_dot(xb, wg_ref[:, D_MODEL:2 * D_MODEL])) * _dot(onb_ref[...], wb_ref[...])
    y = y + _sigmoid(_dot(xb, wg_ref[:, 2 * D_MODEL:3 * D_MODEL])) * _dot(ogq_ref[...], wc_ref[...])
    z = alpha * x + _dot(y.astype(BF16), wo_ref[...])
    mu = jnp.mean(z, axis=-1, keepdims=True)
    d = z - mu
    var = jnp.mean(d * d, axis=-1, keepdims=True)
    o_ref[...] = d * lax.rsqrt(var + LN_EPS) * lg_ref[...] + lb_ref[...]


def _merge(x2, o_f, o_b, rz, gn, o_nb, o_gq, wg, wa, wb, wc, wo, lg, lb, alpha, tm):
    m = x2.shape[0]
    return pl.pallas_call(
        functools.partial(_merge_kernel, alpha=alpha), out_shape=jax.ShapeDtypeStruct((m, D_MODEL), F32),
        grid=(m // tm,),
        in_specs=[_row_spec(tm, D_MODEL), _row_spec(tm, RET_W), _row_spec(tm, RET_W), _row_spec(tm, RET_W),
                  _const_spec(gn.shape), _row_spec(tm, NA_W), _row_spec(tm, GQA_W),
                  _const_spec(wg.shape), _const_spec(wa.shape), _const_spec(wb.shape), _const_spec(wc.shape),
                  _const_spec(wo.shape), _const_spec(lg.shape), _const_spec(lb.shape)],
        out_specs=_row_spec(tm, D_MODEL),
        compiler_params=_params(("parallel",)), name="merge")(
            x2, o_f, o_b, rz, gn, o_nb, o_gq, wg, wa, wb, wc, wo, lg, lb)


PROJ_TM = 512
MERGE_TM = 256
RET_CHUNKS_PER_STEP = 8
GQA_TQ = 256
GQA_TK = 512
GQA_UNROLL = 6


def _rope_tables(seq):
    t = jnp.arange(seq)
    tf = t.astype(F32)
    row = (t // GRID_W).astype(F32)
    col = (t % GRID_W).astype(F32)
    inv_ret = 1.0 / (ROPE_THETA ** jnp.linspace(0.0, 1.0, RET_DK // 2, dtype=F32))
    ang_ret = tf[:, None] * inv_ret[None, :]
    n_ax = GQA_DH // 4
    inv_ax = ROPE_THETA ** (-jnp.arange(n_ax, dtype=F32) / n_ax)
    ang_ax = jnp.concatenate([row[:, None] * inv_ax[None, :], col[:, None] * inv_ax[None, :]], -1)

    def full(ang, reps):
        c, s = jnp.cos(ang), jnp.sin(ang)
        return jnp.tile(jnp.concatenate([c, c], -1), (1, reps)), jnp.tile(jnp.concatenate([-s, s], -1), (1, reps))

    return full(ang_ret, 1), full(ang_ax, LANES // GQA_DH)


def kernel(x, w_in, ret_theta_fwd, ret_theta_bwd, ret_gn_gain, na_rpb, gqa_q_norm, gqa_k_norm,
           w_branch_a, w_branch_b, w_branch_c, w_out, ln_gain, ln_bias):
    batch, seq, d_model = x.shape
    depth = w_in.shape[0]
    assert d_model == D_MODEL and w_in.shape[-1] == D_IN and seq % (GRID_W * NA_QROWS) == 0
    alpha = (2.0 * depth) ** 0.25
    m = batch * seq
    (ret_cos, ret_sin), (ax_cos, ax_sin) = _rope_tables(seq)
    gidx = np.arange(GQA_W) // GQA_DH
    gmat = jnp.asarray(gidx[:, None] == gidx[None, :], BF16)
    ones_col = jnp.concatenate([jnp.ones((m, 1), BF16), jnp.zeros((m, GQA_VROWS - GQA_DH - 1), BF16)], -1)

    x2 = x.reshape(m, D_MODEL)
    for l in range(depth):
        w = w_in[l].astype(BF16)
        rq, rk, rv, rz = _proj_ret(x2, w[:, OFF_RET:OFF_NA], ret_cos, ret_sin, seq, PROJ_TM)
        nq, nk, nv, nz = _proj_na(x2, w[:, OFF_NA:OFF_GQA], PROJ_TM)
        qn = jnp.tile(gqa_q_norm[l], GQA_HEADS)[None, :]
        kn = jnp.tile(gqa_k_norm[l], GQA_KV_HEADS)[None, :]
        cq, ck, cv, cz = _proj_gqa(x2, w[:, OFF_GQA:OFF_GATES], ax_cos, ax_sin, gmat, qn, kn, seq, PROJ_TM)

        decay = jnp.stack([jax.nn.log_sigmoid(ret_theta_fwd[l].astype(F32)),
                           jax.nn.log_sigmoid(ret_theta_bwd[l].astype(F32))], axis=1)
        decay = jnp.broadcast_to(jnp.pad(decay, ((0, 0), (0, 6)))[:, :, None], (RET_HEADS, 8, LANES))
        o_f, o_b = _retention(rq, rk, rv, decay, batch, seq, RET_CHUNKS_PER_STEP)

        bias = _na_bias_tables(na_rpb[l], seq // GRID_W)
        o_nb = _na_attention(nq, nk, nv, nz, bias, batch, seq)

        k2 = ck.reshape(batch, seq, GQA_KV_HEADS, GQA_DH).transpose(0, 2, 1, 3).reshape(
            batch * GQA_KV_HEADS, seq, GQA_DH)
        va = jnp.concatenate([cv.reshape(m, GQA_KV_HEADS, GQA_DH),
                              jnp.broadcast_to(ones_col[:, None, :], (m, GQA_KV_HEADS, GQA_VROWS - GQA_DH))], -1)
        vt = va.reshape(batch, seq, GQA_KV_HEADS, GQA_VROWS).transpose(0, 2, 3, 1).reshape(
            batch * GQA_KV_HEADS, GQA_VROWS, seq)
        o_gq = _gqa_attention(cq, k2, vt, cz, batch, seq, GQA_TQ, GQA_TK, GQA_UNROLL)

        x2 = _merge(x2, o_f, o_b, rz, ret_gn_gain[l][None, :], o_nb, o_gq,
                    w[:, OFF_GATES:D_IN], w_branch_a[l].astype(BF16), w_branch_b[l].astype(BF16),
                    w_branch_c[l].astype(BF16), w_out[l].astype(BF16),
                    ln_gain[l][None, :], ln_bias[l][None, :], alpha, MERGE_TM)
    return x2.reshape(batch, seq, D_MODEL)
```

```python
import functools
import math

import numpy as np
import jax
import jax.numpy as jnp
from jax import lax
from jax.experimental import pallas as pl
from jax.experimental.pallas import tpu as pltpu

D_MODEL = 1024
GRID_W = 64
RET_HEADS = 4
RET_DK = 128
RET_CHUNK = 128
NA_HEADS = 8
NA_DH = 64
NA_KH = 8
NA_KW = 16
GQA_HEADS = 8
GQA_KV_HEADS = 2
GQA_DH = 64
ROPE_THETA = 10000.0
RMS_EPS = 1e-6
LN_EPS = 1e-5

RET_W = RET_HEADS * RET_DK
NA_W = NA_HEADS * NA_DH
GQA_W = GQA_HEADS * GQA_DH
GQA_KV_W = GQA_KV_HEADS * GQA_DH

OFF_RET = 0
OFF_NA = 4 * RET_W
OFF_GQA = OFF_NA + 4 * NA_W
OFF_GATES = OFF_GQA + 2 * GQA_W + 2 * GQA_KV_W
D_IN = OFF_GATES + 3 * D_MODEL

LOG2E = math.log2(math.e)
LANES = 128
VMEM_LIMIT = 56 * 1024 * 1024

BF16 = jnp.bfloat16
F32 = jnp.float32


def _dot(a, b):
    return jnp.dot(a, b, preferred_element_type=F32)


def _dot_nt(a, b):
    return lax.dot_general(a, b, (((1,), (1,)), ((), ())), preferred_element_type=F32)


def _dot_tn(a, b):
    return lax.dot_general(a, b, (((0,), (0,)), ((), ())), preferred_element_type=F32)


def _silu(z):
    return z / (1.0 + jnp.exp(-z))


def _sigmoid(z):
    return 1.0 / (1.0 + jnp.exp(-z))


def _params(sem):
    return pltpu.CompilerParams(dimension_semantics=sem, vmem_limit_bytes=VMEM_LIMIT)


def _proj_ret_kernel(x_ref, w_ref, cos_ref, sin_ref, q_ref, k_ref, v_ref, z_ref):
    xb = x_ref[...].astype(BF16)
    cos = cos_ref[...]
    sin = sin_ref[...]
    scale = RET_DK ** -0.5
    yq = _dot(xb, w_ref[:, 0:RET_W])
    yk = _dot(xb, w_ref[:, RET_W:2 * RET_W])
    for h in range(RET_HEADS):
        sl = slice(h * RET_DK, (h + 1) * RET_DK)
        t = yq[:, sl]
        q_ref[:, sl] = ((t * cos + pltpu.roll(t, RET_DK // 2, 1) * sin) * scale).astype(BF16)
        t = yk[:, sl]
        k_ref[:, sl] = (t * cos + pltpu.roll(t, RET_DK // 2, 1) * sin).astype(BF16)
    v_ref[...] = _dot(xb, w_ref[:, 2 * RET_W:3 * RET_W]).astype(BF16)
    z_ref[...] = _silu(_dot(xb, w_ref[:, 3 * RET_W:4 * RET_W]))


def _proj_na_kernel(x_ref, w_ref, q_ref, k_ref, v_ref, z_ref):
    xb = x_ref[...].astype(BF16)
    q_ref[...] = (_dot(xb, w_ref[:, 0:NA_W]) * (LOG2E * NA_DH ** -0.5)).astype(BF16)
    k_ref[...] = _dot(xb, w_ref[:, NA_W:2 * NA_W]).astype(BF16)
    v_ref[...] = _dot(xb, w_ref[:, 2 * NA_W:3 * NA_W]).astype(BF16)
    z_ref[...] = _silu(_dot(xb, w_ref[:, 3 * NA_W:4 * NA_W]))


def _group_mean_sq(y, g_ref, width):
    sq = y * y
    hi = sq.astype(BF16)
    lo = (sq - hi.astype(F32)).astype(BF16)
    g = g_ref[0:width, 0:width]
    return (_dot(hi, g) + _dot(lo, g)) * (1.0 / GQA_DH)


def _rope64(t, cos, sin, first_half):
    swapped = jnp.where(first_half, pltpu.roll(t, LANES - GQA_DH // 2, 1), pltpu.roll(t, GQA_DH // 2, 1))
    return t * cos + swapped * sin


def _proj_gqa_kernel(x_ref, w_ref, cos_ref, sin_ref, g_ref, qn_ref, kn_ref,
                     q_ref, k_ref, vt_ref, z_ref):
    xb = x_ref[...].astype(BF16)
    cos = cos_ref[...]
    sin = sin_ref[...]
    lane = lax.broadcasted_iota(jnp.int32, cos.shape, 1)
    first_half = (lane % GQA_DH) < (GQA_DH // 2)
    yq = _dot(xb, w_ref[:, 0:GQA_W])
    qn = yq * lax.rsqrt(_group_mean_sq(yq, g_ref, GQA_W) + RMS_EPS) * qn_ref[...]
    for j in range(GQA_W // LANES):
        sl = slice(j * LANES, (j + 1) * LANES)
        q_ref[:, sl] = (_rope64(qn[:, sl], cos, sin, first_half) * (LOG2E * GQA_DH ** -0.5)).astype(BF16)
    yk = _dot(xb, w_ref[:, GQA_W:GQA_W + GQA_KV_W])
    kn = yk * lax.rsqrt(_group_mean_sq(yk, g_ref, GQA_KV_W) + RMS_EPS) * kn_ref[...]
    kr = _rope64(kn, cos, sin, first_half).astype(BF16)
    v_t = _dot(xb, w_ref[:, GQA_W + GQA_KV_W:GQA_W + 2 * GQA_KV_W]).T
    tm = v_t.shape[1]
    pad_row = lax.broadcasted_iota(jnp.int32, (GQA_VROWS - GQA_DH, tm), 0)
    ones_then_zeros = jnp.where(pad_row == 0, 1.0, 0.0)
    for n in range(GQA_KV_HEADS):
        k_ref[n] = kr[:, n * GQA_DH:(n + 1) * GQA_DH]
        vt_ref[n] = jnp.concatenate([v_t[n * GQA_DH:(n + 1) * GQA_DH, :], ones_then_zeros], axis=0).astype(BF16)
    z_ref[...] = _silu(_dot(xb, w_ref[:, GQA_W + 2 * GQA_KV_W:2 * GQA_W + 2 * GQA_KV_W]))


def _row_spec(tm, width):
    return pl.BlockSpec((tm, width), lambda i: (i, 0))


def _const_spec(shape):
    return pl.BlockSpec(shape, lambda i: (0,) * len(shape))


def _table_spec(tm, seq):
    nblk = seq // tm
    return pl.BlockSpec((tm, LANES), lambda i: (i % nblk, 0))


def _proj_ret(x2, w, cos, sin, seq, tm):
    m = x2.shape[0]
    out = [jax.ShapeDtypeStruct((m, RET_W), BF16)] * 3 + [jax.ShapeDtypeStruct((m, RET_W), F32)]
    return pl.pallas_call(
        _proj_ret_kernel, out_shape=out, grid=(m // tm,),
        in_specs=[_row_spec(tm, D_MODEL), _const_spec(w.shape), _table_spec(tm, seq), _table_spec(tm, seq)],
        out_specs=[_row_spec(tm, RET_W)] * 4,
        compiler_params=_params(("parallel",)), name="proj_ret")(x2, w, cos, sin)


def _proj_na(x2, w, tm):
    m = x2.shape[0]
    out = [jax.ShapeDtypeStruct((m, NA_W), BF16)] * 3 + [jax.ShapeDtypeStruct((m, NA_W), F32)]
    return pl.pallas_call(
        _proj_na_kernel, out_shape=out, grid=(m // tm,),
        in_specs=[_row_spec(tm, D_MODEL), _const_spec(w.shape)],
        out_specs=[_row_spec(tm, NA_W)] * 4,
        compiler_params=_params(("parallel",)), name="proj_na")(x2, w)


def _proj_gqa(x2, w, cos, sin, gmat, qn, kn, seq, tm):
    m = x2.shape[0]
    batch, nblk = m // seq, seq // tm
    out = [jax.ShapeDtypeStruct((m, GQA_W), BF16),
           jax.ShapeDtypeStruct((batch, GQA_KV_HEADS, seq, GQA_DH), BF16),
           jax.ShapeDtypeStruct((batch, GQA_KV_HEADS, GQA_VROWS, seq), BF16),
           jax.ShapeDtypeStruct((m, GQA_W), F32)]
    k_spec = pl.BlockSpec((None, GQA_KV_HEADS, tm, GQA_DH), lambda i: (i // nblk, 0, i % nblk, 0))
    vt_spec = pl.BlockSpec((None, GQA_KV_HEADS, GQA_VROWS, tm), lambda i: (i // nblk, 0, 0, i % nblk))
    return pl.pallas_call(
        _proj_gqa_kernel, out_shape=out, grid=(m // tm,),
        in_specs=[_row_spec(tm, D_MODEL), _const_spec(w.shape), _table_spec(tm, seq), _table_spec(tm, seq),
                  _const_spec(gmat.shape), _const_spec(qn.shape), _const_spec(kn.shape)],
        out_specs=[_row_spec(tm, GQA_W), k_spec, vt_spec, _row_spec(tm, GQA_W)],
        compiler_params=_params(("parallel",)), name="proj_gqa")(x2, w, cos, sin, gmat, qn, kn)


def _retention_kernel(dec_ref, qf_ref, kf_ref, vf_ref, qb_ref, kb_ref, vb_ref,
                      of_ref, ob_ref, sf_ref, sb_ref, *, chunks):
    c = RET_CHUNK

    @pl.when(pl.program_id(2) == 0)
    def _():
        sf_ref[...] = jnp.zeros_like(sf_ref)
        sb_ref[...] = jnp.zeros_like(sb_ref)

    lf = dec_ref[0:1, :]
    lb = dec_ref[1:2, :]
    row = lax.broadcasted_iota(jnp.int32, (c, c), 0).astype(F32)
    col = lax.broadcasted_iota(jnp.int32, (c, c), 1).astype(F32)
    diff = row - col
    d_intra = jnp.where(diff >= 0, jnp.exp(jnp.maximum(diff, 0.0) * lf), jnp.exp(jnp.maximum(-diff, 0.0) * lb))
    w_kf = jnp.exp((c - 1.0 - row) * lf)
    w_qf = jnp.exp((row + 1.0) * lf)
    w_kb = jnp.exp(row * lb)
    w_qb = jnp.exp((c - row) * lb)
    chunk_f = jnp.exp(c * lf)
    chunk_b = jnp.exp(c * lb)

    blocks = [slice(i * c, (i + 1) * c) for i in range(chunks)]

    kv_f = [_dot_tn((kf_ref[r, :].astype(F32) * w_kf).astype(BF16), vf_ref[r, :]) for r in blocks]
    kv_b = [_dot_tn((kb_ref[r, :].astype(F32) * w_kb).astype(BF16), vb_ref[r, :]) for r in blocks]
    intra = [_dot((_dot_nt(qf_ref[r, :], kf_ref[r, :]) * d_intra).astype(BF16), vf_ref[r, :]) for r in blocks]

    s_f = sf_ref[...]
    states_f = []
    for i in range(chunks):
        states_f.append(s_f)
        s_f = chunk_f * s_f + kv_f[i]
    sf_ref[...] = s_f
    s_b = sb_ref[...]
    states_b = [None] * chunks
    for i in reversed(range(chunks)):
        states_b[i] = s_b
        s_b = chunk_b * s_b + kv_b[i]
    sb_ref[...] = s_b

    for i, r in enumerate(blocks):
        qw = (qf_ref[r, :].astype(F32) * w_qf).astype(BF16)
        of_ref[r, :] = intra[i] + _dot(qw, states_f[i].astype(BF16))
        qw = (qb_ref[r, :].astype(F32) * w_qb).astype(BF16)
        ob_ref[r, :] = _dot(qw, states_b[i].astype(BF16))


def _retention(rq, rk, rv, decay, batch, seq, chunks):
    m = rq.shape[0]
    rows = chunks * RET_CHUNK
    ng = seq // rows
    fwd = pl.BlockSpec((rows, RET_DK), lambda b, h, g: (b * ng + g, h))
    bwd = pl.BlockSpec((rows, RET_DK), lambda b, h, g: (b * ng + ng - 1 - g, h))
    dec = pl.BlockSpec((None, 8, LANES), lambda b, h, g: (h, 0, 0))
    out = [jax.ShapeDtypeStruct((m, RET_W), F32)] * 2
    return pl.pallas_call(
        functools.partial(_retention_kernel, chunks=chunks), out_shape=out,
        grid=(batch, RET_HEADS, ng),
        in_specs=[dec, fwd, fwd, fwd, bwd, bwd, bwd],
        out_specs=[fwd, bwd],
        scratch_shapes=[pltpu.VMEM((RET_DK, RET_DK), F32), pltpu.VMEM((RET_DK, RET_DK), F32)],
        compiler_params=_params(("parallel", "parallel", "arbitrary")),
        name="retention")(decay, rq, rk, rv, rq, rk, rv)


NA_QROWS = 4
NA_KROWS = 3 * NA_QROWS
NA_TQ = NA_QROWS * GRID_W
NA_TK = NA_KROWS * GRID_W
NA_PAIRS = 4


def _na_kernel(q_ref, k0_ref, k1_ref, k2_ref, v0_ref, v1_ref, v2_ref, bias_ref, z_ref, o_ref,
               s_ref, p_ref, mx_ref, den_ref):
    lane = lax.broadcasted_iota(jnp.int32, (NA_TK, LANES), 1)
    zero = jnp.zeros((NA_TK, LANES), BF16)
    units = 2 * (q_ref.shape[1] // LANES)

    def masked(refs, u):
        cs = slice((u // 2) * LANES, (u // 2 + 1) * LANES)
        x = jnp.concatenate([r[:, cs] for r in refs], axis=0)
        own = (lane < NA_DH) if u % 2 == 0 else (lane >= NA_DH)
        return jnp.where(own, x, zero)

    def scores(u):
        cs = slice((u // 2) * LANES, (u // 2 + 1) * LANES)
        s = _dot_nt(q_ref[:, cs], masked((k0_ref, k1_ref, k2_ref), u)) + bias_ref[u]
        s_ref[u] = s
        mx_ref[u] = jnp.broadcast_to(jnp.max(s, axis=-1, keepdims=True), (NA_TQ, LANES))

    def softmax(u):
        mx = mx_ref[u]
        den = None
        for j in range(NA_TK // LANES):
            cs = slice(j * LANES, (j + 1) * LANES)
            p = jnp.exp2(s_ref[u, :, cs] - mx)
            p_ref[u, :, cs] = p.astype(BF16)
            den = p if den is None else den + p
        den_ref[u] = jnp.broadcast_to(jnp.sum(den, axis=-1, keepdims=True), (NA_TQ, LANES))

    outs = {}

    def values(u):
        o = _dot(p_ref[u], masked((v0_ref, v1_ref, v2_ref), u)) / den_ref[u]
        if u % 2 == 0:
            outs[u // 2] = o
        else:
            cs = slice((u // 2) * LANES, (u // 2 + 1) * LANES)
            o_ref[:, cs] = ((outs.pop(u // 2) + o) * z_ref[:, cs]).astype(o_ref.dtype)

    for t in range(units + 2):
        if t < units:
            scores(t)
        if 1 <= t <= units:
            softmax(t - 1)
        if t >= 2:
            values(t - 2)


def _na_bias_tables(rpb, rows):
    nblk = rows // NA_QROWS
    nrow, ncol = 2 * NA_KH - 1, 2 * NA_KW - 1
    col = np.arange(GRID_W)
    c_start = np.clip(col - NA_KW // 2, 0, GRID_W - NA_KW)
    col_ok = (col[None, :] >= c_start[:, None]) & (col[None, :] < c_start[:, None] + NA_KW)
    cidx = np.clip(col[None, :] - col[:, None] + NA_KW - 1, 0, ncol - 1)
    col_sel = (cidx[:, :, None] == np.arange(ncol)).astype(np.float32)
    row_sel = np.zeros((3, NA_QROWS, NA_KROWS, nrow), np.float32)
    row_ok = np.zeros((3, NA_QROWS, NA_KROWS), bool)
    for cls, blk in enumerate((0, 1, nblk - 1)):
        first_kblk = min(max(blk - 1, 0), nblk - 3)
        qr = blk * NA_QROWS + np.arange(NA_QROWS)
        kr = first_kblk * NA_QROWS + np.arange(NA_KROWS)
        r_start = np.clip(qr - NA_KH // 2, 0, rows - NA_KH)
        ok = (kr[None, :] >= r_start[:, None]) & (kr[None, :] < r_start[:, None] + NA_KH)
        ridx = kr[None, :] - qr[:, None] + NA_KH - 1
        row_ok[cls] = ok
        row_sel[cls] = ((ridx[:, :, None] == np.arange(nrow)) & ok[:, :, None])
    by_col = jnp.einsum('har,qkr->haqk', rpb.astype(F32), col_sel, precision=lax.Precision.HIGHEST)
    vals = jnp.einsum('cija,haqk->chiqjk', row_sel, by_col, precision=lax.Precision.HIGHEST)
    valid = row_ok[:, None, :, None, :, None] & col_ok[None, None, None, :, None, :]
    table = jnp.where(valid, vals * LOG2E, -jnp.inf)
    return table.reshape(3, NA_HEADS, NA_TQ, NA_TK)


def _na_attention(nq, nk, nv, nz, bias, batch, seq):
    m = nq.shape[0]
    rows = seq // GRID_W
    nblk = rows // NA_QROWS
    groups = NA_HEADS // (2 * NA_PAIRS)
    width = NA_PAIRS * LANES

    def kblk(j):
        return lambda b, hp, i: (b * nblk + jnp.clip(i - 1, 0, nblk - 3) + j, hp)

    def bias_idx(b, hp, i):
        cls = jnp.where(i == 0, 0, jnp.where(i == nblk - 1, 2, 1))
        return (cls, hp, 0, 0)

    qspec = pl.BlockSpec((NA_TQ, width), lambda b, hp, i: (b * nblk + i, hp))
    kspecs = [pl.BlockSpec((NA_TQ, width), kblk(j)) for j in range(3)]
    return pl.pallas_call(
        _na_kernel, out_shape=jax.ShapeDtypeStruct((m, NA_W), BF16),
        grid=(batch, groups, nblk),
        in_specs=[qspec] + kspecs + kspecs + [pl.BlockSpec((None, 2 * NA_PAIRS, NA_TQ, NA_TK), bias_idx), qspec],
        out_specs=qspec,
        scratch_shapes=[pltpu.VMEM((2 * NA_PAIRS, NA_TQ, NA_TK), F32),
                        pltpu.VMEM((2 * NA_PAIRS, NA_TQ, NA_TK), BF16),
                        pltpu.VMEM((2 * NA_PAIRS, NA_TQ, LANES), F32),
                        pltpu.VMEM((2 * NA_PAIRS, NA_TQ, LANES), F32)],
        compiler_params=_params(("parallel", "parallel", "arbitrary")),
        name="na_attention")(nq, nk, nk, nk, nv, nv, nv, bias, nz)


GQA_GROUP = GQA_HEADS // GQA_KV_HEADS
GQA_VROWS = 80


def _gqa_kernel(q_ref, k_ref, vt_ref, z_ref, o_ref,
                qs_ref, s_ref, mb_ref, m_ref, acc_ref, *, tk, unroll):
    tq = q_ref.shape[0]
    nk = k_ref.shape[0] // tk
    assert nk % 2 == 0 and unroll % 2 == 0 and nk >= 1 + unroll

    def chunk_start(c):
        return c * tk if isinstance(c, int) else pl.multiple_of(c * tk, tk)
    q_t = q_ref[...].astype(F32).T.astype(BF16)
    for g in range(GQA_GROUP):
        qs_ref[:, g * tq:(g + 1) * tq] = q_t[g * GQA_DH:(g + 1) * GQA_DH, :]
    m_ref[...] = jnp.full(m_ref.shape, -jnp.inf, F32)
    acc_ref[...] = jnp.zeros_like(acc_ref)

    def scores(c, par):
        s = _dot(k_ref[pl.ds(chunk_start(c), tk), :], qs_ref[...])
        s_ref[par] = s
        mb_ref[par] = jnp.max(s, axis=0, keepdims=True)

    def softmax_values(c, par):
        m_old = m_ref[...]
        m_new = jnp.maximum(m_old, mb_ref[par])
        m_ref[...] = m_new
        p = jnp.exp2(s_ref[par] - m_new).astype(BF16)
        pv = _dot(vt_ref[:, pl.ds(chunk_start(c), tk)], p)
        acc_ref[...] = jnp.exp2(m_old - m_new) * acc_ref[...] + pv

    def tick(i, par):
        scores(i, par)
        softmax_values(i - 1, 1 - par)

    scores(0, 0)
    first = 1 + (nk - 1) % unroll
    for i in range(1, first):
        tick(i, i % 2)

    def body(j, carry):
        i = first + unroll * j
        for u in range(unroll):
            tick(i + u, (first + u) % 2)
        return carry

    lax.fori_loop(0, (nk - first) // unroll, body, 0)
    softmax_values(nk - 1, 1)

    inv = 1.0 / acc_ref[GQA_DH:GQA_DH + 1, :]
    out = acc_ref[...] * inv
    heads = [out[0:GQA_DH, g * tq:(g + 1) * tq] for g in range(GQA_GROUP)]
    o_ref[...] = (jnp.concatenate(heads, axis=0).T * z_ref[...]).astype(o_ref.dtype)


def _gqa_attention(cq, k2, vt, cz, batch, seq, tq, tk, unroll):
    m = cq.shape[0]
    nq = seq // tq
    width = GQA_GROUP * GQA_DH
    cols = GQA_GROUP * tq
    qspec = pl.BlockSpec((tq, width), lambda b, n, i: (b * nq + i, n))
    return pl.pallas_call(
        functools.partial(_gqa_kernel, tk=tk, unroll=unroll), out_shape=jax.ShapeDtypeStruct((m, GQA_W), BF16),
        grid=(batch, GQA_KV_HEADS, nq),
        in_specs=[qspec,
                  pl.BlockSpec((None, None, seq, GQA_DH), lambda b, n, i: (b, n, 0, 0)),
                  pl.BlockSpec((None, None, GQA_VROWS, seq), lambda b, n, i: (b, n, 0, 0)),
                  qspec],
        out_specs=qspec,
        scratch_shapes=[pltpu.VMEM((GQA_DH, cols), BF16),
                        pltpu.VMEM((2, tk, cols), F32),
                        pltpu.VMEM((2, 1, cols), F32),
                        pltpu.VMEM((1, cols), F32),
                        pltpu.VMEM((GQA_VROWS, cols), F32)],
        compiler_params=_params(("parallel", "parallel", "arbitrary")),
        name="gqa_attention")(cq, k2, vt, cz)


def _merge_kernel(x_ref, of_ref, ob_ref, rz_ref, gn_ref, onb_ref, ogq_ref,
                  wg_ref, wa_ref, wb_ref, wc_ref, wo_ref, lg_ref, lb_ref, o_ref, *, alpha):
    x = x_ref[...]
    xb = x.astype(BF16)
    ra = of_ref[...] + ob_ref[...]
    parts = []
    for h in range(RET_HEADS):
        t = ra[:, h * RET_DK:(h + 1) * RET_DK]
        mu = jnp.mean(t, axis=-1, keepdims=True)
        d = t - mu
        var = jnp.mean(d * d, axis=-1, keepdims=True)
        parts.append(d * lax.rsqrt(var + LN_EPS))
    o_a = (jnp.concatenate(parts, axis=-1) * gn_ref[...] * rz_ref[...]).astype(BF16)
    y = _sigmoid(_dot(xb, wg_ref[:, 0:D_MODEL])) * _dot(o_a, wa_ref[...])
    y = y + _sigmoid(_dot(xb, wg_ref[:, D_MODEL:2 * D_MODEL])) * _dot(onb_ref[...], wb_ref[...])
    y = y + _sigmoid(_dot(xb, wg_ref[:, 2 * D_MODEL:3 * D_MODEL])) * _dot(ogq_ref[...], wc_ref[...])
    z = alpha * x + _dot(y.astype(BF16), wo_ref[...])
    mu = jnp.mean(z, axis=-1, keepdims=True)
    d = z - mu
    var = jnp.mean(d * d, axis=-1, keepdims=True)
    o_ref[...] = d * lax.rsqrt(var + LN_EPS) * lg_ref[...] + lb_ref[...]


def _merge(x2, o_f, o_b, rz, gn, o_nb, o_gq, wg, wa, wb, wc, wo, lg, lb, alpha, tm):
    m = x2.shape[0]
    return pl.pallas_call(
        functools.partial(_merge_kernel, alpha=alpha), out_shape=jax.ShapeDtypeStruct((m, D_MODEL), F32),
        grid=(m // tm,),
        in_specs=[_row_spec(tm, D_MODEL), _row_spec(tm, RET_W), _row_spec(tm, RET_W), _row_spec(tm, RET_W),
                  _const_spec(gn.shape), _row_spec(tm, NA_W), _row_spec(tm, GQA_W),
                  _const_spec(wg.shape), _const_spec(wa.shape), _const_spec(wb.shape), _const_spec(wc.shape),
                  _const_spec(wo.shape), _const_spec(lg.shape), _const_spec(lb.shape)],
        out_specs=_row_spec(tm, D_MODEL),
        compiler_params=_params(("parallel",)), name="merge")(
            x2, o_f, o_b, rz, gn, o_nb, o_gq, wg, wa, wb, wc, wo, lg, lb)


PROJ_TM = 512
MERGE_TM = 256
RET_CHUNKS_PER_STEP = 8
GQA_TQ = 256
GQA_TK = 512
GQA_UNROLL = 6


def _rope_tables(seq):
    t = jnp.arange(seq)
    tf = t.astype(F32)
    row = (t // GRID_W).astype(F32)
    col = (t % GRID_W).astype(F32)
    inv_ret = 1.0 / (ROPE_THETA ** jnp.linspace(0.0, 1.0, RET_DK // 2, dtype=F32))
    ang_ret = tf[:, None] * inv_ret[None, :]
    n_ax = GQA_DH // 4
    inv_ax = ROPE_THETA ** (-jnp.arange(n_ax, dtype=F32) / n_ax)
    ang_ax = jnp.concatenate([row[:, None] * inv_ax[None, :], col[:, None] * inv_ax[None, :]], -1)

    def full(ang, reps):
        c, s = jnp.cos(ang), jnp.sin(ang)
        return jnp.tile(jnp.concatenate([c, c], -1), (1, reps)), jnp.tile(jnp.concatenate([-s, s], -1), (1, reps))

    return full(ang_ret, 1), full(ang_ax, LANES // GQA_DH)


def kernel(x, w_in, ret_theta_fwd, ret_theta_bwd, ret_gn_gain, na_rpb, gqa_q_norm, gqa_k_norm,
           w_branch_a, w_branch_b, w_branch_c, w_out, ln_gain, ln_bias):
    batch, seq, d_model = x.shape
    depth = w_in.shape[0]
    assert d_model == D_MODEL and w_in.shape[-1] == D_IN and seq % (GRID_W * NA_QROWS) == 0
    alpha = (2.0 * depth) ** 0.25
    m = batch * seq
    (ret_cos, ret_sin), (ax_cos, ax_sin) = _rope_tables(seq)
    gidx = np.arange(GQA_W) // GQA_DH
    gmat = jnp.asarray(gidx[:, None] == gidx[None, :], BF16)

    x2 = x.reshape(m, D_MODEL)
    for l in range(depth):
        w = w_in[l].astype(BF16)
        rq, rk, rv, rz = _proj_ret(x2, w[:, OFF_RET:OFF_NA], ret_cos, ret_sin, seq, PROJ_TM)
        nq, nk, nv, nz = _proj_na(x2, w[:, OFF_NA:OFF_GQA], PROJ_TM)
        qn = jnp.tile(gqa_q_norm[l], GQA_HEADS)[None, :]
        kn = jnp.tile(gqa_k_norm[l], GQA_KV_HEADS)[None, :]
        cq, ck, cvt, cz = _proj_gqa(x2, w[:, OFF_GQA:OFF_GATES], ax_cos, ax_sin, gmat, qn, kn, seq, PROJ_TM)

        decay = jnp.stack([jax.nn.log_sigmoid(ret_theta_fwd[l].astype(F32)),
                           jax.nn.log_sigmoid(ret_theta_bwd[l].astype(F32))], axis=1)
        decay = jnp.broadcast_to(jnp.pad(decay, ((0, 0), (0, 6)))[:, :, None], (RET_HEADS, 8, LANES))
        o_f, o_b = _retention(rq, rk, rv, decay, batch, seq, RET_CHUNKS_PER_STEP)

        bias = _na_bias_tables(na_rpb[l], seq // GRID_W)
        o_nb = _na_attention(nq, nk, nv, nz, bias, batch, seq)

        o_gq = _gqa_attention(cq, ck, cvt, cz, batch, seq, GQA_TQ, GQA_TK, GQA_UNROLL)

        x2 = _merge(x2, o_f, o_b, rz, ret_gn_gain[l][None, :], o_nb, o_gq,
                    w[:, OFF_GATES:D_IN], w_branch_a[l].astype(BF16), w_branch_b[l].astype(BF16),
                    w_branch_c[l].astype(BF16), w_out[l].astype(BF16),
                    ln_gain[l][None, :], ln_bias[l][None, :], alpha, MERGE_TM)
    return x2.reshape(batch, seq, D_MODEL)
```

```python
import functools
import math

import numpy as np
import jax
import jax.numpy as jnp
from jax import lax
from jax.experimental import pallas as pl
from jax.experimental.pallas import tpu as pltpu

D_MODEL = 1024
GRID_W = 64
RET_HEADS = 4
RET_DK = 128
RET_CHUNK = 128
NA_HEADS = 8
NA_DH = 64
NA_KH = 8
NA_KW = 16
GQA_HEADS = 8
GQA_KV_HEADS = 2
GQA_DH = 64
ROPE_THETA = 10000.0
RMS_EPS = 1e-6
LN_EPS = 1e-5

RET_W = RET_HEADS * RET_DK
NA_W = NA_HEADS * NA_DH
GQA_W = GQA_HEADS * GQA_DH
GQA_KV_W = GQA_KV_HEADS * GQA_DH

OFF_RET = 0
OFF_NA = 4 * RET_W
OFF_GQA = OFF_NA + 4 * NA_W
OFF_GATES = OFF_GQA + 2 * GQA_W + 2 * GQA_KV_W
D_IN = OFF_GATES + 3 * D_MODEL

LOG2E = math.log2(math.e)
LANES = 128
VMEM_LIMIT = 56 * 1024 * 1024

BF16 = jnp.bfloat16
F32 = jnp.float32


def _dot(a, b):
    return jnp.dot(a, b, preferred_element_type=F32)


def _dot_nt(a, b):
    return lax.dot_general(a, b, (((1,), (1,)), ((), ())), preferred_element_type=F32)


def _dot_tn(a, b):
    return lax.dot_general(a, b, (((0,), (0,)), ((), ())), preferred_element_type=F32)


def _silu(z):
    return z / (1.0 + jnp.exp(-z))


def _sigmoid(z):
    return 1.0 / (1.0 + jnp.exp(-z))


def _params(sem):
    return pltpu.CompilerParams(dimension_semantics=sem, vmem_limit_bytes=VMEM_LIMIT)


def _proj_ret_kernel(x_ref, w_ref, cos_ref, sin_ref, q_ref, k_ref, v_ref, z_ref):
    xb = x_ref[...].astype(BF16)
    cos = cos_ref[...]
    sin = sin_ref[...]
    scale = RET_DK ** -0.5
    yq = _dot(xb, w_ref[:, 0:RET_W])
    yk = _dot(xb, w_ref[:, RET_W:2 * RET_W])
    for h in range(RET_HEADS):
        sl = slice(h * RET_DK, (h + 1) * RET_DK)
        t = yq[:, sl]
        q_ref[:, sl] = ((t * cos + pltpu.roll(t, RET_DK // 2, 1) * sin) * scale).astype(BF16)
        t = yk[:, sl]
        k_ref[:, sl] = (t * cos + pltpu.roll(t, RET_DK // 2, 1) * sin).astype(BF16)
    v_ref[...] = _dot(xb, w_ref[:, 2 * RET_W:3 * RET_W]).astype(BF16)
    z_ref[...] = _silu(_dot(xb, w_ref[:, 3 * RET_W:4 * RET_W]))


def _proj_na_kernel(x_ref, w_ref, q_ref, k_ref, v_ref, z_ref):
    xb = x_ref[...].astype(BF16)
    q_ref[...] = (_dot(xb, w_ref[:, 0:NA_W]) * (LOG2E * NA_DH ** -0.5)).astype(BF16)
    k_ref[...] = _dot(xb, w_ref[:, NA_W:2 * NA_W]).astype(BF16)
    v_ref[...] = _dot(xb, w_ref[:, 2 * NA_W:3 * NA_W]).astype(BF16)
    z_ref[...] = _silu(_dot(xb, w_ref[:, 3 * NA_W:4 * NA_W]))


def _group_mean_sq(y, g_ref, width):
    sq = y * y
    hi = sq.astype(BF16)
    lo = (sq - hi.astype(F32)).astype(BF16)
    g = g_ref[0:width, 0:width]
    return (_dot(hi, g) + _dot(lo, g)) * (1.0 / GQA_DH)


def _rope64(t, cos, sin, first_half):
    swapped = jnp.where(first_half, pltpu.roll(t, LANES - GQA_DH // 2, 1), pltpu.roll(t, GQA_DH // 2, 1))
    return t * cos + swapped * sin


def _proj_gqa_kernel(x_ref, w_ref, cos_ref, sin_ref, g_ref, qn_ref, kn_ref,
                     q_ref, k_ref, vt_ref, z_ref):
    xb = x_ref[...].astype(BF16)
    cos = cos_ref[...]
    sin = sin_ref[...]
    lane = lax.broadcasted_iota(jnp.int32, cos.shape, 1)
    first_half = (lane % GQA_DH) < (GQA_DH // 2)
    yq = _dot(xb, w_ref[:, 0:GQA_W])
    qn = yq * lax.rsqrt(_group_mean_sq(yq, g_ref, GQA_W) + RMS_EPS) * qn_ref[...]
    for j in range(GQA_W // LANES):
        sl = slice(j * LANES, (j + 1) * LANES)
        q_ref[:, sl] = (_rope64(qn[:, sl], cos, sin, first_half) * (LOG2E * GQA_DH ** -0.5)).astype(BF16)
    yk = _dot(xb, w_ref[:, GQA_W:GQA_W + GQA_KV_W])
    kn = yk * lax.rsqrt(_group_mean_sq(yk, g_ref, GQA_KV_W) + RMS_EPS) * kn_ref[...]
    kr = _rope64(kn, cos, sin, first_half).astype(BF16)
    v_t = _dot(xb, w_ref[:, GQA_W + GQA_KV_W:GQA_W + 2 * GQA_KV_W]).T
    tm = v_t.shape[1]
    pad_row = lax.broadcasted_iota(jnp.int32, (GQA_VROWS - GQA_DH, tm), 0)
    ones_then_zeros = jnp.where(pad_row == 0, 1.0, 0.0)
    for n in range(GQA_KV_HEADS):
        k_ref[n] = kr[:, n * GQA_DH:(n + 1) * GQA_DH]
        vt_ref[n] = jnp.concatenate([v_t[n * GQA_DH:(n + 1) * GQA_DH, :], ones_then_zeros], axis=0).astype(BF16)
    z_ref[...] = _silu(_dot(xb, w_ref[:, GQA_W + 2 * GQA_KV_W:2 * GQA_W + 2 * GQA_KV_W]))


def _row_spec(tm, width):
    return pl.BlockSpec((tm, width), lambda i: (i, 0))


def _const_spec(shape):
    return pl.BlockSpec(shape, lambda i: (0,) * len(shape))


def _table_spec(tm, seq):
    nblk = seq // tm
    return pl.BlockSpec((tm, LANES), lambda i: (i % nblk, 0))


def _proj_ret(x2, w, cos, sin, seq, tm):
    m = x2.shape[0]
    out = [jax.ShapeDtypeStruct((m, RET_W), BF16)] * 3 + [jax.ShapeDtypeStruct((m, RET_W), F32)]
    return pl.pallas_call(
        _proj_ret_kernel, out_shape=out, grid=(m // tm,),
        in_specs=[_row_spec(tm, D_MODEL), _const_spec(w.shape), _table_spec(tm, seq), _table_spec(tm, seq)],
        out_specs=[_row_spec(tm, RET_W)] * 4,
        compiler_params=_params(("parallel",)), name="proj_ret")(x2, w, cos, sin)


def _proj_na(x2, w, tm):
    m = x2.shape[0]
    out = [jax.ShapeDtypeStruct((m, NA_W), BF16)] * 3 + [jax.ShapeDtypeStruct((m, NA_W), F32)]
    return pl.pallas_call(
        _proj_na_kernel, out_shape=out, grid=(m // tm,),
        in_specs=[_row_spec(tm, D_MODEL), _const_spec(w.shape)],
        out_specs=[_row_spec(tm, NA_W)] * 4,
        compiler_params=_params(("parallel",)), name="proj_na")(x2, w)


def _proj_gqa(x2, w, cos, sin, gmat, qn, kn, seq, tm):
    m = x2.shape[0]
    batch, nblk = m // seq, seq // tm
    out = [jax.ShapeDtypeStruct((m, GQA_W), BF16),
           jax.ShapeDtypeStruct((batch, GQA_KV_HEADS, seq, GQA_DH), BF16),
           jax.ShapeDtypeStruct((batch, GQA_KV_HEADS, GQA_VROWS, seq), BF16),
           jax.ShapeDtypeStruct((m, GQA_W), F32)]
    k_spec = pl.BlockSpec((None, GQA_KV_HEADS, tm, GQA_DH), lambda i: (i // nblk, 0, i % nblk, 0))
    vt_spec = pl.BlockSpec((None, GQA_KV_HEADS, GQA_VROWS, tm), lambda i: (i // nblk, 0, 0, i % nblk))
    return pl.pallas_call(
        _proj_gqa_kernel, out_shape=out, grid=(m // tm,),
        in_specs=[_row_spec(tm, D_MODEL), _const_spec(w.shape), _table_spec(tm, seq), _table_spec(tm, seq),
                  _const_spec(gmat.shape), _const_spec(qn.shape), _const_spec(kn.shape)],
        out_specs=[_row_spec(tm, GQA_W), k_spec, vt_spec, _row_spec(tm, GQA_W)],
        compiler_params=_params(("parallel",)), name="proj_gqa")(x2, w, cos, sin, gmat, qn, kn)


def _retention_kernel(dec_ref, qf_ref, kf_ref, vf_ref, qb_ref, kb_ref, vb_ref,
                      of_ref, ob_ref, sf_ref, sb_ref, *, chunks):
    c = RET_CHUNK

    @pl.when(pl.program_id(2) == 0)
    def _():
        sf_ref[...] = jnp.zeros_like(sf_ref)
        sb_ref[...] = jnp.zeros_like(sb_ref)

    lf = dec_ref[0:1, :]
    lb = dec_ref[1:2, :]
    row = lax.broadcasted_iota(jnp.int32, (c, c), 0).astype(F32)
    col = lax.broadcasted_iota(jnp.int32, (c, c), 1).astype(F32)
    diff = row - col
    d_intra = jnp.where(diff >= 0, jnp.exp(jnp.maximum(diff, 0.0) * lf), jnp.exp(jnp.maximum(-diff, 0.0) * lb))
    w_kf = jnp.exp((c - 1.0 - row) * lf)
    w_qf = jnp.exp((row + 1.0) * lf)
    w_kb = jnp.exp(row * lb)
    w_qb = jnp.exp((c - row) * lb)
    chunk_f = jnp.exp(c * lf)
    chunk_b = jnp.exp(c * lb)

    blocks = [slice(i * c, (i + 1) * c) for i in range(chunks)]

    kv_f = [_dot_tn((kf_ref[r, :].astype(F32) * w_kf).astype(BF16), vf_ref[r, :]) for r in blocks]
    kv_b = [_dot_tn((kb_ref[r, :].astype(F32) * w_kb).astype(BF16), vb_ref[r, :]) for r in blocks]
    intra = [_dot((_dot_nt(qf_ref[r, :], kf_ref[r, :]) * d_intra).astype(BF16), vf_ref[r, :]) for r in blocks]

    s_f = sf_ref[...]
    states_f = []
    for i in range(chunks):
        states_f.append(s_f)
        s_f = chunk_f * s_f + kv_f[i]
    sf_ref[...] = s_f
    s_b = sb_ref[...]
    states_b = [None] * chunks
    for i in reversed(range(chunks)):
        states_b[i] = s_b
        s_b = chunk_b * s_b + kv_b[i]
    sb_ref[...] = s_b

    for i, r in enumerate(blocks):
        qw = (qf_ref[r, :].astype(F32) * w_qf).astype(BF16)
        of_ref[r, :] = intra[i] + _dot(qw, states_f[i].astype(BF16))
        qw = (qb_ref[r, :].astype(F32) * w_qb).astype(BF16)
        ob_ref[r, :] = _dot(qw, states_b[i].astype(BF16))


def _retention(rq, rk, rv, decay, batch, seq, chunks):
    m = rq.shape[0]
    rows = chunks * RET_CHUNK
    ng = seq // rows
    fwd = pl.BlockSpec((rows, RET_DK), lambda b, h, g: (b * ng + g, h))
    bwd = pl.BlockSpec((rows, RET_DK), lambda b, h, g: (b * ng + ng - 1 - g, h))
    dec = pl.BlockSpec((None, 8, LANES), lambda b, h, g: (h, 0, 0))
    out = [jax.ShapeDtypeStruct((m, RET_W), F32)] * 2
    return pl.pallas_call(
        functools.partial(_retention_kernel, chunks=chunks), out_shape=out,
        grid=(batch, RET_HEADS, ng),
        in_specs=[dec, fwd, fwd, fwd, bwd, bwd, bwd],
        out_specs=[fwd, bwd],
        scratch_shapes=[pltpu.VMEM((RET_DK, RET_DK), F32), pltpu.VMEM((RET_DK, RET_DK), F32)],
        compiler_params=_params(("parallel", "parallel", "arbitrary")),
        name="retention")(decay, rq, rk, rv, rq, rk, rv)


NA_QROWS = 4
NA_KROWS = 3 * NA_QROWS
NA_TQ = NA_QROWS * GRID_W
NA_TK = NA_KROWS * GRID_W
NA_PAIRS = 4


def _na_kernel(q_ref, k0_ref, k1_ref, k2_ref, v0_ref, v1_ref, v2_ref, bias_ref, z_ref, o_ref,
               s_ref, p_ref, mx_ref, den_ref):
    lane = lax.broadcasted_iota(jnp.int32, (NA_TK, LANES), 1)
    zero = jnp.zeros((NA_TK, LANES), BF16)
    units = 2 * (q_ref.shape[1] // LANES)

    def masked(refs, u):
        cs = slice((u // 2) * LANES, (u // 2 + 1) * LANES)
        x = jnp.concatenate([r[:, cs] for r in refs], axis=0)
        own = (lane < NA_DH) if u % 2 == 0 else (lane >= NA_DH)
        return jnp.where(own, x, zero)

    def scores(u):
        cs = slice((u // 2) * LANES, (u // 2 + 1) * LANES)
        s = _dot_nt(q_ref[:, cs], masked((k0_ref, k1_ref, k2_ref), u)) + bias_ref[u]
        s_ref[u] = s
        mx_ref[u] = jnp.broadcast_to(jnp.max(s, axis=-1, keepdims=True), (NA_TQ, LANES))

    def softmax(u):
        mx = mx_ref[u]
        den = None
        for j in range(NA_TK // LANES):
            cs = slice(j * LANES, (j + 1) * LANES)
            p = jnp.exp2(s_ref[u, :, cs] - mx)
            p_ref[u, :, cs] = p.astype(BF16)
            den = p if den is None else den + p
        den_ref[u] = jnp.broadcast_to(jnp.sum(den, axis=-1, keepdims=True), (NA_TQ, LANES))

    outs = {}

    def values(u):
        o = _dot(p_ref[u], masked((v0_ref, v1_ref, v2_ref), u)) / den_ref[u]
        if u % 2 == 0:
            outs[u // 2] = o
        else:
            cs = slice((u // 2) * LANES, (u // 2 + 1) * LANES)
            o_ref[:, cs] = ((outs.pop(u // 2) + o) * z_ref[:, cs]).astype(o_ref.dtype)

    for t in range(units + 2):
        if t < units:
            scores(t)
        if 1 <= t <= units:
            softmax(t - 1)
        if t >= 2:
            values(t - 2)


def _na_bias_tables(rpb, rows):
    nblk = rows // NA_QROWS
    nrow, ncol = 2 * NA_KH - 1, 2 * NA_KW - 1
    col = np.arange(GRID_W)
    c_start = np.clip(col - NA_KW // 2, 0, GRID_W - NA_KW)
    col_ok = (col[None, :] >= c_start[:, None]) & (col[None, :] < c_start[:, None] + NA_KW)
    cidx = np.clip(col[None, :] - col[:, None] + NA_KW - 1, 0, ncol - 1)
    col_sel = (cidx[:, :, None] == np.arange(ncol)).astype(np.float32)
    row_sel = np.zeros((3, NA_QROWS, NA_KROWS, nrow), np.float32)
    row_ok = np.zeros((3, NA_QROWS, NA_KROWS), bool)
    for cls, blk in enumerate((0, 1, nblk - 1)):
        first_kblk = min(max(blk - 1, 0), nblk - 3)
        qr = blk * NA_QROWS + np.arange(NA_QROWS)
        kr = first_kblk * NA_QROWS + np.arange(NA_KROWS)
        r_start = np.clip(qr - NA_KH // 2, 0, rows - NA_KH)
        ok = (kr[None, :] >= r_start[:, None]) & (kr[None, :] < r_start[:, None] + NA_KH)
        ridx = kr[None, :] - qr[:, None] + NA_KH - 1
        row_ok[cls] = ok
        row_sel[cls] = ((ridx[:, :, None] == np.arange(nrow)) & ok[:, :, None])
    by_col = jnp.einsum('har,qkr->haqk', rpb.astype(F32), col_sel, precision=lax.Precision.HIGHEST)
    vals = jnp.einsum('cija,haqk->chiqjk', row_sel, by_col, precision=lax.Precision.HIGHEST)
    valid = row_ok[:, None, :, None, :, None] & col_ok[None, None, None, :, None, :]
    table = jnp.where(valid, vals * LOG2E, -jnp.inf)
    return table.reshape(3, NA_HEADS, NA_TQ, NA_TK)


def _na_attention(nq, nk, nv, nz, bias, batch, seq):
    m = nq.shape[0]
    rows = seq // GRID_W
    nblk = rows // NA_QROWS
    groups = NA_HEADS // (2 * NA_PAIRS)
    width = NA_PAIRS * LANES

    def kblk(j):
        return lambda b, hp, i: (b * nblk + jnp.clip(i - 1, 0, nblk - 3) + j, hp)

    def bias_idx(b, hp, i):
        cls = jnp.where(i == 0, 0, jnp.where(i == nblk - 1, 2, 1))
        return (cls, hp, 0, 0)

    qspec = pl.BlockSpec((NA_TQ, width), lambda b, hp, i: (b * nblk + i, hp))
    kspecs = [pl.BlockSpec((NA_TQ, width), kblk(j)) for j in range(3)]
    return pl.pallas_call(
        _na_kernel, out_shape=jax.ShapeDtypeStruct((m, NA_W), BF16),
        grid=(batch, groups, nblk),
        in_specs=[qspec] + kspecs + kspecs + [pl.BlockSpec((None, 2 * NA_PAIRS, NA_TQ, NA_TK), bias_idx), qspec],
        out_specs=qspec,
        scratch_shapes=[pltpu.VMEM((2 * NA_PAIRS, NA_TQ, NA_TK), F32),
                        pltpu.VMEM((2 * NA_PAIRS, NA_TQ, NA_TK), BF16),
                        pltpu.VMEM((2 * NA_PAIRS, NA_TQ, LANES), F32),
                        pltpu.VMEM((2 * NA_PAIRS, NA_TQ, LANES), F32)],
        compiler_params=_params(("parallel", "parallel", "arbitrary")),
        name="na_attention")(nq, nk, nk, nk, nv, nv, nv, bias, nz)


GQA_GROUP = GQA_HEADS // GQA_KV_HEADS
GQA_VROWS = 128


def _gqa_kernel(q_ref, k_ref, vt_ref, z_ref, o_ref,
                qs_ref, s_ref, mb_ref, m_ref, acc_ref, *, tk, unroll):
    tq = q_ref.shape[0]
    nk = k_ref.shape[0] // tk
    assert nk % 2 == 0 and unroll % 2 == 0 and nk >= 1 + unroll

    def chunk_start(c):
        return c * tk if isinstance(c, int) else pl.multiple_of(c * tk, tk)
    q_t = q_ref[...].astype(F32).T.astype(BF16)
    for g in range(GQA_GROUP):
        qs_ref[:, g * tq:(g + 1) * tq] = q_t[g * GQA_DH:(g + 1) * GQA_DH, :]
    m_ref[...] = jnp.full(m_ref.shape, -jnp.inf, F32)
    acc_ref[...] = jnp.zeros_like(acc_ref)

    def scores(c, par):
        s = _dot(k_ref[pl.ds(chunk_start(c), tk), :], qs_ref[...])
        s_ref[par] = s
        mb_ref[par] = jnp.max(s, axis=0, keepdims=True)

    def softmax_values(c, par):
        m_old = m_ref[...]
        m_new = jnp.maximum(m_old, mb_ref[par])
        m_ref[...] = m_new
        p = jnp.exp2(s_ref[par] - m_new).astype(BF16)
        pv = _dot(vt_ref[:, pl.ds(chunk_start(c), tk)], p)
        acc_ref[...] = jnp.exp2(m_old - m_new) * acc_ref[...] + pv

    def tick(i, par):
        scores(i, par)
        softmax_values(i - 1, 1 - par)

    scores(0, 0)
    first = 1 + (nk - 1) % unroll
    for i in range(1, first):
        tick(i, i % 2)

    def body(j, carry):
        i = first + unroll * j
        for u in range(unroll):
            tick(i + u, (first + u) % 2)
        return carry

    lax.fori_loop(0, (nk - first) // unroll, body, 0)
    softmax_values(nk - 1, 1)

    inv = 1.0 / acc_ref[GQA_DH:GQA_DH + 1, :]
    out = acc_ref[...] * inv
    heads = [out[0:GQA_DH, g * tq:(g + 1) * tq] for g in range(GQA_GROUP)]
    o_ref[...] = (jnp.concatenate(heads, axis=0).T * z_ref[...]).astype(o_ref.dtype)


def _gqa_attention(cq, k2, vt, cz, batch, seq, tq, tk, unroll):
    m = cq.shape[0]
    nq = seq // tq
    width = GQA_GROUP * GQA_DH
    cols = GQA_GROUP * tq
    qspec = pl.BlockSpec((tq, width), lambda b, n, i: (b * nq + i, n))
    return pl.pallas_call(
        functools.partial(_gqa_kernel, tk=tk, unroll=unroll), out_shape=jax.ShapeDtypeStruct((m, GQA_W), BF16),
        grid=(batch, GQA_KV_HEADS, nq),
        in_specs=[qspec,
                  pl.BlockSpec((None, None, seq, GQA_DH), lambda b, n, i: (b, n, 0, 0)),
                  pl.BlockSpec((None, None, GQA_VROWS, seq), lambda b, n, i: (b, n, 0, 0)),
                  qspec],
        out_specs=qspec,
        scratch_shapes=[pltpu.VMEM((GQA_DH, cols), BF16),
                        pltpu.VMEM((2, tk, cols), F32),
                        pltpu.VMEM((2, 1, cols), F32),
                        pltpu.VMEM((1, cols), F32),
                        pltpu.VMEM((GQA_VROWS, cols), F32)],
        compiler_params=_params(("parallel", "parallel", "arbitrary")),
        name="gqa_attention")(cq, k2, vt, cz)


def _merge_kernel(x_ref, of_ref, ob_ref, rz_ref, gn_ref, onb_ref, ogq_ref,
                  wg_ref, wa_ref, wb_ref, wc_ref, wo_ref, lg_ref, lb_ref, o_ref, *, alpha):
    x = x_ref[...]
    xb = x.astype(BF16)
    ra = of_ref[...] + ob_ref[...]
    parts = []
    for h in range(RET_HEADS):
        t = ra[:, h * RET_DK:(h + 1) * RET_DK]
        mu = jnp.mean(t, axis=-1, keepdims=True)
        d = t - mu
        var = jnp.mean(d * d, axis=-1, keepdims=True)
        parts.append(d * lax.rsqrt(var + LN_EPS))
    o_a = (jnp.concatenate(parts, axis=-1) * gn_ref[...] * rz_ref[...]).astype(BF16)
    y = _sigmoid(_dot(xb, wg_ref[:, 0:D_MODEL])) * _dot(o_a, wa_ref[...])
    y = y + _sigmoid(_dot(xb, wg_ref[:, D_MODEL:2 * D_MODEL])) * _dot(onb_ref[...], wb_ref[...])
    y = y + _sigmoid(_dot(xb, wg_ref[:, 2 * D_MODEL:3 * D_MODEL])) * _dot(ogq_ref[...], wc_ref[...])
    z = alpha * x + _dot(y.astype(BF16), wo_ref[...])
    mu = jnp.mean(z, axis=-1, keepdims=True)
    d = z - mu
    var = jnp.mean(d * d, axis=-1, keepdims=True)
    o_ref[...] = d * lax.rsqrt(var + LN_EPS) * lg_ref[...] + lb_ref[...]


def _merge(x2, o_f, o_b, rz, gn, o_nb, o_gq, wg, wa, wb, wc, wo, lg, lb, alpha, tm):
    m = x2.shape[0]
    return pl.pallas_call(
        functools.partial(_merge_kernel, alpha=alpha), out_shape=jax.ShapeDtypeStruct((m, D_MODEL), F32),
        grid=(m // tm,),
        in_specs=[_row_spec(tm, D_MODEL), _row_spec(tm, RET_W), _row_spec(tm, RET_W), _row_spec(tm, RET_W),
                  _const_spec(gn.shape), _row_spec(tm, NA_W), _row_spec(tm, GQA_W),
                  _const_spec(wg.shape), _const_spec(wa.shape), _const_spec(wb.shape), _const_spec(wc.shape),
                  _const_spec(wo.shape), _const_spec(lg.shape), _const_spec(lb.shape)],
        out_specs=_row_spec(tm, D_MODEL),
        compiler_params=_params(("parallel",)), name="merge")(
            x2, o_f, o_b, rz, gn, o_nb, o_gq, wg, wa, wb, wc, wo, lg, lb)


PROJ_TM = 512
MERGE_TM = 256
RET_CHUNKS_PER_STEP = 8
GQA_TQ = 256
GQA_TK = 512
GQA_UNROLL = 6


def _rope_tables(seq):
    t = jnp.arange(seq)
    tf = t.astype(F32)
    row = (t // GRID_W).astype(F32)
    col = (t % GRID_W).astype(F32)
    inv_ret = 1.0 / (ROPE_THETA ** jnp.linspace(0.0, 1.0, RET_DK // 2, dtype=F32))
    ang_ret = tf[:, None] * inv_ret[None, :]
    n_ax = GQA_DH // 4
    inv_ax = ROPE_THETA ** (-jnp.arange(n_ax, dtype=F32) / n_ax)
    ang_ax = jnp.concatenate([row[:, None] * inv_ax[None, :], col[:, None] * inv_ax[None, :]], -1)

    def full(ang, reps):
        c, s = jnp.cos(ang), jnp.sin(ang)
        return jnp.tile(jnp.concatenate([c, c], -1), (1, reps)), jnp.tile(jnp.concatenate([-s, s], -1), (1, reps))

    return full(ang_ret, 1), full(ang_ax, LANES // GQA_DH)


def kernel(x, w_in, ret_theta_fwd, ret_theta_bwd, ret_gn_gain, na_rpb, gqa_q_norm, gqa_k_norm,
           w_branch_a, w_branch_b, w_branch_c, w_out, ln_gain, ln_bias):
    batch, seq, d_model = x.shape
    depth = w_in.shape[0]
    assert d_model == D_MODEL and w_in.shape[-1] == D_IN and seq % (GRID_W * NA_QROWS) == 0
    alpha = (2.0 * depth) ** 0.25
    m = batch * seq
    (ret_cos, ret_sin), (ax_cos, ax_sin) = _rope_tables(seq)
    gidx = np.arange(GQA_W) // GQA_DH
    gmat = jnp.asarray(gidx[:, None] == gidx[None, :], BF16)

    x2 = x.reshape(m, D_MODEL)
    for l in range(depth):
        w = w_in[l].astype(BF16)
        rq, rk, rv, rz = _proj_ret(x2, w[:, OFF_RET:OFF_NA], ret_cos, ret_sin, seq, PROJ_TM)
        nq, nk, nv, nz = _proj_na(x2, w[:, OFF_NA:OFF_GQA], PROJ_TM)
        qn = jnp.tile(gqa_q_norm[l], GQA_HEADS)[None, :]
        kn = jnp.tile(gqa_k_norm[l], GQA_KV_HEADS)[None, :]
        cq, ck, cvt, cz = _proj_gqa(x2, w[:, OFF_GQA:OFF_GATES], ax_cos, ax_sin, gmat, qn, kn, seq, PROJ_TM)

        decay = jnp.stack([jax.nn.log_sigmoid(ret_theta_fwd[l].astype(F32)),
                           jax.nn.log_sigmoid(ret_theta_bwd[l].astype(F32))], axis=1)
        decay = jnp.broadcast_to(jnp.pad(decay, ((0, 0), (0, 6)))[:, :, None], (RET_HEADS, 8, LANES))
        o_f, o_b = _retention(rq, rk, rv, decay, batch, seq, RET_CHUNKS_PER_STEP)

        bias = _na_bias_tables(na_rpb[l], seq // GRID_W)
        o_nb = _na_attention(nq, nk, nv, nz, bias, batch, seq)

        o_gq = _gqa_attention(cq, ck, cvt, cz, batch, seq, GQA_TQ, GQA_TK, GQA_UNROLL)

        x2 = _merge(x2, o_f, o_b, rz, ret_gn_gain[l][None, :], o_nb, o_gq,
                    w[:, OFF_GATES:D_IN], w_branch_a[l].astype(BF16), w_branch_b[l].astype(BF16),
                    w_branch_c[l].astype(BF16), w_out[l].astype(BF16),
                    ln_gain[l][None, :], ln_bias[l][None, :], alpha, MERGE_TM)
    return x2.reshape(batch, seq, D_MODEL)
```

```python
import functools
import math

import numpy as np
import jax
import jax.numpy as jnp
from jax import lax
from jax.experimental import pallas as pl
from jax.experimental.pallas import tpu as pltpu

D_MODEL = 1024
GRID_W = 64
RET_HEADS = 4
RET_DK = 128
RET_CHUNK = 128
NA_HEADS = 8
NA_DH = 64
NA_KH = 8
NA_KW = 16
GQA_HEADS = 8
GQA_KV_HEADS = 2
GQA_DH = 64
ROPE_THETA = 10000.0
RMS_EPS = 1e-6
LN_EPS = 1e-5

RET_W = RET_HEADS * RET_DK
NA_W = NA_HEADS * NA_DH
GQA_W = GQA_HEADS * GQA_DH
GQA_KV_W = GQA_KV_HEADS * GQA_DH

OFF_RET = 0
OFF_NA = 4 * RET_W
OFF_GQA = OFF_NA + 4 * NA_W
OFF_GATES = OFF_GQA + 2 * GQA_W + 2 * GQA_KV_W
D_IN = OFF_GATES + 3 * D_MODEL

LOG2E = math.log2(math.e)
LANES = 128
VMEM_LIMIT = 56 * 1024 * 1024

BF16 = jnp.bfloat16
F32 = jnp.float32


def _dot(a, b):
    return jnp.dot(a, b, preferred_element_type=F32)


def _dot_nt(a, b):
    return lax.dot_general(a, b, (((1,), (1,)), ((), ())), preferred_element_type=F32)


def _dot_tn(a, b):
    return lax.dot_general(a, b, (((0,), (0,)), ((), ())), preferred_element_type=F32)


def _silu(z):
    return z / (1.0 + jnp.exp(-z))


def _sigmoid(z):
    return 1.0 / (1.0 + jnp.exp(-z))


def _params(sem):
    return pltpu.CompilerParams(dimension_semantics=sem, vmem_limit_bytes=VMEM_LIMIT)


def _proj_ret_kernel(x_ref, w_ref, cos_ref, sin_ref, q_ref, k_ref, v_ref, z_ref):
    xb = x_ref[...].astype(BF16)
    cos = cos_ref[...]
    sin = sin_ref[...]
    scale = RET_DK ** -0.5
    yq = _dot(xb, w_ref[:, 0:RET_W])
    yk = _dot(xb, w_ref[:, RET_W:2 * RET_W])
    for h in range(RET_HEADS):
        sl = slice(h * RET_DK, (h + 1) * RET_DK)
        t = yq[:, sl]
        q_ref[:, sl] = ((t * cos + pltpu.roll(t, RET_DK // 2, 1) * sin) * scale).astype(BF16)
        t = yk[:, sl]
        k_ref[:, sl] = (t * cos + pltpu.roll(t, RET_DK // 2, 1) * sin).astype(BF16)
    v_ref[...] = _dot(xb, w_ref[:, 2 * RET_W:3 * RET_W]).astype(BF16)
    z_ref[...] = _silu(_dot(xb, w_ref[:, 3 * RET_W:4 * RET_W]))


def _proj_na_kernel(x_ref, w_ref, q_ref, k_ref, v_ref, z_ref):
    xb = x_ref[...].astype(BF16)
    q_ref[...] = (_dot(xb, w_ref[:, 0:NA_W]) * (LOG2E * NA_DH ** -0.5)).astype(BF16)
    k_ref[...] = _dot(xb, w_ref[:, NA_W:2 * NA_W]).astype(BF16)
    v_ref[...] = _dot(xb, w_ref[:, 2 * NA_W:3 * NA_W]).astype(BF16)
    z_ref[...] = _silu(_dot(xb, w_ref[:, 3 * NA_W:4 * NA_W]))


def _group_mean_sq(y, g_ref, width):
    sq = y * y
    hi = sq.astype(BF16)
    lo = (sq - hi.astype(F32)).astype(BF16)
    g = g_ref[0:width, 0:width]
    return (_dot(hi, g) + _dot(lo, g)) * (1.0 / GQA_DH)


def _rope64(t, cos, sin, first_half):
    swapped = jnp.where(first_half, pltpu.roll(t, LANES - GQA_DH // 2, 1), pltpu.roll(t, GQA_DH // 2, 1))
    return t * cos + swapped * sin


def _proj_gqa_kernel(x_ref, w_ref, cos_ref, sin_ref, g_ref, qn_ref, kn_ref,
                     q_ref, k_ref, vt_ref, z_ref):
    xb = x_ref[...].astype(BF16)
    cos = cos_ref[...]
    sin = sin_ref[...]
    lane = lax.broadcasted_iota(jnp.int32, cos.shape, 1)
    first_half = (lane % GQA_DH) < (GQA_DH // 2)
    yq = _dot(xb, w_ref[:, 0:GQA_W])
    qn = yq * lax.rsqrt(_group_mean_sq(yq, g_ref, GQA_W) + RMS_EPS) * qn_ref[...]
    for j in range(GQA_W // LANES):
        sl = slice(j * LANES, (j + 1) * LANES)
        q_ref[:, sl] = (_rope64(qn[:, sl], cos, sin, first_half) * (LOG2E * GQA_DH ** -0.5)).astype(BF16)
    yk = _dot(xb, w_ref[:, GQA_W:GQA_W + GQA_KV_W])
    kn = yk * lax.rsqrt(_group_mean_sq(yk, g_ref, GQA_KV_W) + RMS_EPS) * kn_ref[...]
    kr = _rope64(kn, cos, sin, first_half).astype(BF16)
    v_t = _dot(xb, w_ref[:, GQA_W + GQA_KV_W:GQA_W + 2 * GQA_KV_W]).T
    tm = v_t.shape[1]
    pad_row = lax.broadcasted_iota(jnp.int32, (GQA_VROWS - GQA_DH, tm), 0)
    ones_then_zeros = jnp.where(pad_row == 0, 1.0, 0.0)
    for n in range(GQA_KV_HEADS):
        k_ref[n] = kr[:, n * GQA_DH:(n + 1) * GQA_DH]
        vt_ref[n] = jnp.concatenate([v_t[n * GQA_DH:(n + 1) * GQA_DH, :], ones_then_zeros], axis=0).astype(BF16)
    z_ref[...] = _silu(_dot(xb, w_ref[:, GQA_W + 2 * GQA_KV_W:2 * GQA_W + 2 * GQA_KV_W]))


def _row_spec(tm, width):
    return pl.BlockSpec((tm, width), lambda i: (i, 0))


def _const_spec(shape):
    return pl.BlockSpec(shape, lambda i: (0,) * len(shape))


def _table_spec(tm, seq):
    nblk = seq // tm
    return pl.BlockSpec((tm, LANES), lambda i: (i % nblk, 0))


def _proj_ret(x2, w, cos, sin, seq, tm):
    m = x2.shape[0]
    out = [jax.ShapeDtypeStruct((m, RET_W), BF16)] * 3 + [jax.ShapeDtypeStruct((m, RET_W), F32)]
    return pl.pallas_call(
        _proj_ret_kernel, out_shape=out, grid=(m // tm,),
        in_specs=[_row_spec(tm, D_MODEL), _const_spec(w.shape), _table_spec(tm, seq), _table_spec(tm, seq)],
        out_specs=[_row_spec(tm, RET_W)] * 4,
        compiler_params=_params(("parallel",)), name="proj_ret")(x2, w, cos, sin)


def _proj_na(x2, w, tm):
    m = x2.shape[0]
    out = [jax.ShapeDtypeStruct((m, NA_W), BF16)] * 3 + [jax.ShapeDtypeStruct((m, NA_W), F32)]
    return pl.pallas_call(
        _proj_na_kernel, out_shape=out, grid=(m // tm,),
        in_specs=[_row_spec(tm, D_MODEL), _const_spec(w.shape)],
        out_specs=[_row_spec(tm, NA_W)] * 4,
        compiler_params=_params(("parallel",)), name="proj_na")(x2, w)


def _proj_gqa(x2, w, cos, sin, gmat, qn, kn, seq, tm):
    m = x2.shape[0]
    batch, nblk = m // seq, seq // tm
    out = [jax.ShapeDtypeStruct((m, GQA_W), BF16),
           jax.ShapeDtypeStruct((batch, GQA_KV_HEADS, seq, GQA_DH), BF16),
           jax.ShapeDtypeStruct((batch, GQA_KV_HEADS, GQA_VROWS, seq), BF16),
           jax.ShapeDtypeStruct((m, GQA_W), F32)]
    k_spec = pl.BlockSpec((None, GQA_KV_HEADS, tm, GQA_DH), lambda i: (i // nblk, 0, i % nblk, 0))
    vt_spec = pl.BlockSpec((None, GQA_KV_HEADS, GQA_VROWS, tm), lambda i: (i // nblk, 0, 0, i % nblk))
    return pl.pallas_call(
        _proj_gqa_kernel, out_shape=out, grid=(m // tm,),
        in_specs=[_row_spec(tm, D_MODEL), _const_spec(w.shape), _table_spec(tm, seq), _table_spec(tm, seq),
                  _const_spec(gmat.shape), _const_spec(qn.shape), _const_spec(kn.shape)],
        out_specs=[_row_spec(tm, GQA_W), k_spec, vt_spec, _row_spec(tm, GQA_W)],
        compiler_params=_params(("parallel",)), name="proj_gqa")(x2, w, cos, sin, gmat, qn, kn)


def _retention_kernel(dec_ref, qf_ref, kf_ref, vf_ref, qb_ref, kb_ref, vb_ref,
                      of_ref, ob_ref, sf_ref, sb_ref, *, chunks):
    c = RET_CHUNK

    @pl.when(pl.program_id(2) == 0)
    def _():
        sf_ref[...] = jnp.zeros_like(sf_ref)
        sb_ref[...] = jnp.zeros_like(sb_ref)

    lf = dec_ref[0:1, :]
    lb = dec_ref[1:2, :]
    row = lax.broadcasted_iota(jnp.int32, (c, c), 0).astype(F32)
    col = lax.broadcasted_iota(jnp.int32, (c, c), 1).astype(F32)
    diff = row - col
    d_intra = jnp.where(diff >= 0, jnp.exp(jnp.maximum(diff, 0.0) * lf), jnp.exp(jnp.maximum(-diff, 0.0) * lb))
    w_kf = jnp.exp((c - 1.0 - row) * lf)
    w_qf = jnp.exp((row + 1.0) * lf)
    w_kb = jnp.exp(row * lb)
    w_qb = jnp.exp((c - row) * lb)
    chunk_f = jnp.exp(c * lf)
    chunk_b = jnp.exp(c * lb)

    blocks = [slice(i * c, (i + 1) * c) for i in range(chunks)]

    kv_f = [_dot_tn((kf_ref[r, :].astype(F32) * w_kf).astype(BF16), vf_ref[r, :]) for r in blocks]
    kv_b = [_dot_tn((kb_ref[r, :].astype(F32) * w_kb).astype(BF16), vb_ref[r, :]) for r in blocks]
    intra = [_dot((_dot_nt(qf_ref[r, :], kf_ref[r, :]) * d_intra).astype(BF16), vf_ref[r, :]) for r in blocks]

    s_f = sf_ref[...]
    states_f = []
    for i in range(chunks):
        states_f.append(s_f)
        s_f = chunk_f * s_f + kv_f[i]
    sf_ref[...] = s_f
    s_b = sb_ref[...]
    states_b = [None] * chunks
    for i in reversed(range(chunks)):
        states_b[i] = s_b
        s_b = chunk_b * s_b + kv_b[i]
    sb_ref[...] = s_b

    for i, r in enumerate(blocks):
        qw = (qf_ref[r, :].astype(F32) * w_qf).astype(BF16)
        of_ref[r, :] = intra[i] + _dot(qw, states_f[i].astype(BF16))
        qw = (qb_ref[r, :].astype(F32) * w_qb).astype(BF16)
        ob_ref[r, :] = _dot(qw, states_b[i].astype(BF16))


def _retention(rq, rk, rv, decay, batch, seq, chunks):
    m = rq.shape[0]
    rows = chunks * RET_CHUNK
    ng = seq // rows
    fwd = pl.BlockSpec((rows, RET_DK), lambda b, h, g: (b * ng + g, h))
    bwd = pl.BlockSpec((rows, RET_DK), lambda b, h, g: (b * ng + ng - 1 - g, h))
    dec = pl.BlockSpec((None, 8, LANES), lambda b, h, g: (h, 0, 0))
    out = [jax.ShapeDtypeStruct((m, RET_W), F32)] * 2
    return pl.pallas_call(
        functools.partial(_retention_kernel, chunks=chunks), out_shape=out,
        grid=(batch, RET_HEADS, ng),
        in_specs=[dec, fwd, fwd, fwd, bwd, bwd, bwd],
        out_specs=[fwd, bwd],
        scratch_shapes=[pltpu.VMEM((RET_DK, RET_DK), F32), pltpu.VMEM((RET_DK, RET_DK), F32)],
        compiler_params=_params(("parallel", "parallel", "arbitrary")),
        name="retention")(decay, rq, rk, rv, rq, rk, rv)


NA_QROWS = 4
NA_KROWS = 3 * NA_QROWS
NA_TQ = NA_QROWS * GRID_W
NA_TK = NA_KROWS * GRID_W
NA_PAIRS = 4
NA_MASKED = -1e30


def _na_kernel(q_ref, k0_ref, k1_ref, k2_ref, v0_ref, v1_ref, v2_ref, bias_ref, z_ref, o_ref,
               s_ref, p_ref, mx_ref, den_ref):
    lane = lax.broadcasted_iota(jnp.int32, (NA_TK, LANES), 1)
    zero = jnp.zeros((NA_TK, LANES), BF16)
    units = 2 * (q_ref.shape[1] // LANES)

    def masked(refs, u):
        cs = slice((u // 2) * LANES, (u // 2 + 1) * LANES)
        x = jnp.concatenate([r[:, cs] for r in refs], axis=0)
        own = (lane < NA_DH) if u % 2 == 0 else (lane >= NA_DH)
        return jnp.where(own, x, zero)

    def scores(u):
        cs = slice((u // 2) * LANES, (u // 2 + 1) * LANES)
        s = _dot_nt(q_ref[:, cs], masked((k0_ref, k1_ref, k2_ref), u)) + bias_ref[u]
        s_ref[u] = s
        mx_ref[u] = jnp.broadcast_to(jnp.max(s, axis=-1, keepdims=True), (NA_TQ, LANES))

    def softmax(u):
        mx = mx_ref[u]
        den = None
        for j in range(NA_TK // LANES):
            cs = slice(j * LANES, (j + 1) * LANES)
            p = jnp.exp2(s_ref[u, :, cs] - mx)
            p_ref[u, :, cs] = p.astype(BF16)
            den = p if den is None else den + p
        den_ref[u] = jnp.broadcast_to(jnp.sum(den, axis=-1, keepdims=True), (NA_TQ, LANES))

    outs = {}

    def values(u):
        o = _dot(p_ref[u], masked((v0_ref, v1_ref, v2_ref), u)) / den_ref[u]
        if u % 2 == 0:
            outs[u // 2] = o
        else:
            cs = slice((u // 2) * LANES, (u // 2 + 1) * LANES)
            o_ref[:, cs] = ((outs.pop(u // 2) + o) * z_ref[:, cs]).astype(o_ref.dtype)

    for t in range(units + 2):
        if t < units:
            scores(t)
        if 1 <= t <= units:
            softmax(t - 1)
        if t >= 2:
            values(t - 2)


def _na_bias_tables(rpb, rows):
    nblk = rows // NA_QROWS
    nrow, ncol = 2 * NA_KH - 1, 2 * NA_KW - 1
    col = np.arange(GRID_W)
    c_start = np.clip(col - NA_KW // 2, 0, GRID_W - NA_KW)
    col_ok = (col[None, :] >= c_start[:, None]) & (col[None, :] < c_start[:, None] + NA_KW)
    cidx = np.clip(col[None, :] - col[:, None] + NA_KW - 1, 0, ncol - 1)
    col_sel = (cidx[:, :, None] == np.arange(ncol)).astype(np.float32)
    row_sel = np.zeros((3, NA_QROWS, NA_KROWS, nrow + 1), np.float32)
    for cls, blk in enumerate((0, 1, nblk - 1)):
        first_kblk = min(max(blk - 1, 0), nblk - 3)
        qr = blk * NA_QROWS + np.arange(NA_QROWS)
        kr = first_kblk * NA_QROWS + np.arange(NA_KROWS)
        r_start = np.clip(qr - NA_KH // 2, 0, rows - NA_KH)
        ok = (kr[None, :] >= r_start[:, None]) & (kr[None, :] < r_start[:, None] + NA_KH)
        ridx = np.where(ok, kr[None, :] - qr[:, None] + NA_KH - 1, nrow)
        row_sel[cls] = (ridx[:, :, None] == np.arange(nrow + 1))
    by_col = jnp.einsum('har,qkr->haqk', rpb.astype(F32), col_sel, precision=lax.Precision.HIGHEST)
    by_col = jnp.where(col_ok[None, None], by_col * LOG2E, NA_MASKED)
    by_col = jnp.concatenate([by_col, jnp.full((NA_HEADS, 1, GRID_W, GRID_W), NA_MASKED, F32)], axis=1)
    table = jnp.einsum('cija,haqk->chiqjk', row_sel, by_col, precision=lax.Precision.HIGHEST)
    return table.reshape(3, NA_HEADS, NA_TQ, NA_TK)


def _na_attention(nq, nk, nv, nz, bias, batch, seq):
    m = nq.shape[0]
    rows = seq // GRID_W
    nblk = rows // NA_QROWS
    groups = NA_HEADS // (2 * NA_PAIRS)
    width = NA_PAIRS * LANES

    def kblk(j):
        return lambda b, hp, i: (b * nblk + jnp.clip(i - 1, 0, nblk - 3) + j, hp)

    def bias_idx(b, hp, i):
        cls = jnp.where(i == 0, 0, jnp.where(i == nblk - 1, 2, 1))
        return (cls, hp, 0, 0)

    qspec = pl.BlockSpec((NA_TQ, width), lambda b, hp, i: (b * nblk + i, hp))
    kspecs = [pl.BlockSpec((NA_TQ, width), kblk(j)) for j in range(3)]
    return pl.pallas_call(
        _na_kernel, out_shape=jax.ShapeDtypeStruct((m, NA_W), BF16),
        grid=(batch, groups, nblk),
        in_specs=[qspec] + kspecs + kspecs + [pl.BlockSpec((None, 2 * NA_PAIRS, NA_TQ, NA_TK), bias_idx), qspec],
        out_specs=qspec,
        scratch_shapes=[pltpu.VMEM((2 * NA_PAIRS, NA_TQ, NA_TK), F32),
                        pltpu.VMEM((2 * NA_PAIRS, NA_TQ, NA_TK), BF16),
                        pltpu.VMEM((2 * NA_PAIRS, NA_TQ, LANES), F32),
                        pltpu.VMEM((2 * NA_PAIRS, NA_TQ, LANES), F32)],
        compiler_params=_params(("parallel", "parallel", "arbitrary")),
        name="na_attention")(nq, nk, nk, nk, nv, nv, nv, bias, nz)


GQA_GROUP = GQA_HEADS // GQA_KV_HEADS
GQA_VROWS = 128


def _gqa_kernel(q_ref, k_ref, vt_ref, z_ref, o_ref,
                qs_ref, s_ref, mb_ref, m_ref, acc_ref, *, tk, unroll):
    tq = q_ref.shape[0]
    nk = k_ref.shape[0] // tk
    assert nk % 2 == 0 and unroll % 2 == 0 and nk >= 1 + unroll

    def chunk_start(c):
        return c * tk if isinstance(c, int) else pl.multiple_of(c * tk, tk)
    q_t = q_ref[...].astype(F32).T.astype(BF16)
    for g in range(GQA_GROUP):
        qs_ref[:, g * tq:(g + 1) * tq] = q_t[g * GQA_DH:(g + 1) * GQA_DH, :]
    m_ref[...] = jnp.full(m_ref.shape, -jnp.inf, F32)
    acc_ref[...] = jnp.zeros_like(acc_ref)

    def scores(c, par):
        s = _dot(k_ref[pl.ds(chunk_start(c), tk), :], qs_ref[...])
        s_ref[par] = s
        mb_ref[par] = jnp.max(s, axis=0, keepdims=True)

    def softmax_values(c, par):
        m_old = m_ref[...]
        m_new = jnp.maximum(m_old, mb_ref[par])
        m_ref[...] = m_new
        p = jnp.exp2(s_ref[par] - m_new).astype(BF16)
        pv = _dot(vt_ref[:, pl.ds(chunk_start(c), tk)], p)
        acc_ref[...] = jnp.exp2(m_old - m_new) * acc_ref[...] + pv

    def tick(i, par):
        scores(i, par)
        softmax_values(i - 1, 1 - par)

    scores(0, 0)
    first = 1 + (nk - 1) % unroll
    for i in range(1, first):
        tick(i, i % 2)

    def body(j, carry):
        i = first + unroll * j
        for u in range(unroll):
            tick(i + u, (first + u) % 2)
        return carry

    lax.fori_loop(0, (nk - first) // unroll, body, 0)
    softmax_values(nk - 1, 1)

    inv = 1.0 / acc_ref[GQA_DH:GQA_DH + 1, :]
    out = acc_ref[...] * inv
    heads = [out[0:GQA_DH, g * tq:(g + 1) * tq] for g in range(GQA_GROUP)]
    o_ref[...] = (jnp.concatenate(heads, axis=0).T * z_ref[...]).astype(o_ref.dtype)


def _gqa_attention(cq, k2, vt, cz, batch, seq, tq, tk, unroll):
    m = cq.shape[0]
    nq = seq // tq
    width = GQA_GROUP * GQA_DH
    cols = GQA_GROUP * tq
    qspec = pl.BlockSpec((tq, width), lambda b, n, i: (b * nq + i, n))
    return pl.pallas_call(
        functools.partial(_gqa_kernel, tk=tk, unroll=unroll), out_shape=jax.ShapeDtypeStruct((m, GQA_W), BF16),
        grid=(batch, GQA_KV_HEADS, nq),
        in_specs=[qspec,
                  pl.BlockSpec((None, None, seq, GQA_DH), lambda b, n, i: (b, n, 0, 0)),
                  pl.BlockSpec((None, None, GQA_VROWS, seq), lambda b, n, i: (b, n, 0, 0)),
                  qspec],
        out_specs=qspec,
        scratch_shapes=[pltpu.VMEM((GQA_DH, cols), BF16),
                        pltpu.VMEM((2, tk, cols), F32),
                        pltpu.VMEM((2, 1, cols), F32),
                        pltpu.VMEM((1, cols), F32),
                        pltpu.VMEM((GQA_VROWS, cols), F32)],
        compiler_params=_params(("parallel", "parallel", "arbitrary")),
        name="gqa_attention")(cq, k2, vt, cz)


def _merge_kernel(x_ref, of_ref, ob_ref, rz_ref, gn_ref, onb_ref, ogq_ref,
                  wg_ref, wa_ref, wb_ref, wc_ref, wo_ref, lg_ref, lb_ref, o_ref, *, alpha):
    x = x_ref[...]
    xb = x.astype(BF16)
    ra = of_ref[...] + ob_ref[...]
    parts = []
    for h in range(RET_HEADS):
        t = ra[:, h * RET_DK:(h + 1) * RET_DK]
        mu = jnp.mean(t, axis=-1, keepdims=True)
        d = t - mu
        var = jnp.mean(d * d, axis=-1, keepdims=True)
        parts.append(d * lax.rsqrt(var + LN_EPS))
    o_a = (jnp.concatenate(parts, axis=-1) * gn_ref[...] * rz_ref[...]).astype(BF16)
    y = _sigmoid(_dot(xb, wg_ref[:, 0:D_MODEL])) * _dot(o_a, wa_ref[...])
    y = y + _sigmoid(_dot(xb, wg_ref[:, D_MODEL:2 * D_MODEL])) * _dot(onb_ref[...], wb_ref[...])
    y = y + _sigmoid(_dot(xb, wg_ref[:, 2 * D_MODEL:3 * D_MODEL])) * _dot(ogq_ref[...], wc_ref[...])
    z = alpha * x + _dot(y.astype(BF16), wo_ref[...])
    mu = jnp.mean(z, axis=-1, keepdims=True)
    d = z - mu
    var = jnp.mean(d * d, axis=-1, keepdims=True)
    o_ref[...] = d * lax.rsqrt(var + LN_EPS) * lg_ref[...] + lb_ref[...]


def _merge(x2, o_f, o_b, rz, gn, o_nb, o_gq, wg, wa, wb, wc, wo, lg, lb, alpha, tm):
    m = x2.shape[0]
    return pl.pallas_call(
        functools.partial(_merge_kernel, alpha=alpha), out_shape=jax.ShapeDtypeStruct((m, D_MODEL), F32),
        grid=(m // tm,),
        in_specs=[_row_spec(tm, D_MODEL), _row_spec(tm, RET_W), _row_spec(tm, RET_W), _row_spec(tm, RET_W),
                  _const_spec(gn.shape), _row_spec(tm, NA_W), _row_spec(tm, GQA_W),
                  _const_spec(wg.shape), _const_spec(wa.shape), _const_spec(wb.shape), _const_spec(wc.shape),
                  _const_spec(wo.shape), _const_spec(lg.shape), _const_spec(lb.shape)],
        out_specs=_row_spec(tm, D_MODEL),
        compiler_params=_params(("parallel",)), name="merge")(
            x2, o_f, o_b, rz, gn, o_nb, o_gq, wg, wa, wb, wc, wo, lg, lb)


PROJ_TM = 1024
MERGE_TM = 512
RET_CHUNKS_PER_STEP = 8
GQA_TQ = 256
GQA_TK = 512
GQA_UNROLL = 6


def _rope_tables(seq):
    t = jnp.arange(seq)
    tf = t.astype(F32)
    row = (t // GRID_W).astype(F32)
    col = (t % GRID_W).astype(F32)
    inv_ret = 1.0 / (ROPE_THETA ** jnp.linspace(0.0, 1.0, RET_DK // 2, dtype=F32))
    ang_ret = tf[:, None] * inv_ret[None, :]
    n_ax = GQA_DH // 4
    inv_ax = ROPE_THETA ** (-jnp.arange(n_ax, dtype=F32) / n_ax)
    ang_ax = jnp.concatenate([row[:, None] * inv_ax[None, :], col[:, None] * inv_ax[None, :]], -1)

    def full(ang, reps):
        c, s = jnp.cos(ang), jnp.sin(ang)
        return jnp.tile(jnp.concatenate([c, c], -1), (1, reps)), jnp.tile(jnp.concatenate([-s, s], -1), (1, reps))

    return full(ang_ret, 1), full(ang_ax, LANES // GQA_DH)


def kernel(x, w_in, ret_theta_fwd, ret_theta_bwd, ret_gn_gain, na_rpb, gqa_q_norm, gqa_k_norm,
           w_branch_a, w_branch_b, w_branch_c, w_out, ln_gain, ln_bias):
    batch, seq, d_model = x.shape
    depth = w_in.shape[0]
    assert d_model == D_MODEL and w_in.shape[-1] == D_IN and seq % (GRID_W * NA_QROWS) == 0
    alpha = (2.0 * depth) ** 0.25
    m = batch * seq
    (ret_cos, ret_sin), (ax_cos, ax_sin) = _rope_tables(seq)
    gidx = np.arange(GQA_W) // GQA_DH
    gmat = jnp.asarray(gidx[:, None] == gidx[None, :], BF16)

    x2 = x.reshape(m, D_MODEL)
    for l in range(depth):
        w = w_in[l].astype(BF16)
        rq, rk, rv, rz = _proj_ret(x2, w[:, OFF_RET:OFF_NA], ret_cos, ret_sin, seq, PROJ_TM)
        nq, nk, nv, nz = _proj_na(x2, w[:, OFF_NA:OFF_GQA], PROJ_TM)
        qn = jnp.tile(gqa_q_norm[l], GQA_HEADS)[None, :]
        kn = jnp.tile(gqa_k_norm[l], GQA_KV_HEADS)[None, :]
        cq, ck, cvt, cz = _proj_gqa(x2, w[:, OFF_GQA:OFF_GATES], ax_cos, ax_sin, gmat, qn, kn, seq, PROJ_TM)

        decay = jnp.stack([jax.nn.log_sigmoid(ret_theta_fwd[l].astype(F32)),
                           jax.nn.log_sigmoid(ret_theta_bwd[l].astype(F32))], axis=1)
        decay = jnp.broadcast_to(jnp.pad(decay, ((0, 0), (0, 6)))[:, :, None], (RET_HEADS, 8, LANES))
        o_f, o_b = _retention(rq, rk, rv, decay, batch, seq, RET_CHUNKS_PER_STEP)

        bias = _na_bias_tables(na_rpb[l], seq // GRID_W)
        o_nb = _na_attention(nq, nk, nv, nz, bias, batch, seq)

        o_gq = _gqa_attention(cq, ck, cvt, cz, batch, seq, GQA_TQ, GQA_TK, GQA_UNROLL)

        x2 = _merge(x2, o_f, o_b, rz, ret_gn_gain[l][None, :], o_nb, o_gq,
                    w[:, OFF_GATES:D_IN], w_branch_a[l].astype(BF16), w_branch_b[l].astype(BF16),
                    w_branch_c[l].astype(BF16), w_out[l].astype(BF16),
                    ln_gain[l][None, :], ln_bias[l][None, :], alpha, MERGE_TM)
    return x2.reshape(batch, seq, D_MODEL)
```

```python
import functools
import math

import numpy as np
import jax
import jax.numpy as jnp
from jax import lax
from jax.experimental import pallas as pl
from jax.experimental.pallas import tpu as pltpu

D_MODEL = 1024
GRID_W = 64
RET_HEADS = 4
RET_DK = 128
RET_CHUNK = 128
NA_HEADS = 8
NA_DH = 64
NA_KH = 8
NA_KW = 16
GQA_HEADS = 8
GQA_KV_HEADS = 2
GQA_DH = 64
ROPE_THETA = 10000.0
RMS_EPS = 1e-6
LN_EPS = 1e-5

RET_W = RET_HEADS * RET_DK
NA_W = NA_HEADS * NA_DH
GQA_W = GQA_HEADS * GQA_DH
GQA_KV_W = GQA_KV_HEADS * GQA_DH

OFF_RET = 0
OFF_NA = 4 * RET_W
OFF_GQA = OFF_NA + 4 * NA_W
OFF_GATES = OFF_GQA + 2 * GQA_W + 2 * GQA_KV_W
D_IN = OFF_GATES + 3 * D_MODEL

LOG2E = math.log2(math.e)
LANES = 128
VMEM_LIMIT = 56 * 1024 * 1024

BF16 = jnp.bfloat16
F32 = jnp.float32


def _dot(a, b):
    return jnp.dot(a, b, preferred_element_type=F32)


def _dot_nt(a, b):
    return lax.dot_general(a, b, (((1,), (1,)), ((), ())), preferred_element_type=F32)


def _dot_tn(a, b):
    return lax.dot_general(a, b, (((0,), (0,)), ((), ())), preferred_element_type=F32)


def _silu(z):
    return z / (1.0 + jnp.exp(-z))


def _sigmoid(z):
    return 1.0 / (1.0 + jnp.exp(-z))


def _params(sem):
    return pltpu.CompilerParams(dimension_semantics=sem, vmem_limit_bytes=VMEM_LIMIT)


def _proj_ret_kernel(x_ref, w_ref, cos_ref, sin_ref, q_ref, k_ref, v_ref, z_ref):
    xb = x_ref[...].astype(BF16)
    cos = cos_ref[...]
    sin = sin_ref[...]
    scale = RET_DK ** -0.5
    yq = _dot(xb, w_ref[:, 0:RET_W])
    yk = _dot(xb, w_ref[:, RET_W:2 * RET_W])
    for h in range(RET_HEADS):
        sl = slice(h * RET_DK, (h + 1) * RET_DK)
        t = yq[:, sl]
        q_ref[:, sl] = ((t * cos + pltpu.roll(t, RET_DK // 2, 1) * sin) * scale).astype(BF16)
        t = yk[:, sl]
        k_ref[:, sl] = (t * cos + pltpu.roll(t, RET_DK // 2, 1) * sin).astype(BF16)
    v_ref[...] = _dot(xb, w_ref[:, 2 * RET_W:3 * RET_W]).astype(BF16)
    z_ref[...] = _silu(_dot(xb, w_ref[:, 3 * RET_W:4 * RET_W]))


def _proj_na_kernel(x_ref, w_ref, q_ref, k_ref, v_ref, z_ref):
    xb = x_ref[...].astype(BF16)
    q_ref[...] = (_dot(xb, w_ref[:, 0:NA_W]) * (LOG2E * NA_DH ** -0.5)).astype(BF16)
    k_ref[...] = _dot(xb, w_ref[:, NA_W:2 * NA_W]).astype(BF16)
    v_ref[...] = _dot(xb, w_ref[:, 2 * NA_W:3 * NA_W]).astype(BF16)
    z_ref[...] = _silu(_dot(xb, w_ref[:, 3 * NA_W:4 * NA_W]))


def _group_mean_sq(y, g_ref, width):
    sq = y * y
    hi = sq.astype(BF16)
    lo = (sq - hi.astype(F32)).astype(BF16)
    g = g_ref[0:width, 0:width]
    return (_dot(hi, g) + _dot(lo, g)) * (1.0 / GQA_DH)


def _rope64(t, cos, sin, first_half):
    swapped = jnp.where(first_half, pltpu.roll(t, LANES - GQA_DH // 2, 1), pltpu.roll(t, GQA_DH // 2, 1))
    return t * cos + swapped * sin


def _proj_gqa_kernel(x_ref, w_ref, cos_ref, sin_ref, g_ref, qn_ref, kn_ref,
                     q_ref, k_ref, vt_ref, z_ref):
    xb = x_ref[...].astype(BF16)
    cos = cos_ref[...]
    sin = sin_ref[...]
    lane = lax.broadcasted_iota(jnp.int32, cos.shape, 1)
    first_half = (lane % GQA_DH) < (GQA_DH // 2)
    yq = _dot(xb, w_ref[:, 0:GQA_W])
    qn = yq * lax.rsqrt(_group_mean_sq(yq, g_ref, GQA_W) + RMS_EPS) * qn_ref[...]
    for j in range(GQA_W // LANES):
        sl = slice(j * LANES, (j + 1) * LANES)
        q_ref[:, sl] = (_rope64(qn[:, sl], cos, sin, first_half) * (LOG2E * GQA_DH ** -0.5)).astype(BF16)
    yk = _dot(xb, w_ref[:, GQA_W:GQA_W + GQA_KV_W])
    kn = yk * lax.rsqrt(_group_mean_sq(yk, g_ref, GQA_KV_W) + RMS_EPS) * kn_ref[...]
    kr = _rope64(kn, cos, sin, first_half).astype(BF16)
    v_t = _dot(xb, w_ref[:, GQA_W + GQA_KV_W:GQA_W + 2 * GQA_KV_W]).T
    tm = v_t.shape[1]
    pad_row = lax.broadcasted_iota(jnp.int32, (GQA_VROWS - GQA_DH, tm), 0)
    ones_then_zeros = jnp.where(pad_row == 0, 1.0, 0.0)
    for n in range(GQA_KV_HEADS):
        k_ref[n] = kr[:, n * GQA_DH:(n + 1) * GQA_DH]
        vt_ref[n] = jnp.concatenate([v_t[n * GQA_DH:(n + 1) * GQA_DH, :], ones_then_zeros], axis=0).astype(BF16)
    z_ref[...] = _silu(_dot(xb, w_ref[:, GQA_W + 2 * GQA_KV_W:2 * GQA_W + 2 * GQA_KV_W]))


def _row_spec(tm, width):
    return pl.BlockSpec((tm, width), lambda i: (i, 0))


def _const_spec(shape):
    return pl.BlockSpec(shape, lambda i: (0,) * len(shape))


def _table_spec(tm, seq):
    nblk = seq // tm
    return pl.BlockSpec((tm, LANES), lambda i: (i % nblk, 0))


def _proj_ret(x2, w, cos, sin, seq, tm):
    m = x2.shape[0]
    out = [jax.ShapeDtypeStruct((m, RET_W), BF16)] * 3 + [jax.ShapeDtypeStruct((m, RET_W), F32)]
    return pl.pallas_call(
        _proj_ret_kernel, out_shape=out, grid=(m // tm,),
        in_specs=[_row_spec(tm, D_MODEL), _const_spec(w.shape), _table_spec(tm, seq), _table_spec(tm, seq)],
        out_specs=[_row_spec(tm, RET_W)] * 4,
        compiler_params=_params(("parallel",)), name="proj_ret")(x2, w, cos, sin)


def _proj_na(x2, w, tm):
    m = x2.shape[0]
    out = [jax.ShapeDtypeStruct((m, NA_W), BF16)] * 3 + [jax.ShapeDtypeStruct((m, NA_W), F32)]
    return pl.pallas_call(
        _proj_na_kernel, out_shape=out, grid=(m // tm,),
        in_specs=[_row_spec(tm, D_MODEL), _const_spec(w.shape)],
        out_specs=[_row_spec(tm, NA_W)] * 4,
        compiler_params=_params(("parallel",)), name="proj_na")(x2, w)


def _proj_gqa(x2, w, cos, sin, gmat, qn, kn, seq, tm):
    m = x2.shape[0]
    batch, nblk = m // seq, seq // tm
    out = [jax.ShapeDtypeStruct((m, GQA_W), BF16),
           jax.ShapeDtypeStruct((batch, GQA_KV_HEADS, seq, GQA_DH), BF16),
           jax.ShapeDtypeStruct((batch, GQA_KV_HEADS, GQA_VROWS, seq), BF16),
           jax.ShapeDtypeStruct((m, GQA_W), F32)]
    k_spec = pl.BlockSpec((None, GQA_KV_HEADS, tm, GQA_DH), lambda i: (i // nblk, 0, i % nblk, 0))
    vt_spec = pl.BlockSpec((None, GQA_KV_HEADS, GQA_VROWS, tm), lambda i: (i // nblk, 0, 0, i % nblk))
    return pl.pallas_call(
        _proj_gqa_kernel, out_shape=out, grid=(m // tm,),
        in_specs=[_row_spec(tm, D_MODEL), _const_spec(w.shape), _table_spec(tm, seq), _table_spec(tm, seq),
                  _const_spec(gmat.shape), _const_spec(qn.shape), _const_spec(kn.shape)],
        out_specs=[_row_spec(tm, GQA_W), k_spec, vt_spec, _row_spec(tm, GQA_W)],
        compiler_params=_params(("parallel",)), name="proj_gqa")(x2, w, cos, sin, gmat, qn, kn)


def _retention_kernel(dec_ref, qf_ref, kf_ref, vf_ref, qb_ref, kb_ref, vb_ref,
                      of_ref, ob_ref, sf_ref, sb_ref, *, chunks):
    c = RET_CHUNK

    @pl.when(pl.program_id(2) == 0)
    def _():
        sf_ref[...] = jnp.zeros_like(sf_ref)
        sb_ref[...] = jnp.zeros_like(sb_ref)

    lf = dec_ref[0:1, :]
    lb = dec_ref[1:2, :]
    row = lax.broadcasted_iota(jnp.int32, (c, c), 0).astype(F32)
    col = lax.broadcasted_iota(jnp.int32, (c, c), 1).astype(F32)
    diff = row - col
    d_intra = jnp.where(diff >= 0, jnp.exp(jnp.maximum(diff, 0.0) * lf), jnp.exp(jnp.maximum(-diff, 0.0) * lb))
    w_kf = jnp.exp((c - 1.0 - row) * lf)
    w_qf = jnp.exp((row + 1.0) * lf)
    w_kb = jnp.exp(row * lb)
    w_qb = jnp.exp((c - row) * lb)
    chunk_f = jnp.exp(c * lf)
    chunk_b = jnp.exp(c * lb)

    blocks = [slice(i * c, (i + 1) * c) for i in range(chunks)]

    kv_f = [_dot_tn((kf_ref[r, :].astype(F32) * w_kf).astype(BF16), vf_ref[r, :]) for r in blocks]
    kv_b = [_dot_tn((kb_ref[r, :].astype(F32) * w_kb).astype(BF16), vb_ref[r, :]) for r in blocks]
    intra = [_dot((_dot_nt(qf_ref[r, :], kf_ref[r, :]) * d_intra).astype(BF16), vf_ref[r, :]) for r in blocks]

    s_f = sf_ref[...]
    states_f = []
    for i in range(chunks):
        states_f.append(s_f)
        s_f = chunk_f * s_f + kv_f[i]
    sf_ref[...] = s_f
    s_b = sb_ref[...]
    states_b = [None] * chunks
    for i in reversed(range(chunks)):
        states_b[i] = s_b
        s_b = chunk_b * s_b + kv_b[i]
    sb_ref[...] = s_b

    for i, r in enumerate(blocks):
        qw = (qf_ref[r, :].astype(F32) * w_qf).astype(BF16)
        of_ref[r, :] = intra[i] + _dot(qw, states_f[i].astype(BF16))
        qw = (qb_ref[r, :].astype(F32) * w_qb).astype(BF16)
        ob_ref[r, :] = _dot(qw, states_b[i].astype(BF16))


def _retention(rq, rk, rv, decay, batch, seq, chunks):
    m = rq.shape[0]
    rows = chunks * RET_CHUNK
    ng = seq // rows
    fwd = pl.BlockSpec((rows, RET_DK), lambda b, h, g: (b * ng + g, h))
    bwd = pl.BlockSpec((rows, RET_DK), lambda b, h, g: (b * ng + ng - 1 - g, h))
    dec = pl.BlockSpec((None, 8, LANES), lambda b, h, g: (h, 0, 0))
    out = [jax.ShapeDtypeStruct((m, RET_W), F32)] * 2
    return pl.pallas_call(
        functools.partial(_retention_kernel, chunks=chunks), out_shape=out,
        grid=(batch, RET_HEADS, ng),
        in_specs=[dec, fwd, fwd, fwd, bwd, bwd, bwd],
        out_specs=[fwd, bwd],
        scratch_shapes=[pltpu.VMEM((RET_DK, RET_DK), F32), pltpu.VMEM((RET_DK, RET_DK), F32)],
        compiler_params=_params(("parallel", "parallel", "arbitrary")),
        name="retention")(decay, rq, rk, rv, rq, rk, rv)


NA_QROWS = 4
NA_KROWS = 3 * NA_QROWS
NA_TQ = NA_QROWS * GRID_W
NA_TK = NA_KROWS * GRID_W
NA_PAIRS = 4
NA_MASKED = -1e30


def _na_kernel(q_ref, k0_ref, k1_ref, k2_ref, v0_ref, v1_ref, v2_ref, bias_ref, z_ref, o_ref,
               s_ref, p_ref, mx_ref, den_ref):
    lane = lax.broadcasted_iota(jnp.int32, (NA_TK, LANES), 1)
    zero = jnp.zeros((NA_TK, LANES), BF16)
    units = 2 * (q_ref.shape[1] // LANES)

    def masked(refs, u):
        cs = slice((u // 2) * LANES, (u // 2 + 1) * LANES)
        x = jnp.concatenate([r[:, cs] for r in refs], axis=0)
        own = (lane < NA_DH) if u % 2 == 0 else (lane >= NA_DH)
        return jnp.where(own, x, zero)

    def scores(u):
        cs = slice((u // 2) * LANES, (u // 2 + 1) * LANES)
        s = _dot_nt(q_ref[:, cs], masked((k0_ref, k1_ref, k2_ref), u)) + bias_ref[u]
        s_ref[u] = s
        mx_ref[u] = jnp.broadcast_to(jnp.max(s, axis=-1, keepdims=True), (NA_TQ, LANES))

    def softmax(u):
        mx = mx_ref[u]
        den = None
        for j in range(NA_TK // LANES):
            cs = slice(j * LANES, (j + 1) * LANES)
            p = jnp.exp2(s_ref[u, :, cs] - mx)
            p_ref[u, :, cs] = p.astype(BF16)
            den = p if den is None else den + p
        den_ref[u] = jnp.broadcast_to(jnp.sum(den, axis=-1, keepdims=True), (NA_TQ, LANES))

    outs = {}

    def values(u):
        o = _dot(p_ref[u], masked((v0_ref, v1_ref, v2_ref), u)) / den_ref[u]
        if u % 2 == 0:
            outs[u // 2] = o
        else:
            cs = slice((u // 2) * LANES, (u // 2 + 1) * LANES)
            o_ref[:, cs] = ((outs.pop(u // 2) + o) * z_ref[:, cs]).astype(o_ref.dtype)

    for t in range(units + 2):
        if t < units:
            scores(t)
        if 1 <= t <= units:
            softmax(t - 1)
        if t >= 2:
            values(t - 2)


def _na_bias_tables(rpb, rows):
    nblk = rows // NA_QROWS
    nrow, ncol = 2 * NA_KH - 1, 2 * NA_KW - 1
    col = np.arange(GRID_W)
    c_start = np.clip(col - NA_KW // 2, 0, GRID_W - NA_KW)
    col_ok = (col[None, :] >= c_start[:, None]) & (col[None, :] < c_start[:, None] + NA_KW)
    cidx = np.clip(col[None, :] - col[:, None] + NA_KW - 1, 0, ncol - 1)
    col_sel = (cidx[:, :, None] == np.arange(ncol)).astype(np.float32)
    row_sel = np.zeros((3, NA_QROWS, NA_KROWS, nrow + 1), np.float32)
    for cls, blk in enumerate((0, 1, nblk - 1)):
        first_kblk = min(max(blk - 1, 0), nblk - 3)
        qr = blk * NA_QROWS + np.arange(NA_QROWS)
        kr = first_kblk * NA_QROWS + np.arange(NA_KROWS)
        r_start = np.clip(qr - NA_KH // 2, 0, rows - NA_KH)
        ok = (kr[None, :] >= r_start[:, None]) & (kr[None, :] < r_start[:, None] + NA_KH)
        ridx = np.where(ok, kr[None, :] - qr[:, None] + NA_KH - 1, nrow)
        row_sel[cls] = (ridx[:, :, None] == np.arange(nrow + 1))
    by_col = jnp.einsum('har,qkr->haqk', rpb.astype(F32), col_sel, precision=lax.Precision.HIGHEST)
    by_col = jnp.where(col_ok[None, None], by_col * LOG2E, NA_MASKED)
    by_col = jnp.concatenate([by_col, jnp.full((NA_HEADS, 1, GRID_W, GRID_W), NA_MASKED, F32)], axis=1)
    table = jnp.einsum('cija,haqk->chiqjk', row_sel, by_col, precision=lax.Precision.HIGHEST)
    return table.reshape(3, NA_HEADS, NA_TQ, NA_TK)


def _na_attention(nq, nk, nv, nz, bias, batch, seq):
    m = nq.shape[0]
    rows = seq // GRID_W
    nblk = rows // NA_QROWS
    groups = NA_HEADS // (2 * NA_PAIRS)
    width = NA_PAIRS * LANES

    def kblk(j):
        return lambda b, hp, i: (b * nblk + jnp.clip(i - 1, 0, nblk - 3) + j, hp)

    def bias_idx(b, hp, i):
        cls = jnp.where(i == 0, 0, jnp.where(i == nblk - 1, 2, 1))
        return (cls, hp, 0, 0)

    qspec = pl.BlockSpec((NA_TQ, width), lambda b, hp, i: (b * nblk + i, hp))
    kspecs = [pl.BlockSpec((NA_TQ, width), kblk(j)) for j in range(3)]
    return pl.pallas_call(
        _na_kernel, out_shape=jax.ShapeDtypeStruct((m, NA_W), BF16),
        grid=(batch, groups, nblk),
        in_specs=[qspec] + kspecs + kspecs + [pl.BlockSpec((None, 2 * NA_PAIRS, NA_TQ, NA_TK), bias_idx), qspec],
        out_specs=qspec,
        scratch_shapes=[pltpu.VMEM((2 * NA_PAIRS, NA_TQ, NA_TK), F32),
                        pltpu.VMEM((2 * NA_PAIRS, NA_TQ, NA_TK), BF16),
                        pltpu.VMEM((2 * NA_PAIRS, NA_TQ, LANES), F32),
                        pltpu.VMEM((2 * NA_PAIRS, NA_TQ, LANES), F32)],
        compiler_params=_params(("parallel", "parallel", "arbitrary")),
        name="na_attention")(nq, nk, nk, nk, nv, nv, nv, bias, nz)


GQA_GROUP = GQA_HEADS // GQA_KV_HEADS
GQA_VROWS = 128


def _gqa_kernel(q_ref, k_ref, vt_ref, z_ref, o_ref,
                qs_ref, s_ref, mb_ref, m_ref, acc_ref, *, tk, unroll):
    tq = q_ref.shape[0]
    nk = k_ref.shape[0] // tk
    assert nk % 2 == 0 and unroll % 2 == 0 and nk >= 1 + unroll

    def chunk_start(c):
        return c * tk if isinstance(c, int) else pl.multiple_of(c * tk, tk)
    q_t = q_ref[...].astype(F32).T.astype(BF16)
    for g in range(GQA_GROUP):
        qs_ref[:, g * tq:(g + 1) * tq] = q_t[g * GQA_DH:(g + 1) * GQA_DH, :]
    m_ref[...] = jnp.full(m_ref.shape, -jnp.inf, F32)
    acc_ref[...] = jnp.zeros_like(acc_ref)

    def scores(c, par):
        s = _dot(k_ref[pl.ds(chunk_start(c), tk), :], qs_ref[...])
        s_ref[par] = s
        mb_ref[par] = jnp.max(s, axis=0, keepdims=True)

    def softmax_values(c, par):
        m_old = m_ref[...]
        m_new = jnp.maximum(m_old, mb_ref[par])
        m_ref[...] = m_new
        p = jnp.exp2(s_ref[par] - m_new).astype(BF16)
        pv = _dot(vt_ref[:, pl.ds(chunk_start(c), tk)], p)
        acc_ref[...] = jnp.exp2(m_old - m_new) * acc_ref[...] + pv

    def tick(i, par):
        scores(i, par)
        softmax_values(i - 1, 1 - par)

    scores(0, 0)
    first = 1 + (nk - 1) % unroll
    for i in range(1, first):
        tick(i, i % 2)

    def body(j, carry):
        i = first + unroll * j
        for u in range(unroll):
            tick(i + u, (first + u) % 2)
        return carry

    lax.fori_loop(0, (nk - first) // unroll, body, 0)
    softmax_values(nk - 1, 1)

    inv = 1.0 / acc_ref[GQA_DH:GQA_DH + 1, :]
    out = acc_ref[...] * inv
    heads = [out[0:GQA_DH, g * tq:(g + 1) * tq] for g in range(GQA_GROUP)]
    o_ref[...] = (jnp.concatenate(heads, axis=0).T * z_ref[...]).astype(o_ref.dtype)


def _gqa_attention(cq, k2, vt, cz, batch, seq, tq, tk, unroll):
    m = cq.shape[0]
    nq = seq // tq
    width = GQA_GROUP * GQA_DH
    cols = GQA_GROUP * tq
    qspec = pl.BlockSpec((tq, width), lambda b, n, i: (b * nq + i, n))
    return pl.pallas_call(
        functools.partial(_gqa_kernel, tk=tk, unroll=unroll), out_shape=jax.ShapeDtypeStruct((m, GQA_W), BF16),
        grid=(batch, GQA_KV_HEADS, nq),
        in_specs=[qspec,
                  pl.BlockSpec((None, None, seq, GQA_DH), lambda b, n, i: (b, n, 0, 0)),
                  pl.BlockSpec((None, None, GQA_VROWS, seq), lambda b, n, i: (b, n, 0, 0)),
                  qspec],
        out_specs=qspec,
        scratch_shapes=[pltpu.VMEM((GQA_DH, cols), BF16),
                        pltpu.VMEM((2, tk, cols), F32),
                        pltpu.VMEM((2, 1, cols), F32),
                        pltpu.VMEM((1, cols), F32),
                        pltpu.VMEM((GQA_VROWS, cols), F32)],
        compiler_params=_params(("parallel", "parallel", "arbitrary")),
        name="gqa_attention")(cq, k2, vt, cz)


def _merge_kernel(x_ref, of_ref, ob_ref, rz_ref, gn_ref, onb_ref, ogq_ref,
                  wg_ref, wa_ref, wb_ref, wc_ref, wo_ref, lg_ref, lb_ref, o_ref, *, alpha):
    x = x_ref[...]
    xb = x.astype(BF16)
    ra = of_ref[...] + ob_ref[...]
    parts = []
    for h in range(RET_HEADS):
        t = ra[:, h * RET_DK:(h + 1) * RET_DK]
        mu = jnp.mean(t, axis=-1, keepdims=True)
        d = t - mu
        var = jnp.mean(d * d, axis=-1, keepdims=True)
        parts.append(d * lax.rsqrt(var + LN_EPS))
    o_a = (jnp.concatenate(parts, axis=-1) * gn_ref[...] * rz_ref[...]).astype(BF16)
    y = _sigmoid(_dot(xb, wg_ref[:, 0:D_MODEL])) * _dot(o_a, wa_ref[...])
    y = y + _sigmoid(_dot(xb, wg_ref[:, D_MODEL:2 * D_MODEL])) * _dot(onb_ref[...], wb_ref[...])
    y = y + _sigmoid(_dot(xb, wg_ref[:, 2 * D_MODEL:3 * D_MODEL])) * _dot(ogq_ref[...], wc_ref[...])
    z = alpha * x + _dot(y.astype(BF16), wo_ref[...])
    mu = jnp.mean(z, axis=-1, keepdims=True)
    d = z - mu
    var = jnp.mean(d * d, axis=-1, keepdims=True)
    o_ref[...] = d * lax.rsqrt(var + LN_EPS) * lg_ref[...] + lb_ref[...]


def _merge(x2, o_f, o_b, rz, gn, o_nb, o_gq, wg, wa, wb, wc, wo, lg, lb, alpha, tm):
    m = x2.shape[0]
    return pl.pallas_call(
        functools.partial(_merge_kernel, alpha=alpha), out_shape=jax.ShapeDtypeStruct((m, D_MODEL), F32),
        grid=(m // tm,),
        in_specs=[_row_spec(tm, D_MODEL), _row_spec(tm, RET_W), _row_spec(tm, RET_W), _row_spec(tm, RET_W),
                  _const_spec(gn.shape), _row_spec(tm, NA_W), _row_spec(tm, GQA_W),
                  _const_spec(wg.shape), _const_spec(wa.shape), _const_spec(wb.shape), _const_spec(wc.shape),
                  _const_spec(wo.shape), _const_spec(lg.shape), _const_spec(lb.shape)],
        out_specs=_row_spec(tm, D_MODEL),
        compiler_params=_params(("parallel",)), name="merge")(
            x2, o_f, o_b, rz, gn, o_nb, o_gq, wg, wa, wb, wc, wo, lg, lb)


PROJ_TM = 1024
MERGE_TM = 512
RET_CHUNKS_PER_STEP = 8
GQA_TQ = 256
GQA_TK = 512
GQA_UNROLL = 8


def _rope_tables(seq):
    t = jnp.arange(seq)
    tf = t.astype(F32)
    row = (t // GRID_W).astype(F32)
    col = (t % GRID_W).astype(F32)
    inv_ret = 1.0 / (ROPE_THETA ** jnp.linspace(0.0, 1.0, RET_DK // 2, dtype=F32))
    ang_ret = tf[:, None] * inv_ret[None, :]
    n_ax = GQA_DH // 4
    inv_ax = ROPE_THETA ** (-jnp.arange(n_ax, dtype=F32) / n_ax)
    ang_ax = jnp.concatenate([row[:, None] * inv_ax[None, :], col[:, None] * inv_ax[None, :]], -1)

    def full(ang, reps):
        c, s = jnp.cos(ang), jnp.sin(ang)
        return jnp.tile(jnp.concatenate([c, c], -1), (1, reps)), jnp.tile(jnp.concatenate([-s, s], -1), (1, reps))

    return full(ang_ret, 1), full(ang_ax, LANES // GQA_DH)


def kernel(x, w_in, ret_theta_fwd, ret_theta_bwd, ret_gn_gain, na_rpb, gqa_q_norm, gqa_k_norm,
           w_branch_a, w_branch_b, w_branch_c, w_out, ln_gain, ln_bias):
    batch, seq, d_model = x.shape
    depth = w_in.shape[0]
    assert d_model == D_MODEL and w_in.shape[-1] == D_IN and seq % (GRID_W * NA_QROWS) == 0
    alpha = (2.0 * depth) ** 0.25
    m = batch * seq
    (ret_cos, ret_sin), (ax_cos, ax_sin) = _rope_tables(seq)
    gidx = np.arange(GQA_W) // GQA_DH
    gmat = jnp.asarray(gidx[:, None] == gidx[None, :], BF16)

    x2 = x.reshape(m, D_MODEL)
    for l in range(depth):
        w = w_in[l].astype(BF16)
        rq, rk, rv, rz = _proj_ret(x2, w[:, OFF_RET:OFF_NA], ret_cos, ret_sin, seq, PROJ_TM)
        nq, nk, nv, nz = _proj_na(x2, w[:, OFF_NA:OFF_GQA], PROJ_TM)
        qn = jnp.tile(gqa_q_norm[l], GQA_HEADS)[None, :]
        kn = jnp.tile(gqa_k_norm[l], GQA_KV_HEADS)[None, :]
        cq, ck, cvt, cz = _proj_gqa(x2, w[:, OFF_GQA:OFF_GATES], ax_cos, ax_sin, gmat, qn, kn, seq, PROJ_TM)

        decay = jnp.stack([jax.nn.log_sigmoid(ret_theta_fwd[l].astype(F32)),
                           jax.nn.log_sigmoid(ret_theta_bwd[l].astype(F32))], axis=1)
        decay = jnp.broadcast_to(jnp.pad(decay, ((0, 0), (0, 6)))[:, :, None], (RET_HEADS, 8, LANES))
        o_f, o_b = _retention(rq, rk, rv, decay, batch, seq, RET_CHUNKS_PER_STEP)

        bias = _na_bias_tables(na_rpb[l], seq // GRID_W)
        o_nb = _na_attention(nq, nk, nv, nz, bias, batch, seq)

        o_gq = _gqa_attention(cq, ck, cvt, cz, batch, seq, GQA_TQ, GQA_TK, GQA_UNROLL)

        x2 = _merge(x2, o_f, o_b, rz, ret_gn_gain[l][None, :], o_nb, o_gq,
                    w[:, OFF_GATES:D_IN], w_branch_a[l].astype(BF16), w_branch_b[l].astype(BF16),
                    w_branch_c[l].astype(BF16), w_out[l].astype(BF16),
                    ln_gain[l][None, :], ln_bias[l][None, :], alpha, MERGE_TM)
    return x2.reshape(batch, seq, D_MODEL)
```

```python
import functools
import math

import numpy as np
import jax
import jax.numpy as jnp
from jax import lax
from jax.experimental import pallas as pl
from jax.experimental.pallas import tpu as pltpu

D_MODEL = 1024
GRID_W = 64
RET_HEADS = 4
RET_DK = 128
RET_CHUNK = 128
NA_HEADS = 8
NA_DH = 64
NA_KH = 8
NA_KW = 16
GQA_HEADS = 8
GQA_KV_HEADS = 2
GQA_DH = 64
ROPE_THETA = 10000.0
RMS_EPS = 1e-6
LN_EPS = 1e-5

RET_W = RET_HEADS * RET_DK
NA_W = NA_HEADS * NA_DH
GQA_W = GQA_HEADS * GQA_DH
GQA_KV_W = GQA_KV_HEADS * GQA_DH

OFF_RET = 0
OFF_NA = 4 * RET_W
OFF_GQA = OFF_NA + 4 * NA_W
OFF_GATES = OFF_GQA + 2 * GQA_W + 2 * GQA_KV_W
D_IN = OFF_GATES + 3 * D_MODEL

LOG2E = math.log2(math.e)
LANES = 128
VMEM_LIMIT = 56 * 1024 * 1024

BF16 = jnp.bfloat16
F32 = jnp.float32


def _dot(a, b):
    return jnp.dot(a, b, preferred_element_type=F32)


def _dot_nt(a, b):
    return lax.dot_general(a, b, (((1,), (1,)), ((), ())), preferred_element_type=F32)


def _dot_tn(a, b):
    return lax.dot_general(a, b, (((0,), (0,)), ((), ())), preferred_element_type=F32)


def _silu(z):
    return z / (1.0 + jnp.exp(-z))


def _sigmoid(z):
    return 1.0 / (1.0 + jnp.exp(-z))


def _params(sem):
    return pltpu.CompilerParams(dimension_semantics=sem, vmem_limit_bytes=VMEM_LIMIT)


def _proj_ret_kernel(x_ref, w_ref, cos_ref, sin_ref, q_ref, k_ref, v_ref, z_ref):
    xb = x_ref[...].astype(BF16)
    cos = cos_ref[...]
    sin = sin_ref[...]
    scale = RET_DK ** -0.5
    yq = _dot(xb, w_ref[:, 0:RET_W])
    yk = _dot(xb, w_ref[:, RET_W:2 * RET_W])
    for h in range(RET_HEADS):
        sl = slice(h * RET_DK, (h + 1) * RET_DK)
        t = yq[:, sl]
        q_ref[:, sl] = ((t * cos + pltpu.roll(t, RET_DK // 2, 1) * sin) * scale).astype(BF16)
        t = yk[:, sl]
        k_ref[:, sl] = (t * cos + pltpu.roll(t, RET_DK // 2, 1) * sin).astype(BF16)
    v_ref[...] = _dot(xb, w_ref[:, 2 * RET_W:3 * RET_W]).astype(BF16)
    z_ref[...] = _silu(_dot(xb, w_ref[:, 3 * RET_W:4 * RET_W]))


def _proj_na_kernel(x_ref, w_ref, q_ref, k_ref, v_ref, z_ref):
    xb = x_ref[...].astype(BF16)
    q_ref[...] = (_dot(xb, w_ref[:, 0:NA_W]) * (LOG2E * NA_DH ** -0.5)).astype(BF16)
    k_ref[...] = _dot(xb, w_ref[:, NA_W:2 * NA_W]).astype(BF16)
    v_ref[...] = _dot(xb, w_ref[:, 2 * NA_W:3 * NA_W]).astype(BF16)
    z_ref[...] = _silu(_dot(xb, w_ref[:, 3 * NA_W:4 * NA_W]))


def _group_mean_sq(y, g_ref, width):
    sq = y * y
    hi = sq.astype(BF16)
    lo = (sq - hi.astype(F32)).astype(BF16)
    g = g_ref[0:width, 0:width]
    return (_dot(hi, g) + _dot(lo, g)) * (1.0 / GQA_DH)


def _rope64(t, cos, sin, first_half):
    swapped = jnp.where(first_half, pltpu.roll(t, LANES - GQA_DH // 2, 1), pltpu.roll(t, GQA_DH // 2, 1))
    return t * cos + swapped * sin


def _proj_gqa_kernel(x_ref, w_ref, cos_ref, sin_ref, g_ref, qn_ref, kn_ref,
                     q_ref, k_ref, vt_ref, z_ref):
    xb = x_ref[...].astype(BF16)
    cos = cos_ref[...]
    sin = sin_ref[...]
    lane = lax.broadcasted_iota(jnp.int32, cos.shape, 1)
    first_half = (lane % GQA_DH) < (GQA_DH // 2)
    yq = _dot(xb, w_ref[:, 0:GQA_W])
    qn = yq * lax.rsqrt(_group_mean_sq(yq, g_ref, GQA_W) + RMS_EPS) * qn_ref[...]
    for j in range(GQA_W // LANES):
        sl = slice(j * LANES, (j + 1) * LANES)
        q_ref[:, sl] = (_rope64(qn[:, sl], cos, sin, first_half) * (LOG2E * GQA_DH ** -0.5)).astype(BF16)
    yk = _dot(xb, w_ref[:, GQA_W:GQA_W + GQA_KV_W])
    kn = yk * lax.rsqrt(_group_mean_sq(yk, g_ref, GQA_KV_W) + RMS_EPS) * kn_ref[...]
    kr = _rope64(kn, cos, sin, first_half).astype(BF16)
    v_t = _dot(xb, w_ref[:, GQA_W + GQA_KV_W:GQA_W + 2 * GQA_KV_W]).T
    tm = v_t.shape[1]
    pad_row = lax.broadcasted_iota(jnp.int32, (GQA_VROWS - GQA_DH, tm), 0)
    ones_then_zeros = jnp.where(pad_row == 0, 1.0, 0.0)
    for n in range(GQA_KV_HEADS):
        k_ref[n] = kr[:, n * GQA_DH:(n + 1) * GQA_DH]
        vt_ref[n] = jnp.concatenate([v_t[n * GQA_DH:(n + 1) * GQA_DH, :], ones_then_zeros], axis=0).astype(BF16)
    z_ref[...] = _silu(_dot(xb, w_ref[:, GQA_W + 2 * GQA_KV_W:2 * GQA_W + 2 * GQA_KV_W]))


def _row_spec(tm, width):
    return pl.BlockSpec((tm, width), lambda i: (i, 0))


def _const_spec(shape):
    return pl.BlockSpec(shape, lambda i: (0,) * len(shape))


def _table_spec(tm, seq):
    nblk = seq // tm
    return pl.BlockSpec((tm, LANES), lambda i: (i % nblk, 0))


def _proj_ret(x2, w, cos, sin, seq, tm):
    m = x2.shape[0]
    out = [jax.ShapeDtypeStruct((m, RET_W), BF16)] * 3 + [jax.ShapeDtypeStruct((m, RET_W), F32)]
    return pl.pallas_call(
        _proj_ret_kernel, out_shape=out, grid=(m // tm,),
        in_specs=[_row_spec(tm, D_MODEL), _const_spec(w.shape), _table_spec(tm, seq), _table_spec(tm, seq)],
        out_specs=[_row_spec(tm, RET_W)] * 4,
        compiler_params=_params(("parallel",)), name="proj_ret")(x2, w, cos, sin)


def _proj_na(x2, w, tm):
    m = x2.shape[0]
    out = [jax.ShapeDtypeStruct((m, NA_W), BF16)] * 3 + [jax.ShapeDtypeStruct((m, NA_W), F32)]
    return pl.pallas_call(
        _proj_na_kernel, out_shape=out, grid=(m // tm,),
        in_specs=[_row_spec(tm, D_MODEL), _const_spec(w.shape)],
        out_specs=[_row_spec(tm, NA_W)] * 4,
        compiler_params=_params(("parallel",)), name="proj_na")(x2, w)


def _proj_gqa(x2, w, cos, sin, gmat, qn, kn, seq, tm):
    m = x2.shape[0]
    batch, nblk = m // seq, seq // tm
    out = [jax.ShapeDtypeStruct((m, GQA_W), BF16),
           jax.ShapeDtypeStruct((batch, GQA_KV_HEADS, seq, GQA_DH), BF16),
           jax.ShapeDtypeStruct((batch, GQA_KV_HEADS, GQA_VROWS, seq), BF16),
           jax.ShapeDtypeStruct((m, GQA_W), F32)]
    k_spec = pl.BlockSpec((None, GQA_KV_HEADS, tm, GQA_DH), lambda i: (i // nblk, 0, i % nblk, 0))
    vt_spec = pl.BlockSpec((None, GQA_KV_HEADS, GQA_VROWS, tm), lambda i: (i // nblk, 0, 0, i % nblk))
    return pl.pallas_call(
        _proj_gqa_kernel, out_shape=out, grid=(m // tm,),
        in_specs=[_row_spec(tm, D_MODEL), _const_spec(w.shape), _table_spec(tm, seq), _table_spec(tm, seq),
                  _const_spec(gmat.shape), _const_spec(qn.shape), _const_spec(kn.shape)],
        out_specs=[_row_spec(tm, GQA_W), k_spec, vt_spec, _row_spec(tm, GQA_W)],
        compiler_params=_params(("parallel",)), name="proj_gqa")(x2, w, cos, sin, gmat, qn, kn)


def _retention_kernel(dec_ref, qf_ref, kf_ref, vf_ref, qb_ref, kb_ref, vb_ref,
                      of_ref, ob_ref, sf_ref, sb_ref, *, chunks):
    c = RET_CHUNK

    @pl.when(pl.program_id(2) == 0)
    def _():
        sf_ref[...] = jnp.zeros_like(sf_ref)
        sb_ref[...] = jnp.zeros_like(sb_ref)

    lf = dec_ref[0:1, :]
    lb = dec_ref[1:2, :]
    row = lax.broadcasted_iota(jnp.int32, (c, c), 0).astype(F32)
    col = lax.broadcasted_iota(jnp.int32, (c, c), 1).astype(F32)
    diff = row - col
    d_intra = jnp.where(diff >= 0, jnp.exp(jnp.maximum(diff, 0.0) * lf), jnp.exp(jnp.maximum(-diff, 0.0) * lb))
    w_kf = jnp.exp((c - 1.0 - row) * lf)
    w_qf = jnp.exp((row + 1.0) * lf)
    w_kb = jnp.exp(row * lb)
    w_qb = jnp.exp((c - row) * lb)
    chunk_f = jnp.exp(c * lf)
    chunk_b = jnp.exp(c * lb)

    blocks = [slice(i * c, (i + 1) * c) for i in range(chunks)]

    kv_f = [_dot_tn((kf_ref[r, :].astype(F32) * w_kf).astype(BF16), vf_ref[r, :]) for r in blocks]
    kv_b = [_dot_tn((kb_ref[r, :].astype(F32) * w_kb).astype(BF16), vb_ref[r, :]) for r in blocks]
    intra = [_dot((_dot_nt(qf_ref[r, :], kf_ref[r, :]) * d_intra).astype(BF16), vf_ref[r, :]) for r in blocks]

    s_f = sf_ref[...]
    states_f = []
    for i in range(chunks):
        states_f.append(s_f)
        s_f = chunk_f * s_f + kv_f[i]
    sf_ref[...] = s_f
    s_b = sb_ref[...]
    states_b = [None] * chunks
    for i in reversed(range(chunks)):
        states_b[i] = s_b
        s_b = chunk_b * s_b + kv_b[i]
    sb_ref[...] = s_b

    for i, r in enumerate(blocks):
        qw = (qf_ref[r, :].astype(F32) * w_qf).astype(BF16)
        of_ref[r, :] = intra[i] + _dot(qw, states_f[i].astype(BF16))
        qw = (qb_ref[r, :].astype(F32) * w_qb).astype(BF16)
        ob_ref[r, :] = _dot(qw, states_b[i].astype(BF16))


def _retention(rq, rk, rv, decay, batch, seq, chunks):
    m = rq.shape[0]
    rows = chunks * RET_CHUNK
    ng = seq // rows
    fwd = pl.BlockSpec((rows, RET_DK), lambda b, h, g: (b * ng + g, h))
    bwd = pl.BlockSpec((rows, RET_DK), lambda b, h, g: (b * ng + ng - 1 - g, h))
    dec = pl.BlockSpec((None, 8, LANES), lambda b, h, g: (h, 0, 0))
    out = [jax.ShapeDtypeStruct((m, RET_W), F32)] * 2
    return pl.pallas_call(
        functools.partial(_retention_kernel, chunks=chunks), out_shape=out,
        grid=(batch, RET_HEADS, ng),
        in_specs=[dec, fwd, fwd, fwd, bwd, bwd, bwd],
        out_specs=[fwd, bwd],
        scratch_shapes=[pltpu.VMEM((RET_DK, RET_DK), F32), pltpu.VMEM((RET_DK, RET_DK), F32)],
        compiler_params=_params(("parallel", "parallel", "arbitrary")),
        name="retention")(decay, rq, rk, rv, rq, rk, rv)


NA_QROWS = 4
NA_KROWS = 3 * NA_QROWS
NA_TQ = NA_QROWS * GRID_W
NA_TK = NA_KROWS * GRID_W
NA_PAIRS = 4
NA_MASKED = -1e30


def _na_kernel(q_ref, k0_ref, k1_ref, k2_ref, v0_ref, v1_ref, v2_ref, bias_ref, z_ref, o_ref,
               s_ref, p_ref, mx_ref, den_ref):
    lane = lax.broadcasted_iota(jnp.int32, (NA_TK, LANES), 1)
    zero = jnp.zeros((NA_TK, LANES), BF16)
    units = 2 * (q_ref.shape[1] // LANES)

    def masked(refs, u):
        cs = slice((u // 2) * LANES, (u // 2 + 1) * LANES)
        x = jnp.concatenate([r[:, cs] for r in refs], axis=0)
        own = (lane < NA_DH) if u % 2 == 0 else (lane >= NA_DH)
        return jnp.where(own, x, zero)

    def scores(u):
        cs = slice((u // 2) * LANES, (u // 2 + 1) * LANES)
        s = _dot_nt(q_ref[:, cs], masked((k0_ref, k1_ref, k2_ref), u)) + bias_ref[u]
        s_ref[u] = s
        mx_ref[u] = jnp.broadcast_to(jnp.max(s, axis=-1, keepdims=True), (NA_TQ, LANES))

    def softmax(u):
        mx = mx_ref[u]
        den = None
        for j in range(NA_TK // LANES):
            cs = slice(j * LANES, (j + 1) * LANES)
            p = jnp.exp2(s_ref[u, :, cs] - mx)
            p_ref[u, :, cs] = p.astype(BF16)
            den = p if den is None else den + p
        den_ref[u] = jnp.broadcast_to(jnp.sum(den, axis=-1, keepdims=True), (NA_TQ, LANES))

    outs = {}

    def values(u):
        o = _dot(p_ref[u], masked((v0_ref, v1_ref, v2_ref), u)) / den_ref[u]
        if u % 2 == 0:
            outs[u // 2] = o
        else:
            cs = slice((u // 2) * LANES, (u // 2 + 1) * LANES)
            o_ref[:, cs] = ((outs.pop(u // 2) + o) * z_ref[:, cs]).astype(o_ref.dtype)

    for t in range(units + 2):
        if t < units:
            scores(t)
        if 1 <= t <= units:
            softmax(t - 1)
        if t >= 2:
            values(t - 2)


def _na_bias_tables(rpb, rows):
    nblk = rows // NA_QROWS
    nrow, ncol = 2 * NA_KH - 1, 2 * NA_KW - 1
    col = np.arange(GRID_W)
    c_start = np.clip(col - NA_KW // 2, 0, GRID_W - NA_KW)
    col_ok = (col[None, :] >= c_start[:, None]) & (col[None, :] < c_start[:, None] + NA_KW)
    cidx = np.clip(col[None, :] - col[:, None] + NA_KW - 1, 0, ncol - 1)
    col_sel = (cidx[:, :, None] == np.arange(ncol)).astype(np.float32)
    row_sel = np.zeros((3, NA_QROWS, NA_KROWS, nrow + 1), np.float32)
    for cls, blk in enumerate((0, 1, nblk - 1)):
        first_kblk = min(max(blk - 1, 0), nblk - 3)
        qr = blk * NA_QROWS + np.arange(NA_QROWS)
        kr = first_kblk * NA_QROWS + np.arange(NA_KROWS)
        r_start = np.clip(qr - NA_KH // 2, 0, rows - NA_KH)
        ok = (kr[None, :] >= r_start[:, None]) & (kr[None, :] < r_start[:, None] + NA_KH)
        ridx = np.where(ok, kr[None, :] - qr[:, None] + NA_KH - 1, nrow)
        row_sel[cls] = (ridx[:, :, None] == np.arange(nrow + 1))
    by_col = jnp.einsum('har,qkr->haqk', rpb.astype(F32), col_sel, precision=lax.Precision.HIGHEST)
    by_col = jnp.where(col_ok[None, None], by_col * LOG2E, NA_MASKED)
    by_col = jnp.concatenate([by_col, jnp.full((NA_HEADS, 1, GRID_W, GRID_W), NA_MASKED, F32)], axis=1)
    table = jnp.einsum('cija,haqk->chiqjk', row_sel, by_col, precision=lax.Precision.HIGHEST)
    return table.reshape(3, NA_HEADS, NA_TQ, NA_TK)


def _na_attention(nq, nk, nv, nz, bias, batch, seq):
    m = nq.shape[0]
    rows = seq // GRID_W
    nblk = rows // NA_QROWS
    groups = NA_HEADS // (2 * NA_PAIRS)
    width = NA_PAIRS * LANES

    def kblk(j):
        return lambda b, hp, i: (b * nblk + jnp.clip(i - 1, 0, nblk - 3) + j, hp)

    def bias_idx(b, hp, i):
        cls = jnp.where(i == 0, 0, jnp.where(i == nblk - 1, 2, 1))
        return (cls, hp, 0, 0)

    qspec = pl.BlockSpec((NA_TQ, width), lambda b, hp, i: (b * nblk + i, hp))
    kspecs = [pl.BlockSpec((NA_TQ, width), kblk(j)) for j in range(3)]
    return pl.pallas_call(
        _na_kernel, out_shape=jax.ShapeDtypeStruct((m, NA_W), BF16),
        grid=(batch, groups, nblk),
        in_specs=[qspec] + kspecs + kspecs + [pl.BlockSpec((None, 2 * NA_PAIRS, NA_TQ, NA_TK), bias_idx), qspec],
        out_specs=qspec,
        scratch_shapes=[pltpu.VMEM((2 * NA_PAIRS, NA_TQ, NA_TK), F32),
                        pltpu.VMEM((2 * NA_PAIRS, NA_TQ, NA_TK), BF16),
                        pltpu.VMEM((2 * NA_PAIRS, NA_TQ, LANES), F32),
                        pltpu.VMEM((2 * NA_PAIRS, NA_TQ, LANES), F32)],
        compiler_params=_params(("parallel", "parallel", "arbitrary")),
        name="na_attention")(nq, nk, nk, nk, nv, nv, nv, bias, nz)


GQA_GROUP = GQA_HEADS // GQA_KV_HEADS
GQA_VROWS = 128


def _gqa_kernel(q_ref, k_ref, vt_ref, z_ref, o_ref,
                qs_ref, s_ref, mb_ref, m_ref, acc_ref, *, tk, unroll):
    tq = q_ref.shape[0]
    nk = k_ref.shape[0] // tk
    assert nk % 2 == 0 and unroll % 2 == 0 and nk >= 1 + unroll

    def chunk_start(c):
        return c * tk if isinstance(c, int) else pl.multiple_of(c * tk, tk)
    q_t = q_ref[...].astype(F32).T.astype(BF16)
    for g in range(GQA_GROUP):
        qs_ref[:, g * tq:(g + 1) * tq] = q_t[g * GQA_DH:(g + 1) * GQA_DH, :]
    m_ref[...] = jnp.full(m_ref.shape, -jnp.inf, F32)
    acc_ref[...] = jnp.zeros_like(acc_ref)

    def scores(c, par):
        s = _dot(k_ref[pl.ds(chunk_start(c), tk), :], qs_ref[...])
        s_ref[par] = s
        mb_ref[par] = jnp.max(s, axis=0, keepdims=True)

    def softmax_values(c, par):
        m_old = m_ref[...]
        m_new = jnp.maximum(m_old, mb_ref[par])
        m_ref[...] = m_new
        p = jnp.exp2(s_ref[par] - m_new).astype(BF16)
        pv = _dot(vt_ref[:, pl.ds(chunk_start(c), tk)], p)
        acc_ref[...] = jnp.exp2(m_old - m_new) * acc_ref[...] + pv

    def tick(i, par):
        scores(i, par)
        softmax_values(i - 1, 1 - par)

    last = nk - unroll
    first = last - ((last - 1) // unroll) * unroll
    scores(0, 0)
    for i in range(1, first):
        tick(i, i % 2)

    def body(j, carry):
        i = first + unroll * j
        for u in range(unroll):
            tick(i + u, (first + u) % 2)
        return carry

    lax.fori_loop(0, (last - first) // unroll, body, 0)
    for i in range(last, nk):
        tick(i, i % 2)
    softmax_values(nk - 1, 1)

    inv = 1.0 / acc_ref[GQA_DH:GQA_DH + 1, :]
    out = acc_ref[...] * inv
    heads = [out[0:GQA_DH, g * tq:(g + 1) * tq] for g in range(GQA_GROUP)]
    o_ref[...] = (jnp.concatenate(heads, axis=0).T * z_ref[...]).astype(o_ref.dtype)


def _gqa_attention(cq, k2, vt, cz, batch, seq, tq, tk, unroll):
    m = cq.shape[0]
    nq = seq // tq
    width = GQA_GROUP * GQA_DH
    cols = GQA_GROUP * tq
    qspec = pl.BlockSpec((tq, width), lambda b, n, i: (b * nq + i, n))
    return pl.pallas_call(
        functools.partial(_gqa_kernel, tk=tk, unroll=unroll), out_shape=jax.ShapeDtypeStruct((m, GQA_W), BF16),
        grid=(batch, GQA_KV_HEADS, nq),
        in_specs=[qspec,
                  pl.BlockSpec((None, None, seq, GQA_DH), lambda b, n, i: (b, n, 0, 0)),
                  pl.BlockSpec((None, None, GQA_VROWS, seq), lambda b, n, i: (b, n, 0, 0)),
                  qspec],
        out_specs=qspec,
        scratch_shapes=[pltpu.VMEM((GQA_DH, cols), BF16),
                        pltpu.VMEM((2, tk, cols), F32),
                        pltpu.VMEM((2, 1, cols), F32),
                        pltpu.VMEM((1, cols), F32),
                        pltpu.VMEM((GQA_VROWS, cols), F32)],
        compiler_params=_params(("parallel", "parallel", "arbitrary")),
        name="gqa_attention")(cq, k2, vt, cz)


def _merge_kernel(x_ref, of_ref, ob_ref, rz_ref, gn_ref, onb_ref, ogq_ref,
                  wg_ref, wa_ref, wb_ref, wc_ref, wo_ref, lg_ref, lb_ref, o_ref, *, alpha):
    x = x_ref[...]
    xb = x.astype(BF16)
    ra = of_ref[...] + ob_ref[...]
    parts = []
    for h in range(RET_HEADS):
        t = ra[:, h * RET_DK:(h + 1) * RET_DK]
        mu = jnp.mean(t, axis=-1, keepdims=True)
        d = t - mu
        var = jnp.mean(d * d, axis=-1, keepdims=True)
        parts.append(d * lax.rsqrt(var + LN_EPS))
    o_a = (jnp.concatenate(parts, axis=-1) * gn_ref[...] * rz_ref[...]).astype(BF16)
    y = _sigmoid(_dot(xb, wg_ref[:, 0:D_MODEL])) * _dot(o_a, wa_ref[...])
    y = y + _sigmoid(_dot(xb, wg_ref[:, D_MODEL:2 * D_MODEL])) * _dot(onb_ref[...], wb_ref[...])
    y = y + _sigmoid(_dot(xb, wg_ref[:, 2 * D_MODEL:3 * D_MODEL])) * _dot(ogq_ref[...], wc_ref[...])
    z = alpha * x + _dot(y.astype(BF16), wo_ref[...])
    mu = jnp.mean(z, axis=-1, keepdims=True)
    d = z - mu
    var = jnp.mean(d * d, axis=-1, keepdims=True)
    o_ref[...] = d * lax.rsqrt(var + LN_EPS) * lg_ref[...] + lb_ref[...]


def _merge(x2, o_f, o_b, rz, gn, o_nb, o_gq, wg, wa, wb, wc, wo, lg, lb, alpha, tm):
    m = x2.shape[0]
    return pl.pallas_call(
        functools.partial(_merge_kernel, alpha=alpha), out_shape=jax.ShapeDtypeStruct((m, D_MODEL), F32),
        grid=(m // tm,),
        in_specs=[_row_spec(tm, D_MODEL), _row_spec(tm, RET_W), _row_spec(tm, RET_W), _row_spec(tm, RET_W),
                  _const_spec(gn.shape), _row_spec(tm, NA_W), _row_spec(tm, GQA_W),
                  _const_spec(wg.shape), _const_spec(wa.shape), _const_spec(wb.shape), _const_spec(wc.shape),
                  _const_spec(wo.shape), _const_spec(lg.shape), _const_spec(lb.shape)],
        out_specs=_row_spec(tm, D_MODEL),
        compiler_params=_params(("parallel",)), name="merge")(
            x2, o_f, o_b, rz, gn, o_nb, o_gq, wg, wa, wb, wc, wo, lg, lb)


PROJ_TM = 1024
MERGE_TM = 512
RET_CHUNKS_PER_STEP = 8
GQA_TQ = 256
GQA_TK = 512
GQA_UNROLL = 8


def _rope_tables(seq):
    t = jnp.arange(seq)
    tf = t.astype(F32)
    row = (t // GRID_W).astype(F32)
    col = (t % GRID_W).astype(F32)
    inv_ret = 1.0 / (ROPE_THETA ** jnp.linspace(0.0, 1.0, RET_DK // 2, dtype=F32))
    ang_ret = tf[:, None] * inv_ret[None, :]
    n_ax = GQA_DH // 4
    inv_ax = ROPE_THETA ** (-jnp.arange(n_ax, dtype=F32) / n_ax)
    ang_ax = jnp.concatenate([row[:, None] * inv_ax[None, :], col[:, None] * inv_ax[None, :]], -1)

    def full(ang, reps):
        c, s = jnp.cos(ang), jnp.sin(ang)
        return jnp.tile(jnp.concatenate([c, c], -1), (1, reps)), jnp.tile(jnp.concatenate([-s, s], -1), (1, reps))

    return full(ang_ret, 1), full(ang_ax, LANES // GQA_DH)


def kernel(x, w_in, ret_theta_fwd, ret_theta_bwd, ret_gn_gain, na_rpb, gqa_q_norm, gqa_k_norm,
           w_branch_a, w_branch_b, w_branch_c, w_out, ln_gain, ln_bias):
    batch, seq, d_model = x.shape
    depth = w_in.shape[0]
    assert d_model == D_MODEL and w_in.shape[-1] == D_IN and seq % (GRID_W * NA_QROWS) == 0
    alpha = (2.0 * depth) ** 0.25
    m = batch * seq
    (ret_cos, ret_sin), (ax_cos, ax_sin) = _rope_tables(seq)
    gidx = np.arange(GQA_W) // GQA_DH
    gmat = jnp.asarray(gidx[:, None] == gidx[None, :], BF16)

    x2 = x.reshape(m, D_MODEL)
    for l in range(depth):
        w = w_in[l].astype(BF16)
        rq, rk, rv, rz = _proj_ret(x2, w[:, OFF_RET:OFF_NA], ret_cos, ret_sin, seq, PROJ_TM)
        nq, nk, nv, nz = _proj_na(x2, w[:, OFF_NA:OFF_GQA], PROJ_TM)
        qn = jnp.tile(gqa_q_norm[l], GQA_HEADS)[None, :]
        kn = jnp.tile(gqa_k_norm[l], GQA_KV_HEADS)[None, :]
        cq, ck, cvt, cz = _proj_gqa(x2, w[:, OFF_GQA:OFF_GATES], ax_cos, ax_sin, gmat, qn, kn, seq, PROJ_TM)

        decay = jnp.stack([jax.nn.log_sigmoid(ret_theta_fwd[l].astype(F32)),
                           jax.nn.log_sigmoid(ret_theta_bwd[l].astype(F32))], axis=1)
        decay = jnp.broadcast_to(jnp.pad(decay, ((0, 0), (0, 6)))[:, :, None], (RET_HEADS, 8, LANES))
        o_f, o_b = _retention(rq, rk, rv, decay, batch, seq, RET_CHUNKS_PER_STEP)

        bias = _na_bias_tables(na_rpb[l], seq // GRID_W)
        o_nb = _na_attention(nq, nk, nv, nz, bias, batch, seq)

        o_gq = _gqa_attention(cq, ck, cvt, cz, batch, seq, GQA_TQ, GQA_TK, GQA_UNROLL)

        x2 = _merge(x2, o_f, o_b, rz, ret_gn_gain[l][None, :], o_nb, o_gq,
                    w[:, OFF_GATES:D_IN], w_branch_a[l].astype(BF16), w_branch_b[l].astype(BF16),
                    w_branch_c[l].astype(BF16), w_out[l].astype(BF16),
                    ln_gain[l][None, :], ln_bias[l][None, :], alpha, MERGE_TM)
    return x2.reshape(batch, seq, D_MODEL)
```

```python
import functools
import math

import numpy as np
import jax
import jax.numpy as jnp
from jax import lax
from jax.experimental import pallas as pl
from jax.experimental.pallas import tpu as pltpu

D_MODEL = 1024
GRID_W = 64
RET_HEADS = 4
RET_DK = 128
RET_CHUNK = 128
NA_HEADS = 8
NA_DH = 64
NA_KH = 8
NA_KW = 16
GQA_HEADS = 8
GQA_KV_HEADS = 2
GQA_DH = 64
ROPE_THETA = 10000.0
RMS_EPS = 1e-6
LN_EPS = 1e-5

RET_W = RET_HEADS * RET_DK
NA_W = NA_HEADS * NA_DH
GQA_W = GQA_HEADS * GQA_DH
GQA_KV_W = GQA_KV_HEADS * GQA_DH

OFF_RET = 0
OFF_NA = 4 * RET_W
OFF_GQA = OFF_NA + 4 * NA_W
OFF_GATES = OFF_GQA + 2 * GQA_W + 2 * GQA_KV_W
D_IN = OFF_GATES + 3 * D_MODEL

LOG2E = math.log2(math.e)
LANES = 128
VMEM_LIMIT = 56 * 1024 * 1024

BF16 = jnp.bfloat16
F32 = jnp.float32


def _dot(a, b):
    return jnp.dot(a, b, preferred_element_type=F32)


def _dot_nt(a, b):
    return lax.dot_general(a, b, (((1,), (1,)), ((), ())), preferred_element_type=F32)


def _dot_tn(a, b):
    return lax.dot_general(a, b, (((0,), (0,)), ((), ())), preferred_element_type=F32)


def _silu(z):
    return z / (1.0 + jnp.exp(-z))


def _sigmoid(z):
    return 1.0 / (1.0 + jnp.exp(-z))


def _params(sem):
    return pltpu.CompilerParams(dimension_semantics=sem, vmem_limit_bytes=VMEM_LIMIT)


def _proj_ret_kernel(x_ref, w_ref, cos_ref, sin_ref, q_ref, k_ref, v_ref, z_ref):
    xb = x_ref[...].astype(BF16)
    cos = cos_ref[...]
    sin = sin_ref[...]
    scale = RET_DK ** -0.5
    yq = _dot(xb, w_ref[:, 0:RET_W])
    yk = _dot(xb, w_ref[:, RET_W:2 * RET_W])
    for h in range(RET_HEADS):
        sl = slice(h * RET_DK, (h + 1) * RET_DK)
        t = yq[:, sl]
        q_ref[:, sl] = ((t * cos + pltpu.roll(t, RET_DK // 2, 1) * sin) * scale).astype(BF16)
        t = yk[:, sl]
        k_ref[:, sl] = (t * cos + pltpu.roll(t, RET_DK // 2, 1) * sin).astype(BF16)
    v_ref[...] = _dot(xb, w_ref[:, 2 * RET_W:3 * RET_W]).astype(BF16)
    z_ref[...] = _silu(_dot(xb, w_ref[:, 3 * RET_W:4 * RET_W]))


def _proj_na_kernel(x_ref, w_ref, q_ref, k_ref, v_ref, z_ref):
    xb = x_ref[...].astype(BF16)
    q_ref[...] = (_dot(xb, w_ref[:, 0:NA_W]) * (LOG2E * NA_DH ** -0.5)).astype(BF16)
    k_ref[...] = _dot(xb, w_ref[:, NA_W:2 * NA_W]).astype(BF16)
    v_ref[...] = _dot(xb, w_ref[:, 2 * NA_W:3 * NA_W]).astype(BF16)
    z_ref[...] = _silu(_dot(xb, w_ref[:, 3 * NA_W:4 * NA_W]))


def _group_mean_sq(y, g_ref, width):
    sq = y * y
    hi = sq.astype(BF16)
    lo = (sq - hi.astype(F32)).astype(BF16)
    g = g_ref[0:width, 0:width]
    return (_dot(hi, g) + _dot(lo, g)) * (1.0 / GQA_DH)


def _rope64(t, cos, sin, first_half):
    swapped = jnp.where(first_half, pltpu.roll(t, LANES - GQA_DH // 2, 1), pltpu.roll(t, GQA_DH // 2, 1))
    return t * cos + swapped * sin


def _proj_gqa_kernel(x_ref, w_ref, cos_ref, sin_ref, g_ref, qn_ref, kn_ref,
                     q_ref, k_ref, vt_ref, z_ref):
    xb = x_ref[...].astype(BF16)
    cos = cos_ref[...]
    sin = sin_ref[...]
    lane = lax.broadcasted_iota(jnp.int32, cos.shape, 1)
    first_half = (lane % GQA_DH) < (GQA_DH // 2)
    yq = _dot(xb, w_ref[:, 0:GQA_W])
    qn = yq * lax.rsqrt(_group_mean_sq(yq, g_ref, GQA_W) + RMS_EPS) * qn_ref[...]
    for j in range(GQA_W // LANES):
        sl = slice(j * LANES, (j + 1) * LANES)
        q_ref[:, sl] = (_rope64(qn[:, sl], cos, sin, first_half) * (LOG2E * GQA_DH ** -0.5)).astype(BF16)
    yk = _dot(xb, w_ref[:, GQA_W:GQA_W + GQA_KV_W])
    kn = yk * lax.rsqrt(_group_mean_sq(yk, g_ref, GQA_KV_W) + RMS_EPS) * kn_ref[...]
    kr = _rope64(kn, cos, sin, first_half).astype(BF16)
    v_t = _dot(xb, w_ref[:, GQA_W + GQA_KV_W:GQA_W + 2 * GQA_KV_W]).T
    tm = v_t.shape[1]
    pad_row = lax.broadcasted_iota(jnp.int32, (GQA_VROWS - GQA_DH, tm), 0)
    ones_then_zeros = jnp.where(pad_row == 0, 1.0, 0.0)
    for n in range(GQA_KV_HEADS):
        k_ref[n] = kr[:, n * GQA_DH:(n + 1) * GQA_DH]
        vt_ref[n] = jnp.concatenate([v_t[n * GQA_DH:(n + 1) * GQA_DH, :], ones_then_zeros], axis=0).astype(BF16)
    z_ref[...] = _silu(_dot(xb, w_ref[:, GQA_W + 2 * GQA_KV_W:2 * GQA_W + 2 * GQA_KV_W]))


def _row_spec(tm, width):
    return pl.BlockSpec((tm, width), lambda i: (i, 0))


def _const_spec(shape):
    return pl.BlockSpec(shape, lambda i: (0,) * len(shape))


def _table_spec(tm, seq):
    nblk = seq // tm
    return pl.BlockSpec((tm, LANES), lambda i: (i % nblk, 0))


def _proj_ret(x2, w, cos, sin, seq, tm):
    m = x2.shape[0]
    out = [jax.ShapeDtypeStruct((m, RET_W), BF16)] * 3 + [jax.ShapeDtypeStruct((m, RET_W), F32)]
    return pl.pallas_call(
        _proj_ret_kernel, out_shape=out, grid=(m // tm,),
        in_specs=[_row_spec(tm, D_MODEL), _const_spec(w.shape), _table_spec(tm, seq), _table_spec(tm, seq)],
        out_specs=[_row_spec(tm, RET_W)] * 4,
        compiler_params=_params(("parallel",)), name="proj_ret")(x2, w, cos, sin)


def _proj_na(x2, w, tm):
    m = x2.shape[0]
    out = [jax.ShapeDtypeStruct((m, NA_W), BF16)] * 3 + [jax.ShapeDtypeStruct((m, NA_W), F32)]
    return pl.pallas_call(
        _proj_na_kernel, out_shape=out, grid=(m // tm,),
        in_specs=[_row_spec(tm, D_MODEL), _const_spec(w.shape)],
        out_specs=[_row_spec(tm, NA_W)] * 4,
        compiler_params=_params(("parallel",)), name="proj_na")(x2, w)


def _proj_gqa(x2, w, cos, sin, gmat, qn, kn, seq, tm):
    m = x2.shape[0]
    batch, nblk = m // seq, seq // tm
    out = [jax.ShapeDtypeStruct((m, GQA_W), BF16),
           jax.ShapeDtypeStruct((batch, GQA_KV_HEADS, seq, GQA_DH), BF16),
           jax.ShapeDtypeStruct((batch, GQA_KV_HEADS, GQA_VROWS, seq), BF16),
           jax.ShapeDtypeStruct((m, GQA_W), F32)]
    k_spec = pl.BlockSpec((None, GQA_KV_HEADS, tm, GQA_DH), lambda i: (i // nblk, 0, i % nblk, 0))
    vt_spec = pl.BlockSpec((None, GQA_KV_HEADS, GQA_VROWS, tm), lambda i: (i // nblk, 0, 0, i % nblk))
    return pl.pallas_call(
        _proj_gqa_kernel, out_shape=out, grid=(m // tm,),
        in_specs=[_row_spec(tm, D_MODEL), _const_spec(w.shape), _table_spec(tm, seq), _table_spec(tm, seq),
                  _const_spec(gmat.shape), _const_spec(qn.shape), _const_spec(kn.shape)],
        out_specs=[_row_spec(tm, GQA_W), k_spec, vt_spec, _row_spec(tm, GQA_W)],
        compiler_params=_params(("parallel",)), name="proj_gqa")(x2, w, cos, sin, gmat, qn, kn)


def _retention_kernel(dec_ref, qf_ref, kf_ref, vf_ref, qb_ref, kb_ref, vb_ref,
                      of_ref, ob_ref, sf_ref, sb_ref, *, chunks):
    c = RET_CHUNK

    @pl.when(pl.program_id(2) == 0)
    def _():
        sf_ref[...] = jnp.zeros_like(sf_ref)
        sb_ref[...] = jnp.zeros_like(sb_ref)

    lf = dec_ref[0:1, :]
    lb = dec_ref[1:2, :]
    row = lax.broadcasted_iota(jnp.int32, (c, c), 0).astype(F32)
    col = lax.broadcasted_iota(jnp.int32, (c, c), 1).astype(F32)
    diff = row - col
    d_intra = jnp.where(diff >= 0, jnp.exp(jnp.maximum(diff, 0.0) * lf), jnp.exp(jnp.maximum(-diff, 0.0) * lb))
    w_kf = jnp.exp((c - 1.0 - row) * lf)
    w_qf = jnp.exp((row + 1.0) * lf)
    w_kb = jnp.exp(row * lb)
    w_qb = jnp.exp((c - row) * lb)
    chunk_f = jnp.exp(c * lf)
    chunk_b = jnp.exp(c * lb)

    blocks = [slice(i * c, (i + 1) * c) for i in range(chunks)]

    kv_f = [_dot_tn((kf_ref[r, :].astype(F32) * w_kf).astype(BF16), vf_ref[r, :]) for r in blocks]
    kv_b = [_dot_tn((kb_ref[r, :].astype(F32) * w_kb).astype(BF16), vb_ref[r, :]) for r in blocks]
    intra = [_dot((_dot_nt(qf_ref[r, :], kf_ref[r, :]) * d_intra).astype(BF16), vf_ref[r, :]) for r in blocks]

    s_f = sf_ref[...]
    states_f = []
    for i in range(chunks):
        states_f.append(s_f)
        s_f = chunk_f * s_f + kv_f[i]
    sf_ref[...] = s_f
    s_b = sb_ref[...]
    states_b = [None] * chunks
    for i in reversed(range(chunks)):
        states_b[i] = s_b
        s_b = chunk_b * s_b + kv_b[i]
    sb_ref[...] = s_b

    for i, r in enumerate(blocks):
        qw = (qf_ref[r, :].astype(F32) * w_qf).astype(BF16)
        of_ref[r, :] = intra[i] + _dot(qw, states_f[i].astype(BF16))
        qw = (qb_ref[r, :].astype(F32) * w_qb).astype(BF16)
        ob_ref[r, :] = _dot(qw, states_b[i].astype(BF16))


def _retention(rq, rk, rv, decay, batch, seq, chunks):
    m = rq.shape[0]
    rows = chunks * RET_CHUNK
    ng = seq // rows
    fwd = pl.BlockSpec((rows, RET_DK), lambda b, h, g: (b * ng + g, h))
    bwd = pl.BlockSpec((rows, RET_DK), lambda b, h, g: (b * ng + ng - 1 - g, h))
    dec = pl.BlockSpec((None, 8, LANES), lambda b, h, g: (h, 0, 0))
    out = [jax.ShapeDtypeStruct((m, RET_W), F32)] * 2
    return pl.pallas_call(
        functools.partial(_retention_kernel, chunks=chunks), out_shape=out,
        grid=(batch, RET_HEADS, ng),
        in_specs=[dec, fwd, fwd, fwd, bwd, bwd, bwd],
        out_specs=[fwd, bwd],
        scratch_shapes=[pltpu.VMEM((RET_DK, RET_DK), F32), pltpu.VMEM((RET_DK, RET_DK), F32)],
        compiler_params=_params(("parallel", "parallel", "arbitrary")),
        name="retention")(decay, rq, rk, rv, rq, rk, rv)


NA_QROWS = 4
NA_KROWS = 3 * NA_QROWS
NA_TQ = NA_QROWS * GRID_W
NA_TK = NA_KROWS * GRID_W
NA_PAIRS = 4
NA_MASKED = -1e30


def _na_kernel(q_ref, k0_ref, k1_ref, k2_ref, v0_ref, v1_ref, v2_ref, bias_ref, z_ref, o_ref,
               s_ref, p_ref, mx_ref, den_ref):
    lane = lax.broadcasted_iota(jnp.int32, (NA_TK, LANES), 1)
    zero = jnp.zeros((NA_TK, LANES), BF16)
    units = 2 * (q_ref.shape[1] // LANES)

    def masked(refs, u):
        cs = slice((u // 2) * LANES, (u // 2 + 1) * LANES)
        x = jnp.concatenate([r[:, cs] for r in refs], axis=0)
        own = (lane < NA_DH) if u % 2 == 0 else (lane >= NA_DH)
        return jnp.where(own, x, zero)

    def scores(u):
        cs = slice((u // 2) * LANES, (u // 2 + 1) * LANES)
        s = _dot_nt(q_ref[:, cs], masked((k0_ref, k1_ref, k2_ref), u)) + bias_ref[u]
        s_ref[u] = s
        mx_ref[u] = jnp.broadcast_to(jnp.max(s, axis=-1, keepdims=True), (NA_TQ, LANES))

    def softmax(u):
        mx = mx_ref[u]
        den = None
        for j in range(NA_TK // LANES):
            cs = slice(j * LANES, (j + 1) * LANES)
            p = jnp.exp2(s_ref[u, :, cs] - mx)
            p_ref[u, :, cs] = p.astype(BF16)
            den = p if den is None else den + p
        den_ref[u] = jnp.broadcast_to(jnp.sum(den, axis=-1, keepdims=True), (NA_TQ, LANES))

    outs = {}

    def values(u):
        o = _dot(p_ref[u], masked((v0_ref, v1_ref, v2_ref), u)) / den_ref[u]
        if u % 2 == 0:
            outs[u // 2] = o
        else:
            cs = slice((u // 2) * LANES, (u // 2 + 1) * LANES)
            o_ref[:, cs] = ((outs.pop(u // 2) + o) * z_ref[:, cs]).astype(o_ref.dtype)

    for t in range(units + 2):
        if t < units:
            scores(t)
        if 1 <= t <= units:
            softmax(t - 1)
        if t >= 2:
            values(t - 2)


def _na_bias_tables(rpb, rows):
    nblk = rows // NA_QROWS
    nrow, ncol = 2 * NA_KH - 1, 2 * NA_KW - 1
    col = np.arange(GRID_W)
    c_start = np.clip(col - NA_KW // 2, 0, GRID_W - NA_KW)
    col_ok = (col[None, :] >= c_start[:, None]) & (col[None, :] < c_start[:, None] + NA_KW)
    cidx = np.clip(col[None, :] - col[:, None] + NA_KW - 1, 0, ncol - 1)
    col_sel = (cidx[:, :, None] == np.arange(ncol)).astype(np.float32)
    row_sel = np.zeros((3, NA_QROWS, NA_KROWS, nrow + 1), np.float32)
    for cls, blk in enumerate((0, 1, nblk - 1)):
        first_kblk = min(max(blk - 1, 0), nblk - 3)
        qr = blk * NA_QROWS + np.arange(NA_QROWS)
        kr = first_kblk * NA_QROWS + np.arange(NA_KROWS)
        r_start = np.clip(qr - NA_KH // 2, 0, rows - NA_KH)
        ok = (kr[None, :] >= r_start[:, None]) & (kr[None, :] < r_start[:, None] + NA_KH)
        ridx = np.where(ok, kr[None, :] - qr[:, None] + NA_KH - 1, nrow)
        row_sel[cls] = (ridx[:, :, None] == np.arange(nrow + 1))
    by_col = jnp.einsum('har,qkr->haqk', rpb.astype(F32), col_sel, precision=lax.Precision.HIGHEST)
    by_col = jnp.where(col_ok[None, None], by_col * LOG2E, NA_MASKED)
    by_col = jnp.concatenate([by_col, jnp.full((NA_HEADS, 1, GRID_W, GRID_W), NA_MASKED, F32)], axis=1)
    table = jnp.einsum('cija,haqk->chiqjk', row_sel, by_col, precision=lax.Precision.HIGHEST)
    return table.reshape(3, NA_HEADS, NA_TQ, NA_TK)


def _na_attention(nq, nk, nv, nz, bias, batch, seq):
    m = nq.shape[0]
    rows = seq // GRID_W
    nblk = rows // NA_QROWS
    groups = NA_HEADS // (2 * NA_PAIRS)
    width = NA_PAIRS * LANES

    def kblk(j):
        return lambda b, hp, i: (b * nblk + jnp.clip(i - 1, 0, nblk - 3) + j, hp)

    def bias_idx(b, hp, i):
        cls = jnp.where(i == 0, 0, jnp.where(i == nblk - 1, 2, 1))
        return (cls, hp, 0, 0)

    qspec = pl.BlockSpec((NA_TQ, width), lambda b, hp, i: (b * nblk + i, hp))
    kspecs = [pl.BlockSpec((NA_TQ, width), kblk(j)) for j in range(3)]
    return pl.pallas_call(
        _na_kernel, out_shape=jax.ShapeDtypeStruct((m, NA_W), BF16),
        grid=(batch, groups, nblk),
        in_specs=[qspec] + kspecs + kspecs + [pl.BlockSpec((None, 2 * NA_PAIRS, NA_TQ, NA_TK), bias_idx), qspec],
        out_specs=qspec,
        scratch_shapes=[pltpu.VMEM((2 * NA_PAIRS, NA_TQ, NA_TK), F32),
                        pltpu.VMEM((2 * NA_PAIRS, NA_TQ, NA_TK), BF16),
                        pltpu.VMEM((2 * NA_PAIRS, NA_TQ, LANES), F32),
                        pltpu.VMEM((2 * NA_PAIRS, NA_TQ, LANES), F32)],
        compiler_params=_params(("parallel", "parallel", "arbitrary")),
        name="na_attention")(nq, nk, nk, nk, nv, nv, nv, bias, nz)


GQA_GROUP = GQA_HEADS // GQA_KV_HEADS
GQA_VROWS = 128


def _gqa_kernel(q_ref, k_ref, vt_ref, z_ref, o_ref,
                qs_ref, s_ref, mb_ref, m_ref, acc_ref, *, tk, unroll):
    tq = q_ref.shape[0]
    nk = k_ref.shape[0] // tk
    assert nk % 2 == 0 and unroll % 2 == 0 and nk >= 1 + unroll

    def chunk_start(c):
        return c * tk if isinstance(c, int) else pl.multiple_of(c * tk, tk)
    q_t = q_ref[...].astype(F32).T.astype(BF16)
    for g in range(GQA_GROUP):
        qs_ref[:, g * tq:(g + 1) * tq] = q_t[g * GQA_DH:(g + 1) * GQA_DH, :]
    m_ref[...] = jnp.full(m_ref.shape, -jnp.inf, F32)
    acc_ref[...] = jnp.zeros_like(acc_ref)

    def scores(c, par):
        s = _dot(k_ref[pl.ds(chunk_start(c), tk), :], qs_ref[...])
        s_ref[par] = s
        mb_ref[par] = jnp.max(s, axis=0, keepdims=True)

    def softmax_values(c, par):
        m_old = m_ref[...]
        m_new = jnp.maximum(m_old, mb_ref[par])
        m_ref[...] = m_new
        p = jnp.exp2(s_ref[par] - m_new).astype(BF16)
        pv = _dot(vt_ref[:, pl.ds(chunk_start(c), tk)], p)
        acc_ref[...] = jnp.exp2(m_old - m_new) * acc_ref[...] + pv

    def tick(i, par):
        scores(i, par)
        softmax_values(i - 1, 1 - par)

    last = nk - unroll - 1
    first = 1 + ((last - 1) % unroll or unroll)
    trips = (last - first) // unroll + lax.shift_right_arithmetic(pl.program_id(2), 31)
    scores(0, 0)
    for i in range(1, first):
        tick(i, i % 2)

    def body(j, carry):
        i = first + unroll * j
        for u in range(unroll):
            tick(i + u, (first + u) % 2)
        return carry

    lax.fori_loop(0, trips, body, 0)
    for i in range(last, nk):
        tick(i, i % 2)
    softmax_values(nk - 1, 1)

    inv = 1.0 / acc_ref[GQA_DH:GQA_DH + 1, :]
    out = acc_ref[...] * inv
    heads = [out[0:GQA_DH, g * tq:(g + 1) * tq] for g in range(GQA_GROUP)]
    o_ref[...] = (jnp.concatenate(heads, axis=0).T * z_ref[...]).astype(o_ref.dtype)


def _gqa_attention(cq, k2, vt, cz, batch, seq, tq, tk, unroll):
    m = cq.shape[0]
    nq = seq // tq
    width = GQA_GROUP * GQA_DH
    cols = GQA_GROUP * tq
    qspec = pl.BlockSpec((tq, width), lambda b, n, i: (b * nq + i, n))
    return pl.pallas_call(
        functools.partial(_gqa_kernel, tk=tk, unroll=unroll), out_shape=jax.ShapeDtypeStruct((m, GQA_W), BF16),
        grid=(batch, GQA_KV_HEADS, nq),
        in_specs=[qspec,
                  pl.BlockSpec((None, None, seq, GQA_DH), lambda b, n, i: (b, n, 0, 0)),
                  pl.BlockSpec((None, None, GQA_VROWS, seq), lambda b, n, i: (b, n, 0, 0)),
                  qspec],
        out_specs=qspec,
        scratch_shapes=[pltpu.VMEM((GQA_DH, cols), BF16),
                        pltpu.VMEM((2, tk, cols), F32),
                        pltpu.VMEM((2, 1, cols), F32),
                        pltpu.VMEM((1, cols), F32),
                        pltpu.VMEM((GQA_VROWS, cols), F32)],
        compiler_params=_params(("parallel", "parallel", "arbitrary")),
        name="gqa_attention")(cq, k2, vt, cz)


def _merge_kernel(x_ref, of_ref, ob_ref, rz_ref, gn_ref, onb_ref, ogq_ref,
                  wg_ref, wa_ref, wb_ref, wc_ref, wo_ref, lg_ref, lb_ref, o_ref, *, alpha):
    x = x_ref[...]
    xb = x.astype(BF16)
    ra = of_ref[...] + ob_ref[...]
    parts = []
    for h in range(RET_HEADS):
        t = ra[:, h * RET_DK:(h + 1) * RET_DK]
        mu = jnp.mean(t, axis=-1, keepdims=True)
        d = t - mu
        var = jnp.mean(d * d, axis=-1, keepdims=True)
        parts.append(d * lax.rsqrt(var + LN_EPS))
    o_a = (jnp.concatenate(parts, axis=-1) * gn_ref[...] * rz_ref[...]).astype(BF16)
    y = _sigmoid(_dot(xb, wg_ref[:, 0:D_MODEL])) * _dot(o_a, wa_ref[...])
    y = y + _sigmoid(_dot(xb, wg_ref[:, D_MODEL:2 * D_MODEL])) * _dot(onb_ref[...], wb_ref[...])
    y = y + _sigmoid(_dot(xb, wg_ref[:, 2 * D_MODEL:3 * D_MODEL])) * _dot(ogq_ref[...], wc_ref[...])
    z = alpha * x + _dot(y.astype(BF16), wo_ref[...])
    mu = jnp.mean(z, axis=-1, keepdims=True)
    d = z - mu
    var = jnp.mean(d * d, axis=-1, keepdims=True)
    o_ref[...] = d * lax.rsqrt(var + LN_EPS) * lg_ref[...] + lb_ref[...]


def _merge(x2, o_f, o_b, rz, gn, o_nb, o_gq, wg, wa, wb, wc, wo, lg, lb, alpha, tm):
    m = x2.shape[0]
    return pl.pallas_call(
        functools.partial(_merge_kernel, alpha=alpha), out_shape=jax.ShapeDtypeStruct((m, D_MODEL), F32),
        grid=(m // tm,),
        in_specs=[_row_spec(tm, D_MODEL), _row_spec(tm, RET_W), _row_spec(tm, RET_W), _row_spec(tm, RET_W),
                  _const_spec(gn.shape), _row_spec(tm, NA_W), _row_spec(tm, GQA_W),
                  _const_spec(wg.shape), _const_spec(wa.shape), _const_spec(wb.shape), _const_spec(wc.shape),
                  _const_spec(wo.shape), _const_spec(lg.shape), _const_spec(lb.shape)],
        out_specs=_row_spec(tm, D_MODEL),
        compiler_params=_params(("parallel",)), name="merge")(
            x2, o_f, o_b, rz, gn, o_nb, o_gq, wg, wa, wb, wc, wo, lg, lb)


PROJ_TM = 1024
MERGE_TM = 512
RET_CHUNKS_PER_STEP = 8
GQA_TQ = 256
GQA_TK = 512
GQA_UNROLL = 10


def _rope_tables(seq):
    t = jnp.arange(seq)
    tf = t.astype(F32)
    row = (t // GRID_W).astype(F32)
    col = (t % GRID_W).astype(F32)
    inv_ret = 1.0 / (ROPE_THETA ** jnp.linspace(0.0, 1.0, RET_DK // 2, dtype=F32))
    ang_ret = tf[:, None] * inv_ret[None, :]
    n_ax = GQA_DH // 4
    inv_ax = ROPE_THETA ** (-jnp.arange(n_ax, dtype=F32) / n_ax)
    ang_ax = jnp.concatenate([row[:, None] * inv_ax[None, :], col[:, None] * inv_ax[None, :]], -1)

    def full(ang, reps):
        c, s = jnp.cos(ang), jnp.sin(ang)
        return jnp.tile(jnp.concatenate([c, c], -1), (1, reps)), jnp.tile(jnp.concatenate([-s, s], -1), (1, reps))

    return full(ang_ret, 1), full(ang_ax, LANES // GQA_DH)


def kernel(x, w_in, ret_theta_fwd, ret_theta_bwd, ret_gn_gain, na_rpb, gqa_q_norm, gqa_k_norm,
           w_branch_a, w_branch_b, w_branch_c, w_out, ln_gain, ln_bias):
    batch, seq, d_model = x.shape
    depth = w_in.shape[0]
    assert d_model == D_MODEL and w_in.shape[-1] == D_IN and seq % (GRID_W * NA_QROWS) == 0
    alpha = (2.0 * depth) ** 0.25
    m = batch * seq
    (ret_cos, ret_sin), (ax_cos, ax_sin) = _rope_tables(seq)
    gidx = np.arange(GQA_W) // GQA_DH
    gmat = jnp.asarray(gidx[:, None] == gidx[None, :], BF16)

    x2 = x.reshape(m, D_MODEL)
    for l in range(depth):
        w = w_in[l].astype(BF16)
        rq, rk, rv, rz = _proj_ret(x2, w[:, OFF_RET:OFF_NA], ret_cos, ret_sin, seq, PROJ_TM)
        nq, nk, nv, nz = _proj_na(x2, w[:, OFF_NA:OFF_GQA], PROJ_TM)
        qn = jnp.tile(gqa_q_norm[l], GQA_HEADS)[None, :]
        kn = jnp.tile(gqa_k_norm[l], GQA_KV_HEADS)[None, :]
        cq, ck, cvt, cz = _proj_gqa(x2, w[:, OFF_GQA:OFF_GATES], ax_cos, ax_sin, gmat, qn, kn, seq, PROJ_TM)

        decay = jnp.stack([jax.nn.log_sigmoid(ret_theta_fwd[l].astype(F32)),
                           jax.nn.log_sigmoid(ret_theta_bwd[l].astype(F32))], axis=1)
        decay = jnp.broadcast_to(jnp.pad(decay, ((0, 0), (0, 6)))[:, :, None], (RET_HEADS, 8, LANES))
        o_f, o_b = _retention(rq, rk, rv, decay, batch, seq, RET_CHUNKS_PER_STEP)

        bias = _na_bias_tables(na_rpb[l], seq // GRID_W)
        o_nb = _na_attention(nq, nk, nv, nz, bias, batch, seq)

        o_gq = _gqa_attention(cq, ck, cvt, cz, batch, seq, GQA_TQ, GQA_TK, GQA_UNROLL)

        x2 = _merge(x2, o_f, o_b, rz, ret_gn_gain[l][None, :], o_nb, o_gq,
                    w[:, OFF_GATES:D_IN], w_branch_a[l].astype(BF16), w_branch_b[l].astype(BF16),
                    w_branch_c[l].astype(BF16), w_out[l].astype(BF16),
                    ln_gain[l][None, :], ln_bias[l][None, :], alpha, MERGE_TM)
    return x2.reshape(batch, seq, D_MODEL)
```

```python
import functools
import math

import numpy as np
import jax
import jax.numpy as jnp
from jax import lax
from jax.experimental import pallas as pl
from jax.experimental.pallas import tpu as pltpu

D_MODEL = 1024
GRID_W = 64
RET_HEADS = 4
RET_DK = 128
RET_CHUNK = 128
NA_HEADS = 8
NA_DH = 64
NA_KH = 8
NA_KW = 16
GQA_HEADS = 8
GQA_KV_HEADS = 2
GQA_DH = 64
ROPE_THETA = 10000.0
RMS_EPS = 1e-6
LN_EPS = 1e-5

RET_W = RET_HEADS * RET_DK
NA_W = NA_HEADS * NA_DH
GQA_W = GQA_HEADS * GQA_DH
GQA_KV_W = GQA_KV_HEADS * GQA_DH

OFF_RET = 0
OFF_NA = 4 * RET_W
OFF_GQA = OFF_NA + 4 * NA_W
OFF_GATES = OFF_GQA + 2 * GQA_W + 2 * GQA_KV_W
D_IN = OFF_GATES + 3 * D_MODEL

LOG2E = math.log2(math.e)
LANES = 128
VMEM_LIMIT = 56 * 1024 * 1024

BF16 = jnp.bfloat16
F32 = jnp.float32


def _dot(a, b):
    return jnp.dot(a, b, preferred_element_type=F32)


def _dot_nt(a, b):
    return lax.dot_general(a, b, (((1,), (1,)), ((), ())), preferred_element_type=F32)


def _dot_tn(a, b):
    return lax.dot_general(a, b, (((0,), (0,)), ((), ())), preferred_element_type=F32)


def _silu(z):
    return z / (1.0 + jnp.exp(-z))


def _sigmoid(z):
    return 1.0 / (1.0 + jnp.exp(-z))


def _params(sem):
    return pltpu.CompilerParams(dimension_semantics=sem, vmem_limit_bytes=VMEM_LIMIT)


def _proj_ret_kernel(x_ref, w_ref, cos_ref, sin_ref, q_ref, k_ref, v_ref, z_ref):
    xb = x_ref[...]
    cos = cos_ref[...]
    sin = sin_ref[...]
    scale = RET_DK ** -0.5
    yq = _dot(xb, w_ref[:, 0:RET_W])
    yk = _dot(xb, w_ref[:, RET_W:2 * RET_W])
    for h in range(RET_HEADS):
        sl = slice(h * RET_DK, (h + 1) * RET_DK)
        t = yq[:, sl]
        q_ref[:, sl] = ((t * cos + pltpu.roll(t, RET_DK // 2, 1) * sin) * scale).astype(BF16)
        t = yk[:, sl]
        k_ref[:, sl] = (t * cos + pltpu.roll(t, RET_DK // 2, 1) * sin).astype(BF16)
    v_ref[...] = _dot(xb, w_ref[:, 2 * RET_W:3 * RET_W]).astype(BF16)
    z_ref[...] = _silu(_dot(xb, w_ref[:, 3 * RET_W:4 * RET_W]))


def _proj_na_kernel(x_ref, w_ref, q_ref, k_ref, v_ref, z_ref):
    xb = x_ref[...]
    q_ref[...] = (_dot(xb, w_ref[:, 0:NA_W]) * (LOG2E * NA_DH ** -0.5)).astype(BF16)
    k_ref[...] = _dot(xb, w_ref[:, NA_W:2 * NA_W]).astype(BF16)
    v_ref[...] = _dot(xb, w_ref[:, 2 * NA_W:3 * NA_W]).astype(BF16)
    z_ref[...] = _silu(_dot(xb, w_ref[:, 3 * NA_W:4 * NA_W]))


def _group_mean_sq(y, g_ref, width):
    sq = y * y
    hi = sq.astype(BF16)
    lo = (sq - hi.astype(F32)).astype(BF16)
    g = g_ref[0:width, 0:width]
    return (_dot(hi, g) + _dot(lo, g)) * (1.0 / GQA_DH)


def _rope64(t, cos, sin, first_half):
    swapped = jnp.where(first_half, pltpu.roll(t, LANES - GQA_DH // 2, 1), pltpu.roll(t, GQA_DH // 2, 1))
    return t * cos + swapped * sin


def _proj_gqa_kernel(x_ref, w_ref, cos_ref, sin_ref, g_ref, qn_ref, kn_ref,
                     q_ref, k_ref, vt_ref, z_ref):
    xb = x_ref[...]
    cos = cos_ref[...]
    sin = sin_ref[...]
    lane = lax.broadcasted_iota(jnp.int32, cos.shape, 1)
    first_half = (lane % GQA_DH) < (GQA_DH // 2)
    yq = _dot(xb, w_ref[:, 0:GQA_W])
    qn = yq * lax.rsqrt(_group_mean_sq(yq, g_ref, GQA_W) + RMS_EPS) * qn_ref[...]
    for j in range(GQA_W // LANES):
        sl = slice(j * LANES, (j + 1) * LANES)
        q_ref[:, sl] = (_rope64(qn[:, sl], cos, sin, first_half) * (LOG2E * GQA_DH ** -0.5)).astype(BF16)
    yk = _dot(xb, w_ref[:, GQA_W:GQA_W + GQA_KV_W])
    kn = yk * lax.rsqrt(_group_mean_sq(yk, g_ref, GQA_KV_W) + RMS_EPS) * kn_ref[...]
    kr = _rope64(kn, cos, sin, first_half).astype(BF16)
    v_t = _dot(xb, w_ref[:, GQA_W + GQA_KV_W:GQA_W + 2 * GQA_KV_W]).T
    tm = v_t.shape[1]
    pad_row = lax.broadcasted_iota(jnp.int32, (GQA_VROWS - GQA_DH, tm), 0)
    ones_then_zeros = jnp.where(pad_row == 0, 1.0, 0.0)
    for n in range(GQA_KV_HEADS):
        k_ref[n] = kr[:, n * GQA_DH:(n + 1) * GQA_DH]
        vt_ref[n] = jnp.concatenate([v_t[n * GQA_DH:(n + 1) * GQA_DH, :], ones_then_zeros], axis=0).astype(BF16)
    z_ref[...] = _silu(_dot(xb, w_ref[:, GQA_W + 2 * GQA_KV_W:2 * GQA_W + 2 * GQA_KV_W]))


def _row_spec(tm, width):
    return pl.BlockSpec((tm, width), lambda i: (i, 0))


def _const_spec(shape):
    return pl.BlockSpec(shape, lambda i: (0,) * len(shape))


def _table_spec(tm, seq):
    nblk = seq // tm
    return pl.BlockSpec((tm, LANES), lambda i: (i % nblk, 0))


def _proj_ret(x2, w, cos, sin, seq, tm):
    m = x2.shape[0]
    out = [jax.ShapeDtypeStruct((m, RET_W), BF16)] * 3 + [jax.ShapeDtypeStruct((m, RET_W), F32)]
    return pl.pallas_call(
        _proj_ret_kernel, out_shape=out, grid=(m // tm,),
        in_specs=[_row_spec(tm, D_MODEL), _const_spec(w.shape), _table_spec(tm, seq), _table_spec(tm, seq)],
        out_specs=[_row_spec(tm, RET_W)] * 4,
        compiler_params=_params(("parallel",)), name="proj_ret")(x2, w, cos, sin)


def _proj_na(x2, w, tm):
    m = x2.shape[0]
    out = [jax.ShapeDtypeStruct((m, NA_W), BF16)] * 3 + [jax.ShapeDtypeStruct((m, NA_W), F32)]
    return pl.pallas_call(
        _proj_na_kernel, out_shape=out, grid=(m // tm,),
        in_specs=[_row_spec(tm, D_MODEL), _const_spec(w.shape)],
        out_specs=[_row_spec(tm, NA_W)] * 4,
        compiler_params=_params(("parallel",)), name="proj_na")(x2, w)


def _proj_gqa(x2, w, cos, sin, gmat, qn, kn, seq, tm):
    m = x2.shape[0]
    batch, nblk = m // seq, seq // tm
    out = [jax.ShapeDtypeStruct((m, GQA_W), BF16),
           jax.ShapeDtypeStruct((batch, GQA_KV_HEADS, seq, GQA_DH), BF16),
           jax.ShapeDtypeStruct((batch, GQA_KV_HEADS, GQA_VROWS, seq), BF16),
           jax.ShapeDtypeStruct((m, GQA_W), F32)]
    k_spec = pl.BlockSpec((None, GQA_KV_HEADS, tm, GQA_DH), lambda i: (i // nblk, 0, i % nblk, 0))
    vt_spec = pl.BlockSpec((None, GQA_KV_HEADS, GQA_VROWS, tm), lambda i: (i // nblk, 0, 0, i % nblk))
    return pl.pallas_call(
        _proj_gqa_kernel, out_shape=out, grid=(m // tm,),
        in_specs=[_row_spec(tm, D_MODEL), _const_spec(w.shape), _table_spec(tm, seq), _table_spec(tm, seq),
                  _const_spec(gmat.shape), _const_spec(qn.shape), _const_spec(kn.shape)],
        out_specs=[_row_spec(tm, GQA_W), k_spec, vt_spec, _row_spec(tm, GQA_W)],
        compiler_params=_params(("parallel",)), name="proj_gqa")(x2, w, cos, sin, gmat, qn, kn)


def _retention_kernel(dec_ref, qf_ref, kf_ref, vf_ref, qb_ref, kb_ref, vb_ref,
                      of_ref, ob_ref, sf_ref, sb_ref, *, chunks):
    c = RET_CHUNK

    @pl.when(pl.program_id(2) == 0)
    def _():
        sf_ref[...] = jnp.zeros_like(sf_ref)
        sb_ref[...] = jnp.zeros_like(sb_ref)

    lf = dec_ref[0:1, :]
    lb = dec_ref[1:2, :]
    row = lax.broadcasted_iota(jnp.int32, (c, c), 0).astype(F32)
    col = lax.broadcasted_iota(jnp.int32, (c, c), 1).astype(F32)
    diff = row - col
    d_intra = jnp.where(diff >= 0, jnp.exp(jnp.maximum(diff, 0.0) * lf), jnp.exp(jnp.maximum(-diff, 0.0) * lb))
    w_kf = jnp.exp((c - 1.0 - row) * lf)
    w_qf = jnp.exp((row + 1.0) * lf)
    w_kb = jnp.exp(row * lb)
    w_qb = jnp.exp((c - row) * lb)
    chunk_f = jnp.exp(c * lf)
    chunk_b = jnp.exp(c * lb)

    blocks = [slice(i * c, (i + 1) * c) for i in range(chunks)]

    kv_f = [_dot_tn((kf_ref[r, :].astype(F32) * w_kf).astype(BF16), vf_ref[r, :]) for r in blocks]
    kv_b = [_dot_tn((kb_ref[r, :].astype(F32) * w_kb).astype(BF16), vb_ref[r, :]) for r in blocks]
    intra = [_dot((_dot_nt(qf_ref[r, :], kf_ref[r, :]) * d_intra).astype(BF16), vf_ref[r, :]) for r in blocks]

    s_f = sf_ref[...]
    states_f = []
    for i in range(chunks):
        states_f.append(s_f)
        s_f = chunk_f * s_f + kv_f[i]
    sf_ref[...] = s_f
    s_b = sb_ref[...]
    states_b = [None] * chunks
    for i in reversed(range(chunks)):
        states_b[i] = s_b
        s_b = chunk_b * s_b + kv_b[i]
    sb_ref[...] = s_b

    for i, r in enumerate(blocks):
        qw = (qf_ref[r, :].astype(F32) * w_qf).astype(BF16)
        of_ref[r, :] = intra[i] + _dot(qw, states_f[i].astype(BF16))
        qw = (qb_ref[r, :].astype(F32) * w_qb).astype(BF16)
        ob_ref[r, :] = _dot(qw, states_b[i].astype(BF16))


def _retention(rq, rk, rv, decay, batch, seq, chunks):
    m = rq.shape[0]
    rows = chunks * RET_CHUNK
    ng = seq // rows
    fwd = pl.BlockSpec((rows, RET_DK), lambda b, h, g: (b * ng + g, h))
    bwd = pl.BlockSpec((rows, RET_DK), lambda b, h, g: (b * ng + ng - 1 - g, h))
    dec = pl.BlockSpec((None, 8, LANES), lambda b, h, g: (h, 0, 0))
    out = [jax.ShapeDtypeStruct((m, RET_W), F32)] * 2
    return pl.pallas_call(
        functools.partial(_retention_kernel, chunks=chunks), out_shape=out,
        grid=(batch, RET_HEADS, ng),
        in_specs=[dec, fwd, fwd, fwd, bwd, bwd, bwd],
        out_specs=[fwd, bwd],
        scratch_shapes=[pltpu.VMEM((RET_DK, RET_DK), F32), pltpu.VMEM((RET_DK, RET_DK), F32)],
        compiler_params=_params(("parallel", "parallel", "arbitrary")),
        name="retention")(decay, rq, rk, rv, rq, rk, rv)


NA_QROWS = 4
NA_KROWS = 3 * NA_QROWS
NA_TQ = NA_QROWS * GRID_W
NA_TK = NA_KROWS * GRID_W
NA_PAIRS = 4
NA_MASKED = -1e30


def _na_kernel(q_ref, k0_ref, k1_ref, k2_ref, v0_ref, v1_ref, v2_ref, bias_ref, z_ref, o_ref,
               s_ref, p_ref, mx_ref, den_ref):
    lane = lax.broadcasted_iota(jnp.int32, (NA_TK, LANES), 1)
    zero = jnp.zeros((NA_TK, LANES), BF16)
    units = 2 * (q_ref.shape[1] // LANES)

    def masked(refs, u):
        cs = slice((u // 2) * LANES, (u // 2 + 1) * LANES)
        x = jnp.concatenate([r[:, cs] for r in refs], axis=0)
        own = (lane < NA_DH) if u % 2 == 0 else (lane >= NA_DH)
        return jnp.where(own, x, zero)

    def scores(u):
        cs = slice((u // 2) * LANES, (u // 2 + 1) * LANES)
        s = _dot_nt(q_ref[:, cs], masked((k0_ref, k1_ref, k2_ref), u)) + bias_ref[u]
        s_ref[u] = s
        mx_ref[u] = jnp.broadcast_to(jnp.max(s, axis=-1, keepdims=True), (NA_TQ, LANES))

    def softmax(u):
        mx = mx_ref[u]
        den = None
        for j in range(NA_TK // LANES):
            cs = slice(j * LANES, (j + 1) * LANES)
            p = jnp.exp2(s_ref[u, :, cs] - mx)
            p_ref[u, :, cs] = p.astype(BF16)
            den = p if den is None else den + p
        den_ref[u] = jnp.broadcast_to(jnp.sum(den, axis=-1, keepdims=True), (NA_TQ, LANES))

    outs = {}

    def values(u):
        o = _dot(p_ref[u], masked((v0_ref, v1_ref, v2_ref), u)) / den_ref[u]
        if u % 2 == 0:
            outs[u // 2] = o
        else:
            cs = slice((u // 2) * LANES, (u // 2 + 1) * LANES)
            o_ref[:, cs] = ((outs.pop(u // 2) + o) * z_ref[:, cs]).astype(o_ref.dtype)

    for t in range(units + 2):
        if t < units:
            scores(t)
        if 1 <= t <= units:
            softmax(t - 1)
        if t >= 2:
            values(t - 2)


def _na_bias_tables(rpb, rows):
    nblk = rows // NA_QROWS
    nrow, ncol = 2 * NA_KH - 1, 2 * NA_KW - 1
    col = np.arange(GRID_W)
    c_start = np.clip(col - NA_KW // 2, 0, GRID_W - NA_KW)
    col_ok = (col[None, :] >= c_start[:, None]) & (col[None, :] < c_start[:, None] + NA_KW)
    cidx = np.clip(col[None, :] - col[:, None] + NA_KW - 1, 0, ncol - 1)
    col_sel = (cidx[:, :, None] == np.arange(ncol)).astype(np.float32)
    row_sel = np.zeros((3, NA_QROWS, NA_KROWS, nrow + 1), np.float32)
    for cls, blk in enumerate((0, 1, nblk - 1)):
        first_kblk = min(max(blk - 1, 0), nblk - 3)
        qr = blk * NA_QROWS + np.arange(NA_QROWS)
        kr = first_kblk * NA_QROWS + np.arange(NA_KROWS)
        r_start = np.clip(qr - NA_KH // 2, 0, rows - NA_KH)
        ok = (kr[None, :] >= r_start[:, None]) & (kr[None, :] < r_start[:, None] + NA_KH)
        ridx = np.where(ok, kr[None, :] - qr[:, None] + NA_KH - 1, nrow)
        row_sel[cls] = (ridx[:, :, None] == np.arange(nrow + 1))
    by_col = jnp.einsum('har,qkr->haqk', rpb.astype(F32), col_sel, precision=lax.Precision.HIGHEST)
    by_col = jnp.where(col_ok[None, None], by_col * LOG2E, NA_MASKED)
    by_col = jnp.concatenate([by_col, jnp.full((NA_HEADS, 1, GRID_W, GRID_W), NA_MASKED, F32)], axis=1)
    table = jnp.einsum('cija,haqk->chiqjk', row_sel, by_col, precision=lax.Precision.HIGHEST)
    return table.reshape(3, NA_HEADS, NA_TQ, NA_TK)


def _na_attention(nq, nk, nv, nz, bias, batch, seq):
    m = nq.shape[0]
    rows = seq // GRID_W
    nblk = rows // NA_QROWS
    groups = NA_HEADS // (2 * NA_PAIRS)
    width = NA_PAIRS * LANES

    def kblk(j):
        return lambda b, hp, i: (b * nblk + jnp.clip(i - 1, 0, nblk - 3) + j, hp)

    def bias_idx(b, hp, i):
        cls = jnp.where(i == 0, 0, jnp.where(i == nblk - 1, 2, 1))
        return (cls, hp, 0, 0)

    qspec = pl.BlockSpec((NA_TQ, width), lambda b, hp, i: (b * nblk + i, hp))
    kspecs = [pl.BlockSpec((NA_TQ, width), kblk(j)) for j in range(3)]
    return pl.pallas_call(
        _na_kernel, out_shape=jax.ShapeDtypeStruct((m, NA_W), BF16),
        grid=(batch, groups, nblk),
        in_specs=[qspec] + kspecs + kspecs + [pl.BlockSpec((None, 2 * NA_PAIRS, NA_TQ, NA_TK), bias_idx), qspec],
        out_specs=qspec,
        scratch_shapes=[pltpu.VMEM((2 * NA_PAIRS, NA_TQ, NA_TK), F32),
                        pltpu.VMEM((2 * NA_PAIRS, NA_TQ, NA_TK), BF16),
                        pltpu.VMEM((2 * NA_PAIRS, NA_TQ, LANES), F32),
                        pltpu.VMEM((2 * NA_PAIRS, NA_TQ, LANES), F32)],
        compiler_params=_params(("parallel", "parallel", "arbitrary")),
        name="na_attention")(nq, nk, nk, nk, nv, nv, nv, bias, nz)


GQA_GROUP = GQA_HEADS // GQA_KV_HEADS
GQA_VROWS = 128


def _gqa_kernel(q_ref, k_ref, vt_ref, z_ref, o_ref,
                qs_ref, s_ref, mb_ref, m_ref, acc_ref, *, tk, unroll):
    tq = q_ref.shape[0]
    nk = k_ref.shape[0] // tk
    assert nk % 2 == 0 and unroll % 2 == 0 and nk >= 1 + unroll

    def chunk_start(c):
        return c * tk if isinstance(c, int) else pl.multiple_of(c * tk, tk)
    q_t = q_ref[...].astype(F32).T.astype(BF16)
    for g in range(GQA_GROUP):
        qs_ref[:, g * tq:(g + 1) * tq] = q_t[g * GQA_DH:(g + 1) * GQA_DH, :]
    m_ref[...] = jnp.full(m_ref.shape, -jnp.inf, F32)
    acc_ref[...] = jnp.zeros_like(acc_ref)

    def scores(c, par):
        s = _dot(k_ref[pl.ds(chunk_start(c), tk), :], qs_ref[...])
        s_ref[par] = s
        mb_ref[par] = jnp.max(s, axis=0, keepdims=True)

    def softmax_values(c, par):
        m_old = m_ref[...]
        m_new = jnp.maximum(m_old, mb_ref[par])
        m_ref[...] = m_new
        p = jnp.exp2(s_ref[par] - m_new).astype(BF16)
        pv = _dot(vt_ref[:, pl.ds(chunk_start(c), tk)], p)
        acc_ref[...] = jnp.exp2(m_old - m_new) * acc_ref[...] + pv

    def tick(i, par):
        scores(i, par)
        softmax_values(i - 1, 1 - par)

    last = nk - unroll - 1
    first = 1 + ((last - 1) % unroll or unroll)
    trips = (last - first) // unroll + lax.shift_right_arithmetic(pl.program_id(2), 31)
    scores(0, 0)
    for i in range(1, first):
        tick(i, i % 2)

    def body(j, carry):
        i = first + unroll * j
        for u in range(unroll):
            tick(i + u, (first + u) % 2)
        return carry

    lax.fori_loop(0, trips, body, 0)
    for i in range(last, nk):
        tick(i, i % 2)
    softmax_values(nk - 1, 1)

    inv = 1.0 / acc_ref[GQA_DH:GQA_DH + 1, :]
    out = acc_ref[...] * inv
    heads = [out[0:GQA_DH, g * tq:(g + 1) * tq] for g in range(GQA_GROUP)]
    o_ref[...] = (jnp.concatenate(heads, axis=0).T * z_ref[...]).astype(o_ref.dtype)


def _gqa_attention(cq, k2, vt, cz, batch, seq, tq, tk, unroll):
    m = cq.shape[0]
    nq = seq // tq
    width = GQA_GROUP * GQA_DH
    cols = GQA_GROUP * tq
    qspec = pl.BlockSpec((tq, width), lambda b, n, i: (b * nq + i, n))
    return pl.pallas_call(
        functools.partial(_gqa_kernel, tk=tk, unroll=unroll), out_shape=jax.ShapeDtypeStruct((m, GQA_W), BF16),
        grid=(batch, GQA_KV_HEADS, nq),
        in_specs=[qspec,
                  pl.BlockSpec((None, None, seq, GQA_DH), lambda b, n, i: (b, n, 0, 0)),
                  pl.BlockSpec((None, None, GQA_VROWS, seq), lambda b, n, i: (b, n, 0, 0)),
                  qspec],
        out_specs=qspec,
        scratch_shapes=[pltpu.VMEM((GQA_DH, cols), BF16),
                        pltpu.VMEM((2, tk, cols), F32),
                        pltpu.VMEM((2, 1, cols), F32),
                        pltpu.VMEM((1, cols), F32),
                        pltpu.VMEM((GQA_VROWS, cols), F32)],
        compiler_params=_params(("parallel", "parallel", "arbitrary")),
        name="gqa_attention")(cq, k2, vt, cz)


def _merge_kernel(x_ref, of_ref, ob_ref, rz_ref, gn_ref, onb_ref, ogq_ref,
                  wg_ref, wa_ref, wb_ref, wc_ref, wo_ref, lg_ref, lb_ref, o_ref, ob16_ref, *, alpha):
    x = x_ref[...]
    xb = x.astype(BF16)
    ra = of_ref[...] + ob_ref[...]
    parts = []
    for h in range(RET_HEADS):
        t = ra[:, h * RET_DK:(h + 1) * RET_DK]
        mu = jnp.mean(t, axis=-1, keepdims=True)
        d = t - mu
        var = jnp.mean(d * d, axis=-1, keepdims=True)
        parts.append(d * lax.rsqrt(var + LN_EPS))
    o_a = (jnp.concatenate(parts, axis=-1) * gn_ref[...] * rz_ref[...]).astype(BF16)
    y = _sigmoid(_dot(xb, wg_ref[:, 0:D_MODEL])) * _dot(o_a, wa_ref[...])
    y = y + _sigmoid(_dot(xb, wg_ref[:, D_MODEL:2 * D_MODEL])) * _dot(onb_ref[...], wb_ref[...])
    y = y + _sigmoid(_dot(xb, wg_ref[:, 2 * D_MODEL:3 * D_MODEL])) * _dot(ogq_ref[...], wc_ref[...])
    z = alpha * x + _dot(y.astype(BF16), wo_ref[...])
    mu = jnp.mean(z, axis=-1, keepdims=True)
    d = z - mu
    var = jnp.mean(d * d, axis=-1, keepdims=True)
    out = d * lax.rsqrt(var + LN_EPS) * lg_ref[...] + lb_ref[...]
    o_ref[...] = out
    ob16_ref[...] = out.astype(BF16)


def _merge(x2, o_f, o_b, rz, gn, o_nb, o_gq, wg, wa, wb, wc, wo, lg, lb, alpha, tm):
    m = x2.shape[0]
    return pl.pallas_call(
        functools.partial(_merge_kernel, alpha=alpha),
        out_shape=[jax.ShapeDtypeStruct((m, D_MODEL), F32), jax.ShapeDtypeStruct((m, D_MODEL), BF16)],
        grid=(m // tm,),
        in_specs=[_row_spec(tm, D_MODEL), _row_spec(tm, RET_W), _row_spec(tm, RET_W), _row_spec(tm, RET_W),
                  _const_spec(gn.shape), _row_spec(tm, NA_W), _row_spec(tm, GQA_W),
                  _const_spec(wg.shape), _const_spec(wa.shape), _const_spec(wb.shape), _const_spec(wc.shape),
                  _const_spec(wo.shape), _const_spec(lg.shape), _const_spec(lb.shape)],
        out_specs=[_row_spec(tm, D_MODEL), _row_spec(tm, D_MODEL)],
        compiler_params=_params(("parallel",)), name="merge")(
            x2, o_f, o_b, rz, gn, o_nb, o_gq, wg, wa, wb, wc, wo, lg, lb)


PROJ_TM = 1024
MERGE_TM = 512
RET_CHUNKS_PER_STEP = 8
GQA_TQ = 256
GQA_TK = 512
GQA_UNROLL = 10


def _rope_tables(seq):
    t = jnp.arange(seq)
    tf = t.astype(F32)
    row = (t // GRID_W).astype(F32)
    col = (t % GRID_W).astype(F32)
    inv_ret = 1.0 / (ROPE_THETA ** jnp.linspace(0.0, 1.0, RET_DK // 2, dtype=F32))
    ang_ret = tf[:, None] * inv_ret[None, :]
    n_ax = GQA_DH // 4
    inv_ax = ROPE_THETA ** (-jnp.arange(n_ax, dtype=F32) / n_ax)
    ang_ax = jnp.concatenate([row[:, None] * inv_ax[None, :], col[:, None] * inv_ax[None, :]], -1)

    def full(ang, reps):
        c, s = jnp.cos(ang), jnp.sin(ang)
        return jnp.tile(jnp.concatenate([c, c], -1), (1, reps)), jnp.tile(jnp.concatenate([-s, s], -1), (1, reps))

    return full(ang_ret, 1), full(ang_ax, LANES // GQA_DH)


def kernel(x, w_in, ret_theta_fwd, ret_theta_bwd, ret_gn_gain, na_rpb, gqa_q_norm, gqa_k_norm,
           w_branch_a, w_branch_b, w_branch_c, w_out, ln_gain, ln_bias):
    batch, seq, d_model = x.shape
    depth = w_in.shape[0]
    assert d_model == D_MODEL and w_in.shape[-1] == D_IN and seq % (GRID_W * NA_QROWS) == 0
    alpha = (2.0 * depth) ** 0.25
    m = batch * seq
    (ret_cos, ret_sin), (ax_cos, ax_sin) = _rope_tables(seq)
    gidx = np.arange(GQA_W) // GQA_DH
    gmat = jnp.asarray(gidx[:, None] == gidx[None, :], BF16)

    x2 = x.reshape(m, D_MODEL)
    xb2 = x2.astype(BF16)
    for l in range(depth):
        w = w_in[l].astype(BF16)
        rq, rk, rv, rz = _proj_ret(xb2, w[:, OFF_RET:OFF_NA], ret_cos, ret_sin, seq, PROJ_TM)
        nq, nk, nv, nz = _proj_na(xb2, w[:, OFF_NA:OFF_GQA], PROJ_TM)
        qn = jnp.tile(gqa_q_norm[l], GQA_HEADS)[None, :]
        kn = jnp.tile(gqa_k_norm[l], GQA_KV_HEADS)[None, :]
        cq, ck, cvt, cz = _proj_gqa(xb2, w[:, OFF_GQA:OFF_GATES], ax_cos, ax_sin, gmat, qn, kn, seq, PROJ_TM)

        decay = jnp.stack([jax.nn.log_sigmoid(ret_theta_fwd[l].astype(F32)),
                           jax.nn.log_sigmoid(ret_theta_bwd[l].astype(F32))], axis=1)
        decay = jnp.broadcast_to(jnp.pad(decay, ((0, 0), (0, 6)))[:, :, None], (RET_HEADS, 8, LANES))
        o_f, o_b = _retention(rq, rk, rv, decay, batch, seq, RET_CHUNKS_PER_STEP)

        bias = _na_bias_tables(na_rpb[l], seq // GRID_W)
        o_nb = _na_attention(nq, nk, nv, nz, bias, batch, seq)

        o_gq = _gqa_attention(cq, ck, cvt, cz, batch, seq, GQA_TQ, GQA_TK, GQA_UNROLL)

        x2, xb2 = _merge(x2, o_f, o_b, rz, ret_gn_gain[l][None, :], o_nb, o_gq,
                    w[:, OFF_GATES:D_IN], w_branch_a[l].astype(BF16), w_branch_b[l].astype(BF16),
                    w_branch_c[l].astype(BF16), w_out[l].astype(BF16),
                    ln_gain[l][None, :], ln_bias[l][None, :], alpha, MERGE_TM)
    return x2.reshape(batch, seq, D_MODEL)
```
